```python
import math
import jax
import jax.numpy as jnp
from jax import lax
import numpy as np

D_MODEL = 1024
BATCH = 8
SEQ = 8192
DEPTH = 2
DEC_BATCH = 8
DEC_SEQ = 2048
PAST_LEN = 128

N_MIXERS = 2
N_A_LAYERS = (DEPTH + N_MIXERS - 1) // N_MIXERS
N_B_LAYERS = DEPTH // N_MIXERS

A_WINDOWS = (128, 512, 2048)
A_DILATIONS = (1, 4, 16)
A_GROUPS = len(A_WINDOWS)
A_HEADS = 8
A_HEAD_DIM = D_MODEL // A_HEADS
A_WIDTH = A_HEADS * A_HEAD_DIM

NUM_BUCKETS = 32
MAX_DISTANCE = max(A_WINDOWS) // 2

B_HEADS = 16
Q_LORA = 384
KV_LORA = 256
QK_NOPE = 64
QK_ROPE = 32
V_HEAD = 64
ROPE_THETA = 10000.0
Q_BLOCK = 128

D_FF = 2816
CONV_W = 3

ALPHA = (2.0 * DEPTH) ** 0.25
BETA = (8.0 * DEPTH) ** -0.25

LN_EPS = 1e-5
RMS_EPS = 1e-6
NEG_INF = -1e30

kernel_name = "hybrid_dilated_mla_encoder"


def layer_norm(x, g, b):
    xf = x.astype(jnp.float32)
    mu = jnp.mean(xf, axis=-1, keepdims=True)
    xc = xf - mu
    var = jnp.mean(xc * xc, axis=-1, keepdims=True)
    return (xc * lax.rsqrt(var + LN_EPS) * g.astype(jnp.float32) + b.astype(jnp.float32)).astype(x.dtype)


def rms_norm(x, g):
    xf = x.astype(jnp.float32)
    ms = jnp.mean(xf * xf, axis=-1, keepdims=True)
    return (xf * lax.rsqrt(ms + RMS_EPS) * g.astype(jnp.float32)).astype(x.dtype)


def t5_bucket(rel):
    nb = NUM_BUCKETS // 2
    max_exact = nb // 2
    ret = jnp.where(rel > 0, nb, 0)
    n = jnp.abs(rel)
    large = max_exact + (jnp.log(jnp.maximum(n, 1).astype(jnp.float32) / max_exact)
                         / math.log(MAX_DISTANCE / max_exact) * (nb - max_exact)).astype(jnp.int32)
    large = jnp.minimum(large, nb - 1)
    return ret + jnp.where(n < max_exact, n, large)


def dilated_group_attention(q, k, v, bias_tab, dilation, half):
    B, S, H, C = q.shape
    L = S // dilation
    nblk = -(-L // half)
    Lp = nblk * half

    def to_sub(t):
        t = t.reshape(B, L, dilation, H, C).transpose(0, 2, 1, 3, 4)
        return jnp.pad(t, ((0, 0), (0, 0), (0, Lp - L), (0, 0), (0, 0)))

    def neighbours(t):
        t = jnp.pad(to_sub(t), ((0, 0), (0, 0), (half, half), (0, 0), (0, 0)))
        t = t.reshape(B, dilation, nblk + 2, half, H, C)
        return jnp.concatenate([t[:, :, :-2], t[:, :, 1:-1], t[:, :, 2:]], axis=3)

    qs = to_sub(q).reshape(B, dilation, nblk, half, H, C)
    ks = neighbours(k)
    vs = neighbours(v)

    qi = jnp.arange(half)[:, None]
    kj = jnp.arange(3 * half)[None, :]
    rel = kj - half - qi
    bias = bias_tab[t5_bucket(rel * dilation)].transpose(2, 0, 1).astype(jnp.float32)
    key_pos = (jnp.arange(nblk)[:, None] - 1) * half + kj
    valid = (jnp.abs(rel) <= half)[None] & ((key_pos >= 0) & (key_pos < L))[:, None, :]

    s = jnp.einsum('brnqhc,brnkhc->brnhqk', qs, ks).astype(jnp.float32) * (C ** -0.5) + bias
    s = jnp.where(valid[:, None], s, NEG_INF)
    m = jnp.max(s, axis=-1, keepdims=True)
    p = jnp.exp(s - m)
    den = jnp.sum(p, axis=-1, keepdims=True)
    lse = (m + jnp.log(den))[..., 0]
    o = jnp.einsum('brnhqk,brnkhc->brnqhc', (p / den).astype(v.dtype), vs)

    def from_sub(t):
        t = t.reshape((B, dilation, Lp) + t.shape[4:])[:, :, :L]
        t = jnp.moveaxis(t, 1, 2)
        return t.reshape((B, S) + t.shape[3:])

    return from_sub(o), from_sub(lse.transpose(0, 1, 2, 4, 3))


def dilated_mixer(x, w_qkv, w_o, rel_bias):
    B, S, _ = x.shape
    qkv = (x @ w_qkv).reshape(B, S, A_GROUPS, 3, A_HEADS, A_HEAD_DIM)
    outs, lses = [], []
    for g in range(A_GROUPS):
        dil = A_DILATIONS[g]
        half = A_WINDOWS[g] // (2 * dil)
        o, lse = dilated_group_attention(qkv[:, :, g, 0], qkv[:, :, g, 1], qkv[:, :, g, 2],
                                         rel_bias[:, g * A_HEADS:(g + 1) * A_HEADS], dil, half)
        outs.append(o)
        lses.append(lse)
    wts = jax.nn.softmax(jnp.stack(lses, axis=0), axis=0)
    o = jnp.einsum('gbsh,gbshc->bshc', wts.astype(x.dtype), jnp.stack(outs, axis=0))
    return o.reshape(B, S, A_WIDTH) @ w_o


def rope_tables(S):
    inv = 1.0 / (ROPE_THETA ** (jnp.arange(0, QK_ROPE, 2, dtype=jnp.float32) / QK_ROPE))
    ang = jnp.arange(S, dtype=jnp.float32)[:, None] * inv[None, :]
    return jnp.cos(ang), jnp.sin(ang)


def apply_rope(t, cos, sin):
    t1, t2 = jnp.split(t, 2, axis=-1)
    cos = cos.astype(t.dtype)
    sin = sin.astype(t.dtype)
    return jnp.concatenate([t1 * cos - t2 * sin, t1 * sin + t2 * cos], axis=-1)


def mla_mixer(x, w_dkv, g_q, g_kv, w_uq, w_ukv, w_o):
    B, S, _ = x.shape
    c = x @ w_dkv
    cq = c[..., :Q_LORA]
    ckv = c[..., Q_LORA:Q_LORA + KV_LORA]
    kr = c[..., Q_LORA + KV_LORA:]
    q = (rms_norm(cq, g_q) @ w_uq).reshape(B, S, B_HEADS, QK_NOPE + QK_ROPE)
    kv = (rms_norm(ckv, g_kv) @ w_ukv).reshape(B, S, B_HEADS, QK_NOPE + V_HEAD)
    qn, qr = q[..., :QK_NOPE], q[..., QK_NOPE:]
    kn, v = kv[..., :QK_NOPE], kv[..., QK_NOPE:]
    cos, sin = rope_tables(S)
    qr = apply_rope(qr, cos[:, None], sin[:, None])
    kr = apply_rope(kr, cos, sin)
    scale = (QK_NOPE + QK_ROPE) ** -0.5
    nqb = S // Q_BLOCK
    qn_b = qn.reshape(B, nqb, Q_BLOCK, B_HEADS, QK_NOPE).transpose(1, 0, 2, 3, 4)
    qr_b = qr.reshape(B, nqb, Q_BLOCK, B_HEADS, QK_ROPE).transpose(1, 0, 2, 3, 4)

    def attend(blk):
        qn_i, qr_i = blk
        s = (jnp.einsum('bqhc,bkhc->bhqk', qn_i, kn)
             + jnp.einsum('bqhr,bkr->bhqk', qr_i, kr)).astype(jnp.float32) * scale
        p = jax.nn.softmax(s, axis=-1)
        return jnp.einsum('bhqk,bkhc->bqhc', p.astype(v.dtype), v)

    o = lax.map(attend, (qn_b, qr_b))
    o = o.transpose(1, 0, 2, 3, 4).reshape(B, S, B_HEADS * V_HEAD)
    return o @ w_o


def conv_ffn(x, w_in, conv_w, conv_b, w_out):
    h = x @ w_in
    hp = jnp.pad(h, ((0, 0), (1, 1), (0, 0)))
    h = hp[:, :-2] * conv_w[0] + hp[:, 1:-1] * conv_w[1] + hp[:, 2:] * conv_w[2] + conv_b
    a, g = jnp.split(h, 2, axis=-1)
    return (a * jax.nn.gelu(g, approximate=False)) @ w_out


def encoder(x, rel_bias, w_qkv_a, w_o_a, w_dkv_b, g_q_b, g_kv_b, w_uq_b, w_ukv_b, w_o_b,
            ffn_w_in, ffn_conv_w, ffn_conv_b, ffn_w_out, ln_g, ln_b):
    for i in range(DEPTH):
        j = i // N_MIXERS
        if i % N_MIXERS == 0:
            mix = dilated_mixer(x, w_qkv_a[j], w_o_a[j], rel_bias)
        else:
            mix = mla_mixer(x, w_dkv_b[j], g_q_b[j], g_kv_b[j], w_uq_b[j], w_ukv_b[j], w_o_b[j])
        x = layer_norm(ALPHA * x + mix, ln_g[i, 0], ln_b[i, 0])
        x = layer_norm(ALPHA * x + conv_ffn(x, ffn_w_in[i], ffn_conv_w[i], ffn_conv_b[i], ffn_w_out[i]),
                       ln_g[i, 1], ln_b[i, 1])
    return x


def setup_inputs(seed: int = 0) -> dict:
    key = jax.random.key(seed)
    ks = jax.random.split(key, 20)
    f32 = jnp.float32

    def nrm(k, shape, fan_in, gain=1.0):
        return jax.random.normal(k, shape, f32) * (gain * fan_in ** -0.5)

    x_prompt = jax.random.normal(ks[0], (BATCH, SEQ, D_MODEL), f32)
    x_sample = jax.random.normal(ks[1], (DEC_BATCH, DEC_SEQ, D_MODEL), f32)
    rel_bias = jax.random.normal(ks[2], (NUM_BUCKETS, A_GROUPS * A_HEADS), f32) * 0.5
    w_qkv_a = nrm(ks[3], (N_A_LAYERS, D_MODEL, A_GROUPS * 3 * A_HEADS * A_HEAD_DIM), D_MODEL)
    w_o_a = nrm(ks[4], (N_A_LAYERS, A_WIDTH, D_MODEL), A_WIDTH, BETA)
    w_dkv_b = nrm(ks[5], (N_B_LAYERS, D_MODEL, Q_LORA + KV_LORA + QK_ROPE), D_MODEL)
    g_q_b = 1.0 + 0.05 * jax.random.normal(ks[6], (N_B_LAYERS, Q_LORA), f32)
    g_kv_b = 1.0 + 0.05 * jax.random.normal(ks[7], (N_B_LAYERS, KV_LORA), f32)
    w_uq_b = nrm(ks[8], (N_B_LAYERS, Q_LORA, B_HEADS * (QK_NOPE + QK_ROPE)), Q_LORA)
    w_ukv_b = nrm(ks[9], (N_B_LAYERS, KV_LORA, B_HEADS * (QK_NOPE + V_HEAD)), KV_LORA)
    w_o_b = nrm(ks[10], (N_B_LAYERS, B_HEADS * V_HEAD, D_MODEL), B_HEADS * V_HEAD, BETA)
    ffn_w_in = nrm(ks[11], (DEPTH, D_MODEL, 2 * D_FF), D_MODEL)
    ffn_conv_w = nrm(ks[12], (DEPTH, CONV_W, 2 * D_FF), CONV_W)
    ffn_conv_b = 0.02 * jax.random.normal(ks[13], (DEPTH, 2 * D_FF), f32)
    ffn_w_out = nrm(ks[14], (DEPTH, D_FF, D_MODEL), D_FF, BETA)
    ln_g = 1.0 + 0.05 * jax.random.normal(ks[15], (DEPTH, 2, D_MODEL), f32)
    ln_b = 0.02 * jax.random.normal(ks[16], (DEPTH, 2, D_MODEL), f32)
    return {"x_prompt": x_prompt, "x_sample": x_sample, "rel_bias": rel_bias,
            "w_qkv_a": w_qkv_a, "w_o_a": w_o_a, "w_dkv_b": w_dkv_b, "g_q_b": g_q_b,
            "g_kv_b": g_kv_b, "w_uq_b": w_uq_b, "w_ukv_b": w_ukv_b, "w_o_b": w_o_b,
            "ffn_w_in": ffn_w_in, "ffn_conv_w": ffn_conv_w, "ffn_conv_b": ffn_conv_b,
            "ffn_w_out": ffn_w_out, "ln_g": ln_g, "ln_b": ln_b}


def reference(x_prompt, x_sample, rel_bias, w_qkv_a, w_o_a, w_dkv_b, g_q_b, g_kv_b, w_uq_b,
              w_ukv_b, w_o_b, ffn_w_in, ffn_conv_w, ffn_conv_b, ffn_w_out, ln_g, ln_b):
    y_prompt = encoder(x_prompt, rel_bias, w_qkv_a, w_o_a, w_dkv_b, g_q_b, g_kv_b, w_uq_b, w_ukv_b,
                       w_o_b, ffn_w_in, ffn_conv_w, ffn_conv_b, ffn_w_out, ln_g, ln_b)
    y_sample = encoder(x_sample, rel_bias, w_qkv_a, w_o_a, w_dkv_b, g_q_b, g_kv_b, w_uq_b, w_ukv_b,
                       w_o_b, ffn_w_in, ffn_conv_w, ffn_conv_b, ffn_w_out, ln_g, ln_b)
    return (y_prompt, y_sample)
```

```python
import functools
import math

import jax
import jax.numpy as jnp
import numpy as np
from jax import lax
from jax.experimental import pallas as pl
from jax.experimental.pallas import tpu as pltpu

F32 = jnp.float32
BF16 = jnp.bfloat16

D_MODEL = 1024
DEPTH = 2
N_MIXERS = 2

A_WINDOWS = (128, 512, 2048)
A_DILATIONS = (1, 4, 16)
A_GROUPS = 3
A_HEADS = 8
A_HEAD_DIM = 128
A_WIDTH = A_HEADS * A_HEAD_DIM
A_HALF = 64
NUM_BUCKETS = 32
MAX_DISTANCE = max(A_WINDOWS) // 2

B_HEADS = 16
Q_LORA = 384
KV_LORA = 256
QK_NOPE = 64
QK_ROPE = 32
V_HEAD = 64
ROPE_THETA = 10000.0
HEAD_PAD = 128

D_FF = 2816
FF_CHUNK = 256

ALPHA = (2.0 * DEPTH) ** 0.25
LN_EPS = 1e-5
RMS_EPS = 1e-6
NEG_INF = -1e30

LANES = 128
SUBLANES = 8
VMEM_LIMIT = 56 * 1024 * 1024

LOG2E = math.log2(math.e)


def _cparams(sem):
    return pltpu.CompilerParams(dimension_semantics=sem, vmem_limit_bytes=VMEM_LIMIT)


def _resident(shape):
    nd = len(shape)
    return pl.BlockSpec(shape, lambda *_: (0,) * nd, pipeline_mode=pl.Buffered(1))


def _layer_norm(y, g, b):
    mu = jnp.mean(y, axis=-1, keepdims=True)
    yc = y - mu
    var = jnp.mean(yc * yc, axis=-1, keepdims=True)
    return yc * lax.rsqrt(var + LN_EPS) * g + b


def _dot(a, b):
    return jnp.dot(a, b, preferred_element_type=F32)


def _dot_nt(a, b):
    return lax.dot_general(a, b, (((1,), (1,)), ((), ())), preferred_element_type=F32)


def _dot_tn(a, b):
    return lax.dot_general(a, b, (((0,), (0,)), ((), ())), preferred_element_type=F32)


def _qkv_proj_kernel(x_ref, w_ref, o_ref, *, n_chunk):
    xb = x_ref[...].astype(BF16)
    n_total = w_ref.shape[1]
    for n0 in range(0, n_total, n_chunk):
        o_ref[:, n0:n0 + n_chunk] = _dot(xb, w_ref[:, n0:n0 + n_chunk]).astype(BF16)


def _qkv_proj(x2d, w_bf16, tm):
    t, d = x2d.shape
    n = w_bf16.shape[1]
    return pl.pallas_call(
        functools.partial(_qkv_proj_kernel, n_chunk=1024),
        grid=(t // tm,),
        in_specs=[pl.BlockSpec((tm, d), lambda i: (i, 0)), _resident((d, n))],
        out_specs=pl.BlockSpec((tm, n), lambda i: (i, 0)),
        out_shape=jax.ShapeDtypeStruct((t, n), BF16),
        compiler_params=_cparams(("parallel",)),
        name="qkv_proj",
    )(x2d, w_bf16)


def _t5_bucket(rel):
    nb = NUM_BUCKETS // 2
    max_exact = nb // 2
    ret = jnp.where(rel > 0, nb, 0)
    n = jnp.abs(rel)
    large = max_exact + (jnp.log(jnp.maximum(n, 1).astype(F32) / max_exact)
                         / math.log(MAX_DISTANCE / max_exact) * (nb - max_exact)).astype(jnp.int32)
    large = jnp.minimum(large, nb - 1)
    return ret + jnp.where(n < max_exact, n, large)


def _band_bucket_index():
    qi = jnp.arange(2 * A_HALF)[:, None]
    kj = jnp.arange(4 * A_HALF)[None, :]
    rel = kj - A_HALF - qi
    tiles = [jnp.where(jnp.abs(rel) <= A_HALF, _t5_bucket(rel * dil), -1) for dil in A_DILATIONS]
    return jnp.stack(tiles, axis=0).astype(jnp.int32)


def _bias_tile_kernel(tab_ref, idx_ref, o_ref):
    col = pl.program_id(0) * A_HEADS + pl.program_id(1)
    idx = idx_ref[0]
    acc = jnp.full(idx.shape, NEG_INF, F32)
    for bucket in range(NUM_BUCKETS):
        acc = jnp.where(idx == bucket, tab_ref[bucket, col], acc)
    o_ref[0, 0] = acc


def _bias_tiles(rel_bias):
    idx = _band_bucket_index()
    tq, tk = idx.shape[1:]
    return pl.pallas_call(
        _bias_tile_kernel,
        grid=(A_GROUPS, A_HEADS),
        in_specs=[pl.BlockSpec(memory_space=pltpu.SMEM),
                  pl.BlockSpec((1, tq, tk), lambda g, h: (g, 0, 0))],
        out_specs=pl.BlockSpec((1, 1, tq, tk), lambda g, h: (g, h, 0, 0)),
        out_shape=jax.ShapeDtypeStruct((A_GROUPS, A_HEADS, tq, tk), F32),
        compiler_params=_cparams(("arbitrary", "arbitrary")),
        name="bias_tiles",
    )(rel_bias, idx)


def _dilated_attn_kernel(q_ref, kl_ref, km_ref, kr_ref, vl_ref, vm_ref, vr_ref, bias_ref,
                         o_ref, lse_ref, kbuf, vbuf, *, tq, sub_len):
    li = pl.program_id(2)
    kbuf[0:A_HALF] = kl_ref[0]
    kbuf[A_HALF:A_HALF + tq] = km_ref[0]
    kbuf[A_HALF + tq:] = kr_ref[0]
    vbuf[0:A_HALF] = vl_ref[0]
    vbuf[A_HALF:A_HALF + tq] = vm_ref[0]
    vbuf[A_HALF + tq:] = vr_ref[0]

    scale = A_HEAD_DIM ** -0.5
    qt, kt = 2 * A_HALF, 4 * A_HALF
    lane = lax.broadcasted_iota(jnp.int32, (qt, LANES), 1)
    for t in range(tq // qt):
        pos = li * tq + t * qt - A_HALF + lax.broadcasted_iota(jnp.int32, (1, kt), 1)
        edge = jnp.where((pos >= 0) & (pos < sub_len), 0.0, NEG_INF).astype(F32)
        lse_tile = jnp.zeros((qt, LANES), F32)
        for h in range(A_HEADS):
            cols = slice(h * A_HEAD_DIM, (h + 1) * A_HEAD_DIM)
            q = q_ref[0, t * qt:(t + 1) * qt, cols]
            k = kbuf[t * qt:t * qt + kt, cols]
            v = vbuf[t * qt:t * qt + kt, cols]
            s = _dot_nt(q, k) * scale + bias_ref[0, h] + edge
            m = jnp.max(s, axis=-1, keepdims=True)
            p = jnp.exp(s - m)
            den = jnp.sum(p, axis=-1, keepdims=True)
            o = _dot(p.astype(BF16), v) / den
            o_ref[0, t * qt:(t + 1) * qt, cols] = o
            lse_tile = jnp.where(lane == h, m + jnp.log(den), lse_tile)
        lse_ref[0, t * qt:(t + 1) * qt, :] = lse_tile


def _dilated_attn(qkv, bias, g, batch, seq):
    dil = A_DILATIONS[g]
    sub_len = seq // dil
    tq = min(256, sub_len)
    n_t = sub_len // tq
    hb = tq // A_HALF
    n_hb = sub_len // A_HALF
    n_col = qkv.shape[1] // A_WIDTH
    qkv_v = qkv.reshape(batch, sub_len, dil * qkv.shape[1])

    def col(which):
        return lambda b, r, i: (b, i, r * n_col + g * 3 + which)

    def col_left(which):
        return lambda b, r, i: (b, jnp.maximum(i * hb - 1, 0), r * n_col + g * 3 + which)

    def col_right(which):
        return lambda b, r, i: (b, jnp.minimum((i + 1) * hb, n_hb - 1), r * n_col + g * 3 + which)

    main = lambda which: pl.BlockSpec((1, tq, A_WIDTH), col(which))
    left = lambda which: pl.BlockSpec((1, A_HALF, A_WIDTH), col_left(which))
    right = lambda which: pl.BlockSpec((1, A_HALF, A_WIDTH), col_right(which))

    o, lse = pl.pallas_call(
        functools.partial(_dilated_attn_kernel, tq=tq, sub_len=sub_len),
        grid=(batch, dil, n_t),
        in_specs=[main(0), left(1), main(1), right(1), left(2), main(2), right(2),
                  pl.BlockSpec((1, A_HEADS, 2 * A_HALF, 4 * A_HALF), lambda b, r, i: (g, 0, 0, 0))],
        out_specs=[pl.BlockSpec((1, tq, A_WIDTH), lambda b, r, i: (b, i, r)),
                   pl.BlockSpec((1, tq, LANES), lambda b, r, i: (b, i, r))],
        out_shape=[jax.ShapeDtypeStruct((batch, sub_len, dil * A_WIDTH), F32),
                   jax.ShapeDtypeStruct((batch, sub_len, dil * LANES), F32)],
        scratch_shapes=[pltpu.VMEM((tq + 2 * A_HALF, A_WIDTH), BF16),
                        pltpu.VMEM((tq + 2 * A_HALF, A_WIDTH), BF16)],
        compiler_params=_cparams(("parallel", "parallel", "parallel")),
        name=f"dilated_attn_g{g}",
    )(qkv_v, qkv_v, qkv_v, qkv_v, qkv_v, qkv_v, qkv_v, bias)
    return o.reshape(batch * seq, A_WIDTH), lse.reshape(batch * seq, LANES)


def _merge_proj_ln_kernel(o0_ref, o1_ref, o2_ref, l0_ref, l1_ref, l2_ref, x_ref, w_ref, g_ref, b_ref,
                          out_ref, merged):
    l0, l1, l2 = l0_ref[...], l1_ref[...], l2_ref[...]
    m = jnp.maximum(jnp.maximum(l0, l1), l2)
    e0, e1, e2 = jnp.exp(l0 - m), jnp.exp(l1 - m), jnp.exp(l2 - m)
    den = e0 + e1 + e2
    w0, w1, w2 = e0 / den, e1 / den, e2 / den
    for h in range(A_HEADS):
        cols = slice(h * A_HEAD_DIM, (h + 1) * A_HEAD_DIM)
        mix = (w0[:, h:h + 1] * o0_ref[:, cols] + w1[:, h:h + 1] * o1_ref[:, cols]
               + w2[:, h:h + 1] * o2_ref[:, cols])
        merged[:, cols] = mix.astype(BF16)
    y = ALPHA * x_ref[...] + _dot(merged[...], w_ref[...])
    out_ref[...] = _layer_norm(y, g_ref[...], b_ref[...])


def _merge_proj_ln(os, lses, x2d, w_bf16, ln_g, ln_b, tm):
    t, d = x2d.shape
    row = lambda w: pl.BlockSpec((tm, w), lambda i: (i, 0))
    return pl.pallas_call(
        _merge_proj_ln_kernel,
        grid=(t // tm,),
        in_specs=[row(A_WIDTH)] * 3 + [row(LANES)] * 3 + [row(d), _resident((A_WIDTH, d)),
                                                           _resident((1, d)), _resident((1, d))],
        out_specs=row(d),
        out_shape=jax.ShapeDtypeStruct((t, d), F32),
        scratch_shapes=[pltpu.VMEM((tm, A_WIDTH), BF16)],
        compiler_params=_cparams(("parallel",)),
        name="merge_proj_ln",
    )(*os, *lses, x2d, w_bf16, ln_g, ln_b)


def _ffn_ln_kernel(x_ref, xp_ref, xn_ref, win_ref, cw_ref, cb_ref, wout_ref, g_ref, b_ref,
                   out_ref, act, *, tm):
    si = pl.program_id(1)
    n_s = pl.num_programs(1)
    x = x_ref[0]
    xb = x.astype(BF16)
    prev_ok = (si > 0).astype(F32)
    next_ok = (si < n_s - 1).astype(F32)
    xe = jnp.concatenate([xp_ref[0] * prev_ok, xn_ref[0] * next_ok], axis=0).astype(BF16)
    row = lax.broadcasted_iota(jnp.int32, (tm, 2 * FF_CHUNK), 0)
    for c in range(D_FF // FF_CHUNK):
        cols = slice(c * 2 * FF_CHUNK, (c + 1) * 2 * FF_CHUNK)
        w = win_ref[:, cols]
        h = _dot(xb, w)
        he = _dot(xe, w)
        up = jnp.where(row == 0, he[SUBLANES - 1:SUBLANES], pltpu.roll(h, 1, 0))
        dn = jnp.where(row == tm - 1, he[SUBLANES:SUBLANES + 1], pltpu.roll(h, tm - 1, 0))
        cw = cw_ref[:, cols]
        hc = up * cw[0:1] + h * cw[1:2] + dn * cw[2:3] + cb_ref[:, cols]
        a = hc[:, :FF_CHUNK]
        gate = hc[:, FF_CHUNK:]
        gelu = 0.5 * gate * (1.0 + lax.erf(gate * np.float32(math.sqrt(0.5))))
        act[:, c * FF_CHUNK:(c + 1) * FF_CHUNK] = (a * gelu).astype(BF16)
    y = ALPHA * x + _dot(act[...], wout_ref[...])
    out_ref[0] = _layer_norm(y, g_ref[...], b_ref[...])


def _chunk_interleave(w):
    lead = w.shape[:-1]
    w = w.reshape(lead + (2, D_FF // FF_CHUNK, FF_CHUNK))
    w = jnp.swapaxes(w, -3, -2)
    return w.reshape(lead + (2 * D_FF,))


def _ffn_ln(x3d, w_in, conv_w, conv_b, w_out, ln_g, ln_b, tm):
    b, s, d = x3d.shape
    nb8 = tm // SUBLANES
    n8 = s // SUBLANES
    win = _chunk_interleave(w_in).astype(BF16)
    cw = _chunk_interleave(conv_w)
    cb = _chunk_interleave(conv_b)[None]
    return pl.pallas_call(
        functools.partial(_ffn_ln_kernel, tm=tm),
        grid=(b, s // tm),
        in_specs=[pl.BlockSpec((1, tm, d), lambda bi, si: (bi, si, 0)),
                  pl.BlockSpec((1, SUBLANES, d), lambda bi, si: (bi, jnp.maximum(si * nb8 - 1, 0), 0)),
                  pl.BlockSpec((1, SUBLANES, d), lambda bi, si: (bi, jnp.minimum((si + 1) * nb8, n8 - 1), 0)),
                  _resident((d, 2 * D_FF)), _resident((3, 2 * D_FF)), _resident((1, 2 * D_FF)),
                  _resident((D_FF, d)), _resident((1, d)), _resident((1, d))],
        out_specs=pl.BlockSpec((1, tm, d), lambda bi, si: (bi, si, 0)),
        out_shape=jax.ShapeDtypeStruct((b, s, d), F32),
        scratch_shapes=[pltpu.VMEM((tm, D_FF), BF16)],
        compiler_params=_cparams(("parallel", "parallel")),
        name="ffn_ln",
    )(x3d, x3d, x3d, win, cw, cb, w_out.astype(BF16), ln_g, ln_b)


def _rope_tables(seq):
    inv = 1.0 / (ROPE_THETA ** (jnp.arange(0, QK_ROPE, 2, dtype=F32) / QK_ROPE))
    ang = jnp.arange(seq, dtype=F32)[:, None] * inv[None, :]
    cos, sin = jnp.cos(ang), jnp.sin(ang)
    half = QK_ROPE // 2
    ones = jnp.ones((seq, QK_NOPE), F32)
    z_nope = jnp.zeros((seq, QK_NOPE), F32)
    z_half = jnp.zeros((seq, half), F32)
    z_pad = jnp.zeros((seq, HEAD_PAD - QK_NOPE - QK_ROPE), F32)
    cmul = jnp.concatenate([ones, cos, cos, z_pad], axis=1)
    s_first = jnp.concatenate([z_nope, -sin, z_half, z_pad], axis=1)
    s_second = jnp.concatenate([z_nope, z_half, sin, z_pad], axis=1)
    return cmul, s_first, s_second


def _rope(t, cmul, s_first, s_second):
    half = QK_ROPE // 2
    return (t * cmul + pltpu.roll(t, HEAD_PAD - half, 1) * s_first
            + pltpu.roll(t, half, 1) * s_second)


def _rms_norm(x, g):
    ms = jnp.mean(x * x, axis=-1, keepdims=True)
    return x * lax.rsqrt(ms + RMS_EPS) * g


def _mla_proj_kernel(x_ref, wd_ref, gq_ref, gkv_ref, wuq_ref, wuk_ref, wuvt_ref,
                     cm_ref, s1_ref, s2_ref, q_ref, k_ref, vt_ref):
    xb = x_ref[0].astype(BF16)
    c = _dot(xb, wd_ref[...])
    cqn = _rms_norm(c[:, :Q_LORA], gq_ref[...]).astype(BF16)
    ckvn = _rms_norm(c[:, Q_LORA:Q_LORA + KV_LORA], gkv_ref[...]).astype(BF16)
    cm, s1, s2 = cm_ref[...], s1_ref[...], s2_ref[...]
    kr = _rope(c[:, Q_LORA + KV_LORA:], cm, s1, s2)
    q = _dot(cqn, wuq_ref[...])
    k = _dot(ckvn, wuk_ref[...])
    for h in range(B_HEADS):
        cols = slice(h * HEAD_PAD, (h + 1) * HEAD_PAD)
        q_ref[0, :, cols] = _rope(q[:, cols], cm, s1, s2).astype(BF16)
        k_ref[0, :, cols] = (k[:, cols] + kr).astype(BF16)
    vt_ref[0] = _dot_nt(wuvt_ref[...], ckvn).astype(BF16)


def _mla_weights(w_dkv, w_uq, w_ukv):
    d = w_dkv.shape[0]
    pad_l = jnp.zeros((d, QK_NOPE), F32)
    pad_r = jnp.zeros((d, HEAD_PAD - QK_NOPE - QK_ROPE), F32)
    wd = jnp.concatenate([w_dkv[:, :Q_LORA + KV_LORA], pad_l, w_dkv[:, Q_LORA + KV_LORA:], pad_r], axis=1)
    wq = w_uq.reshape(Q_LORA, B_HEADS, QK_NOPE + QK_ROPE)
    wq = jnp.pad(wq, ((0, 0), (0, 0), (0, HEAD_PAD - QK_NOPE - QK_ROPE))).reshape(Q_LORA, B_HEADS * HEAD_PAD)
    wkv = w_ukv.reshape(KV_LORA, B_HEADS, QK_NOPE + V_HEAD)
    wk = jnp.pad(wkv[:, :, :QK_NOPE], ((0, 0), (0, 0), (0, HEAD_PAD - QK_NOPE))).reshape(KV_LORA, B_HEADS * HEAD_PAD)
    wvt = wkv[:, :, QK_NOPE:].reshape(KV_LORA, B_HEADS * V_HEAD).T
    return wd.astype(BF16), wq.astype(BF16), wk.astype(BF16), wvt.astype(BF16)


def _mla_proj(x3d, w_dkv, g_q, g_kv, w_uq, w_ukv, tm):
    b, s, d = x3d.shape
    wd, wq, wk, wvt = _mla_weights(w_dkv, w_uq, w_ukv)
    cm, s1, s2 = _rope_tables(s)
    hq = B_HEADS * HEAD_PAD
    hv = B_HEADS * V_HEAD
    tab = pl.BlockSpec((tm, HEAD_PAD), lambda bi, si: (si, 0))
    return pl.pallas_call(
        _mla_proj_kernel,
        grid=(b, s // tm),
        in_specs=[pl.BlockSpec((1, tm, d), lambda bi, si: (bi, si, 0)),
                  _resident(wd.shape), _resident((1, Q_LORA)), _resident((1, KV_LORA)),
                  _resident(wq.shape), _resident(wk.shape), _resident(wvt.shape), tab, tab, tab],
        out_specs=[pl.BlockSpec((1, tm, hq), lambda bi, si: (bi, si, 0)),
                   pl.BlockSpec((1, tm, hq), lambda bi, si: (bi, si, 0)),
                   pl.BlockSpec((1, hv, tm), lambda bi, si: (bi, 0, si))],
        out_shape=[jax.ShapeDtypeStruct((b, s, hq), BF16),
                   jax.ShapeDtypeStruct((b, s, hq), BF16),
                   jax.ShapeDtypeStruct((b, hv, s), BF16)],
        compiler_params=_cparams(("parallel", "parallel")),
        name="mla_proj",
    )(x3d, wd, g_q[None], g_kv[None], wq, wk, wvt, cm, s1, s2)


def _mla_attn_kernel(q_ref, k_ref, vt_ref, o_ref, *, tq, tk):
    seq = k_ref.shape[1]
    c = np.float32((QK_NOPE + QK_ROPE) ** -0.5 * LOG2E)
    for qs in range(q_ref.shape[1] // tq):
        q = q_ref[0, qs * tq:(qs + 1) * tq, :]

        def body(i, carry):
            m, l, acc = carry
            k0 = pl.multiple_of(i * tk, tk)
            s = _dot_nt(k_ref[0, pl.ds(k0, tk), :], q)
            m_new = jnp.maximum(m, jnp.max(s, axis=0, keepdims=True))
            alpha = jnp.exp2((m - m_new) * c)
            p = jnp.exp2((s - m_new) * c)
            l = alpha * l + jnp.sum(p, axis=0, keepdims=True)
            acc = alpha * acc + _dot(vt_ref[0, :, pl.ds(k0, tk)], p.astype(BF16))
            return m_new, l, acc

        init = (jnp.full((1, tq), NEG_INF, F32), jnp.zeros((1, tq), F32), jnp.zeros((V_HEAD, tq), F32))
        _, l, acc = lax.fori_loop(0, seq // tk, body, init)
        o_ref[0, :, qs * tq:(qs + 1) * tq] = (acc / l).astype(BF16)


def _mla_attn(q, k, vt, tq_block, tq, tk):
    b, s, _ = q.shape
    return pl.pallas_call(
        functools.partial(_mla_attn_kernel, tq=tq, tk=tk),
        grid=(b, B_HEADS, s // tq_block),
        in_specs=[pl.BlockSpec((1, tq_block, HEAD_PAD), lambda bi, h, qi: (bi, qi, h)),
                  pl.BlockSpec((1, s, HEAD_PAD), lambda bi, h, qi: (bi, 0, h)),
                  pl.BlockSpec((1, V_HEAD, s), lambda bi, h, qi: (bi, h, 0))],
        out_specs=pl.BlockSpec((1, V_HEAD, tq_block), lambda bi, h, qi: (bi, h, qi)),
        out_shape=jax.ShapeDtypeStruct((b, B_HEADS * V_HEAD, s), BF16),
        compiler_params=_cparams(("parallel", "parallel", "arbitrary")),
        name="mla_attn",
    )(q, k, vt)


def _proj_t_ln_kernel(ot_ref, x_ref, w_ref, g_ref, b_ref, out_ref):
    y = ALPHA * x_ref[0] + _dot_tn(ot_ref[0], w_ref[...])
    out_ref[0] = _layer_norm(y, g_ref[...], b_ref[...])


def _proj_t_ln(ot, x3d, w_bf16, ln_g, ln_b, tm):
    b, s, d = x3d.shape
    k = ot.shape[1]
    return pl.pallas_call(
        _proj_t_ln_kernel,
        grid=(b, s // tm),
        in_specs=[pl.BlockSpec((1, k, tm), lambda bi, si: (bi, 0, si)),
                  pl.BlockSpec((1, tm, d), lambda bi, si: (bi, si, 0)),
                  _resident((k, d)), _resident((1, d)), _resident((1, d))],
        out_specs=pl.BlockSpec((1, tm, d), lambda bi, si: (bi, si, 0)),
        out_shape=jax.ShapeDtypeStruct((b, s, d), F32),
        compiler_params=_cparams(("parallel", "parallel")),
        name="proj_t_ln",
    )(ot, x3d, w_bf16, ln_g, ln_b)


def _encoder(x, bias, w_qkv_a, w_o_a, w_dkv_b, g_q_b, g_kv_b, w_uq_b, w_ukv_b, w_o_b,
             ffn_w_in, ffn_conv_w, ffn_conv_b, ffn_w_out, ln_g, ln_b):
    b, s, d = x.shape
    tm = 512
    for i in range(DEPTH):
        j = i // N_MIXERS
        g1, b1 = ln_g[i, 0][None], ln_b[i, 0][None]
        g2, b2 = ln_g[i, 1][None], ln_b[i, 1][None]
        if i % N_MIXERS == 0:
            x2d = x.reshape(b * s, d)
            qkv = _qkv_proj(x2d, w_qkv_a[j].astype(BF16), tm=256)
            outs = [_dilated_attn(qkv, bias, g, b, s) for g in range(A_GROUPS)]
            x = _merge_proj_ln([o for o, _ in outs], [l for _, l in outs], x2d,
                               w_o_a[j].astype(BF16), g1, b1, tm).reshape(b, s, d)
        else:
            q, k, vt = _mla_proj(x, w_dkv_b[j], g_q_b[j], g_kv_b[j], w_uq_b[j], w_ukv_b[j], tm)
            ot = _mla_attn(q, k, vt, tq_block=min(1024, s), tq=256, tk=256)
            x = _proj_t_ln(ot, x, w_o_b[j].astype(BF16), g1, b1, tm)
        x = _ffn_ln(x, ffn_w_in[i], ffn_conv_w[i], ffn_conv_b[i], ffn_w_out[i], g2, b2, tm)
    return x


def kernel(x_prompt, x_sample, rel_bias, w_qkv_a, w_o_a, w_dkv_b, g_q_b, g_kv_b, w_uq_b, w_ukv_b, w_o_b,
           ffn_w_in, ffn_conv_w, ffn_conv_b, ffn_w_out, ln_g, ln_b):
    bias = _bias_tiles(rel_bias)
    args = (bias, w_qkv_a, w_o_a, w_dkv_b, g_q_b, g_kv_b, w_uq_b, w_ukv_b, w_o_b,
            ffn_w_in, ffn_conv_w, ffn_conv_b, ffn_w_out, ln_g, ln_b)
    return (_encoder(x_prompt, *args), _encoder(x_sample, *args))
```

```python
import functools
import math

import jax
import jax.numpy as jnp
import numpy as np
from jax import lax
from jax.experimental import pallas as pl
from jax.experimental.pallas import tpu as pltpu

F32 = jnp.float32
BF16 = jnp.bfloat16

D_MODEL = 1024
DEPTH = 2
N_MIXERS = 2

A_WINDOWS = (128, 512, 2048)
A_DILATIONS = (1, 4, 16)
A_GROUPS = 3
A_HEADS = 8
A_HEAD_DIM = 128
A_WIDTH = A_HEADS * A_HEAD_DIM
A_HALF = 64
NUM_BUCKETS = 32
MAX_DISTANCE = max(A_WINDOWS) // 2

B_HEADS = 16
Q_LORA = 384
KV_LORA = 256
QK_NOPE = 64
QK_ROPE = 32
V_HEAD = 64
ROPE_THETA = 10000.0
HEAD_PAD = 128
V_ROWS = V_HEAD + 16

D_FF = 2816
FF_CHUNK = 256

ALPHA = (2.0 * DEPTH) ** 0.25
LN_EPS = 1e-5
RMS_EPS = 1e-6
NEG_INF = -1e30

LANES = 128
SUBLANES = 8
VMEM_LIMIT = 56 * 1024 * 1024

QK_SCALE_LOG2E = np.float32((QK_NOPE + QK_ROPE) ** -0.5 * math.log2(math.e))


def _cparams(sem):
    return pltpu.CompilerParams(dimension_semantics=sem, vmem_limit_bytes=VMEM_LIMIT)


def _resident(shape):
    nd = len(shape)
    return pl.BlockSpec(shape, lambda *_: (0,) * nd, pipeline_mode=pl.Buffered(1))


def _layer_norm(y, g, b):
    mu = jnp.mean(y, axis=-1, keepdims=True)
    yc = y - mu
    var = jnp.mean(yc * yc, axis=-1, keepdims=True)
    return yc * lax.rsqrt(var + LN_EPS) * g + b


def _dot(a, b):
    return jnp.dot(a, b, preferred_element_type=F32)


def _dot_nt(a, b):
    return lax.dot_general(a, b, (((1,), (1,)), ((), ())), preferred_element_type=F32)


def _dot_tn(a, b):
    return lax.dot_general(a, b, (((0,), (0,)), ((), ())), preferred_element_type=F32)


def _qkv_proj_kernel(x_ref, w_ref, o_ref, *, n_chunk):
    xb = x_ref[...].astype(BF16)
    n_total = w_ref.shape[1]
    for n0 in range(0, n_total, n_chunk):
        o_ref[:, n0:n0 + n_chunk] = _dot(xb, w_ref[:, n0:n0 + n_chunk]).astype(BF16)


def _qkv_proj(x2d, w_bf16, tm):
    t, d = x2d.shape
    n = w_bf16.shape[1]
    return pl.pallas_call(
        functools.partial(_qkv_proj_kernel, n_chunk=1024),
        grid=(t // tm,),
        in_specs=[pl.BlockSpec((tm, d), lambda i: (i, 0)), _resident((d, n))],
        out_specs=pl.BlockSpec((tm, n), lambda i: (i, 0)),
        out_shape=jax.ShapeDtypeStruct((t, n), BF16),
        compiler_params=_cparams(("parallel",)),
        name="qkv_proj",
    )(x2d, w_bf16)


def _t5_bucket(rel):
    nb = NUM_BUCKETS // 2
    max_exact = nb // 2
    ret = jnp.where(rel > 0, nb, 0)
    n = jnp.abs(rel)
    large = max_exact + (jnp.log(jnp.maximum(n, 1).astype(F32) / max_exact)
                         / math.log(MAX_DISTANCE / max_exact) * (nb - max_exact)).astype(jnp.int32)
    large = jnp.minimum(large, nb - 1)
    return ret + jnp.where(n < max_exact, n, large)


def _band_bucket_index():
    qi = jnp.arange(2 * A_HALF)[:, None]
    kj = jnp.arange(4 * A_HALF)[None, :]
    rel = kj - A_HALF - qi
    tiles = [jnp.where(jnp.abs(rel) <= A_HALF, _t5_bucket(rel * dil), -1) for dil in A_DILATIONS]
    return jnp.stack(tiles, axis=0).astype(jnp.int32)


def _bias_tile_kernel(tab_ref, idx_ref, o_ref):
    col = pl.program_id(0) * A_HEADS + pl.program_id(1)
    idx = idx_ref[0]
    acc = jnp.full(idx.shape, NEG_INF, F32)
    for bucket in range(NUM_BUCKETS):
        acc = jnp.where(idx == bucket, tab_ref[bucket, col], acc)
    o_ref[0, 0] = acc


def _bias_tiles(rel_bias):
    idx = _band_bucket_index()
    tq, tk = idx.shape[1:]
    return pl.pallas_call(
        _bias_tile_kernel,
        grid=(A_GROUPS, A_HEADS),
        in_specs=[pl.BlockSpec(memory_space=pltpu.SMEM),
                  pl.BlockSpec((1, tq, tk), lambda g, h: (g, 0, 0))],
        out_specs=pl.BlockSpec((1, 1, tq, tk), lambda g, h: (g, h, 0, 0)),
        out_shape=jax.ShapeDtypeStruct((A_GROUPS, A_HEADS, tq, tk), F32),
        compiler_params=_cparams(("arbitrary", "arbitrary")),
        name="bias_tiles",
    )(rel_bias, idx)


def _dilated_attn_kernel(q_ref, kl_ref, km_ref, kr_ref, vl_ref, vm_ref, vr_ref, bias_ref,
                         o_ref, lse_ref, kbuf, vbuf, *, tq, sub_len):
    li = pl.program_id(2)
    kbuf[0:A_HALF] = kl_ref[0]
    kbuf[A_HALF:A_HALF + tq] = km_ref[0]
    kbuf[A_HALF + tq:] = kr_ref[0]
    vbuf[0:A_HALF] = vl_ref[0]
    vbuf[A_HALF:A_HALF + tq] = vm_ref[0]
    vbuf[A_HALF + tq:] = vr_ref[0]

    scale = A_HEAD_DIM ** -0.5
    qt, kt = 2 * A_HALF, 4 * A_HALF
    lane = lax.broadcasted_iota(jnp.int32, (qt, LANES), 1)
    for t in range(tq // qt):
        pos = li * tq + t * qt - A_HALF + lax.broadcasted_iota(jnp.int32, (1, kt), 1)
        edge = jnp.where((pos >= 0) & (pos < sub_len), 0.0, NEG_INF).astype(F32)
        lse_tile = jnp.zeros((qt, LANES), F32)
        for h in range(A_HEADS):
            cols = slice(h * A_HEAD_DIM, (h + 1) * A_HEAD_DIM)
            q = q_ref[0, t * qt:(t + 1) * qt, cols]
            k = kbuf[t * qt:t * qt + kt, cols]
            v = vbuf[t * qt:t * qt + kt, cols]
            s = _dot_nt(q, k) * scale + bias_ref[0, h] + edge
            m = jnp.max(s, axis=-1, keepdims=True)
            p = jnp.exp(s - m)
            den = jnp.sum(p, axis=-1, keepdims=True)
            o = _dot(p.astype(BF16), v) / den
            o_ref[0, t * qt:(t + 1) * qt, cols] = o
            lse_tile = jnp.where(lane == h, m + jnp.log(den), lse_tile)
        lse_ref[0, t * qt:(t + 1) * qt, :] = lse_tile


def _dilated_attn(qkv, bias, g, batch, seq):
    dil = A_DILATIONS[g]
    sub_len = seq // dil
    tq = min(256, sub_len)
    n_t = sub_len // tq
    hb = tq // A_HALF
    n_hb = sub_len // A_HALF
    n_col = qkv.shape[1] // A_WIDTH
    qkv_v = qkv.reshape(batch, sub_len, dil * qkv.shape[1])

    def col(which):
        return lambda b, r, i: (b, i, r * n_col + g * 3 + which)

    def col_left(which):
        return lambda b, r, i: (b, jnp.maximum(i * hb - 1, 0), r * n_col + g * 3 + which)

    def col_right(which):
        return lambda b, r, i: (b, jnp.minimum((i + 1) * hb, n_hb - 1), r * n_col + g * 3 + which)

    main = lambda which: pl.BlockSpec((1, tq, A_WIDTH), col(which))
    left = lambda which: pl.BlockSpec((1, A_HALF, A_WIDTH), col_left(which))
    right = lambda which: pl.BlockSpec((1, A_HALF, A_WIDTH), col_right(which))

    o, lse = pl.pallas_call(
        functools.partial(_dilated_attn_kernel, tq=tq, sub_len=sub_len),
        grid=(batch, dil, n_t),
        in_specs=[main(0), left(1), main(1), right(1), left(2), main(2), right(2),
                  pl.BlockSpec((1, A_HEADS, 2 * A_HALF, 4 * A_HALF), lambda b, r, i: (g, 0, 0, 0))],
        out_specs=[pl.BlockSpec((1, tq, A_WIDTH), lambda b, r, i: (b, i, r)),
                   pl.BlockSpec((1, tq, LANES), lambda b, r, i: (b, i, r))],
        out_shape=[jax.ShapeDtypeStruct((batch, sub_len, dil * A_WIDTH), F32),
                   jax.ShapeDtypeStruct((batch, sub_len, dil * LANES), F32)],
        scratch_shapes=[pltpu.VMEM((tq + 2 * A_HALF, A_WIDTH), BF16),
                        pltpu.VMEM((tq + 2 * A_HALF, A_WIDTH), BF16)],
        compiler_params=_cparams(("parallel", "parallel", "parallel")),
        name=f"dilated_attn_g{g}",
    )(qkv_v, qkv_v, qkv_v, qkv_v, qkv_v, qkv_v, qkv_v, bias)
    return o.reshape(batch * seq, A_WIDTH), lse.reshape(batch * seq, LANES)


def _merge_proj_ln_kernel(o0_ref, o1_ref, o2_ref, l0_ref, l1_ref, l2_ref, x_ref, w_ref, g_ref, b_ref,
                          out_ref, merged):
    l0, l1, l2 = l0_ref[...], l1_ref[...], l2_ref[...]
    m = jnp.maximum(jnp.maximum(l0, l1), l2)
    e0, e1, e2 = jnp.exp(l0 - m), jnp.exp(l1 - m), jnp.exp(l2 - m)
    den = e0 + e1 + e2
    w0, w1, w2 = e0 / den, e1 / den, e2 / den
    for h in range(A_HEADS):
        cols = slice(h * A_HEAD_DIM, (h + 1) * A_HEAD_DIM)
        mix = (w0[:, h:h + 1] * o0_ref[:, cols] + w1[:, h:h + 1] * o1_ref[:, cols]
               + w2[:, h:h + 1] * o2_ref[:, cols])
        merged[:, cols] = mix.astype(BF16)
    y = ALPHA * x_ref[...] + _dot(merged[...], w_ref[...])
    out_ref[...] = _layer_norm(y, g_ref[...], b_ref[...])


def _merge_proj_ln(os, lses, x2d, w_bf16, ln_g, ln_b, tm):
    t, d = x2d.shape
    row = lambda w: pl.BlockSpec((tm, w), lambda i: (i, 0))
    return pl.pallas_call(
        _merge_proj_ln_kernel,
        grid=(t // tm,),
        in_specs=[row(A_WIDTH)] * 3 + [row(LANES)] * 3 + [row(d), _resident((A_WIDTH, d)),
                                                           _resident((1, d)), _resident((1, d))],
        out_specs=row(d),
        out_shape=jax.ShapeDtypeStruct((t, d), F32),
        scratch_shapes=[pltpu.VMEM((tm, A_WIDTH), BF16)],
        compiler_params=_cparams(("parallel",)),
        name="merge_proj_ln",
    )(*os, *lses, x2d, w_bf16, ln_g, ln_b)


def _ffn_ln_kernel(x_ref, xp_ref, xn_ref, win_ref, cw_ref, cb_ref, wout_ref, g_ref, b_ref,
                   out_ref, act, *, tm):
    si = pl.program_id(1)
    n_s = pl.num_programs(1)
    x = x_ref[0]
    xb = x.astype(BF16)
    prev_ok = (si > 0).astype(F32)
    next_ok = (si < n_s - 1).astype(F32)
    xe = jnp.concatenate([xp_ref[0] * prev_ok, xn_ref[0] * next_ok], axis=0).astype(BF16)
    row = lax.broadcasted_iota(jnp.int32, (tm, 2 * FF_CHUNK), 0)
    for c in range(D_FF // FF_CHUNK):
        cols = slice(c * 2 * FF_CHUNK, (c + 1) * 2 * FF_CHUNK)
        w = win_ref[:, cols]
        h = _dot(xb, w)
        he = _dot(xe, w)
        up = jnp.where(row == 0, he[SUBLANES - 1:SUBLANES], pltpu.roll(h, 1, 0))
        dn = jnp.where(row == tm - 1, he[SUBLANES:SUBLANES + 1], pltpu.roll(h, tm - 1, 0))
        cw = cw_ref[:, cols]
        hc = up * cw[0:1] + h * cw[1:2] + dn * cw[2:3] + cb_ref[:, cols]
        a = hc[:, :FF_CHUNK]
        gate = hc[:, FF_CHUNK:]
        gelu = 0.5 * gate * (1.0 + lax.erf(gate * np.float32(math.sqrt(0.5))))
        act[:, c * FF_CHUNK:(c + 1) * FF_CHUNK] = (a * gelu).astype(BF16)
    y = ALPHA * x + _dot(act[...], wout_ref[...])
    out_ref[0] = _layer_norm(y, g_ref[...], b_ref[...])


def _chunk_interleave(w):
    lead = w.shape[:-1]
    w = w.reshape(lead + (2, D_FF // FF_CHUNK, FF_CHUNK))
    w = jnp.swapaxes(w, -3, -2)
    return w.reshape(lead + (2 * D_FF,))


def _ffn_ln(x3d, w_in, conv_w, conv_b, w_out, ln_g, ln_b, tm):
    b, s, d = x3d.shape
    nb8 = tm // SUBLANES
    n8 = s // SUBLANES
    win = _chunk_interleave(w_in).astype(BF16)
    cw = _chunk_interleave(conv_w)
    cb = _chunk_interleave(conv_b)[None]
    return pl.pallas_call(
        functools.partial(_ffn_ln_kernel, tm=tm),
        grid=(b, s // tm),
        in_specs=[pl.BlockSpec((1, tm, d), lambda bi, si: (bi, si, 0)),
                  pl.BlockSpec((1, SUBLANES, d), lambda bi, si: (bi, jnp.maximum(si * nb8 - 1, 0), 0)),
                  pl.BlockSpec((1, SUBLANES, d), lambda bi, si: (bi, jnp.minimum((si + 1) * nb8, n8 - 1), 0)),
                  _resident((d, 2 * D_FF)), _resident((3, 2 * D_FF)), _resident((1, 2 * D_FF)),
                  _resident((D_FF, d)), _resident((1, d)), _resident((1, d))],
        out_specs=pl.BlockSpec((1, tm, d), lambda bi, si: (bi, si, 0)),
        out_shape=jax.ShapeDtypeStruct((b, s, d), F32),
        scratch_shapes=[pltpu.VMEM((tm, D_FF), BF16)],
        compiler_params=_cparams(("parallel", "parallel")),
        name="ffn_ln",
    )(x3d, x3d, x3d, win, cw, cb, w_out.astype(BF16), ln_g, ln_b)


def _rope_tables(seq):
    inv = 1.0 / (ROPE_THETA ** (jnp.arange(0, QK_ROPE, 2, dtype=F32) / QK_ROPE))
    ang = jnp.arange(seq, dtype=F32)[:, None] * inv[None, :]
    cos, sin = jnp.cos(ang), jnp.sin(ang)
    half = QK_ROPE // 2
    ones = jnp.ones((seq, QK_NOPE), F32)
    z_nope = jnp.zeros((seq, QK_NOPE), F32)
    z_half = jnp.zeros((seq, half), F32)
    z_pad = jnp.zeros((seq, HEAD_PAD - QK_NOPE - QK_ROPE), F32)
    cmul = jnp.concatenate([ones, cos, cos, z_pad], axis=1)
    s_first = jnp.concatenate([z_nope, -sin, z_half, z_pad], axis=1)
    s_second = jnp.concatenate([z_nope, z_half, sin, z_pad], axis=1)
    return cmul, s_first, s_second


def _rope(t, cmul, s_first, s_second):
    half = QK_ROPE // 2
    return (t * cmul + pltpu.roll(t, HEAD_PAD - half, 1) * s_first
            + pltpu.roll(t, half, 1) * s_second)


def _rms_norm(x, g):
    ms = jnp.mean(x * x, axis=-1, keepdims=True)
    return x * lax.rsqrt(ms + RMS_EPS) * g


def _mla_proj_kernel(x_ref, wd_ref, gq_ref, gkv_ref, wuq_ref, wuk_ref, wuvt_ref,
                     cm_ref, s1_ref, s2_ref, q_ref, k_ref, vt_ref):
    xb = x_ref[0].astype(BF16)
    c = _dot(xb, wd_ref[...])
    cqn = _rms_norm(c[:, :Q_LORA], gq_ref[...]).astype(BF16)
    ckvn = _rms_norm(c[:, Q_LORA:Q_LORA + KV_LORA], gkv_ref[...]).astype(BF16)
    cm, s1, s2 = cm_ref[...], s1_ref[...], s2_ref[...]
    kr = _rope(c[:, Q_LORA + KV_LORA:], cm, s1, s2)
    q = _dot(cqn, wuq_ref[...])
    k = _dot(ckvn, wuk_ref[...])
    for h in range(B_HEADS):
        cols = slice(h * HEAD_PAD, (h + 1) * HEAD_PAD)
        q_ref[0, :, cols] = (_rope(q[:, cols], cm, s1, s2) * QK_SCALE_LOG2E).astype(BF16)
        k_ref[0, :, cols] = (k[:, cols] + kr).astype(BF16)
    vt = _dot_nt(wuvt_ref[...], ckvn).astype(BF16)
    tm = vt.shape[1]
    row = lax.broadcasted_iota(jnp.int32, (V_ROWS - V_HEAD, tm), 0)
    ones_row = jnp.where(row == 0, 1.0, 0.0).astype(BF16)
    for h in range(B_HEADS):
        vt_ref[0, h * V_ROWS:h * V_ROWS + V_HEAD, :] = vt[h * V_HEAD:(h + 1) * V_HEAD]
        vt_ref[0, h * V_ROWS + V_HEAD:(h + 1) * V_ROWS, :] = ones_row


def _mla_weights(w_dkv, w_uq, w_ukv):
    d = w_dkv.shape[0]
    pad_l = jnp.zeros((d, QK_NOPE), F32)
    pad_r = jnp.zeros((d, HEAD_PAD - QK_NOPE - QK_ROPE), F32)
    wd = jnp.concatenate([w_dkv[:, :Q_LORA + KV_LORA], pad_l, w_dkv[:, Q_LORA + KV_LORA:], pad_r], axis=1)
    wq = w_uq.reshape(Q_LORA, B_HEADS, QK_NOPE + QK_ROPE)
    wq = jnp.pad(wq, ((0, 0), (0, 0), (0, HEAD_PAD - QK_NOPE - QK_ROPE))).reshape(Q_LORA, B_HEADS * HEAD_PAD)
    wkv = w_ukv.reshape(KV_LORA, B_HEADS, QK_NOPE + V_HEAD)
    wk = jnp.pad(wkv[:, :, :QK_NOPE], ((0, 0), (0, 0), (0, HEAD_PAD - QK_NOPE))).reshape(KV_LORA, B_HEADS * HEAD_PAD)
    wvt = wkv[:, :, QK_NOPE:].reshape(KV_LORA, B_HEADS * V_HEAD).T
    return wd.astype(BF16), wq.astype(BF16), wk.astype(BF16), wvt.astype(BF16)


def _mla_proj(x3d, w_dkv, g_q, g_kv, w_uq, w_ukv, tm):
    b, s, d = x3d.shape
    wd, wq, wk, wvt = _mla_weights(w_dkv, w_uq, w_ukv)
    cm, s1, s2 = _rope_tables(s)
    hq = B_HEADS * HEAD_PAD
    hv = B_HEADS * V_ROWS
    tab = pl.BlockSpec((tm, HEAD_PAD), lambda bi, si: (si, 0))
    return pl.pallas_call(
        _mla_proj_kernel,
        grid=(b, s // tm),
        in_specs=[pl.BlockSpec((1, tm, d), lambda bi, si: (bi, si, 0)),
                  _resident(wd.shape), _resident((1, Q_LORA)), _resident((1, KV_LORA)),
                  _resident(wq.shape), _resident(wk.shape), _resident(wvt.shape), tab, tab, tab],
        out_specs=[pl.BlockSpec((1, tm, hq), lambda bi, si: (bi, si, 0)),
                   pl.BlockSpec((1, tm, hq), lambda bi, si: (bi, si, 0)),
                   pl.BlockSpec((1, hv, tm), lambda bi, si: (bi, 0, si))],
        out_shape=[jax.ShapeDtypeStruct((b, s, hq), BF16),
                   jax.ShapeDtypeStruct((b, s, hq), BF16),
                   jax.ShapeDtypeStruct((b, hv, s), BF16)],
        compiler_params=_cparams(("parallel", "parallel")),
        name="mla_proj",
    )(x3d, wd, g_q[None], g_kv[None], wq, wk, wvt, cm, s1, s2)


def _mla_attn_kernel(q_ref, k_ref, vt_ref, o_ref, *, tq, tk):
    seq = k_ref.shape[1]
    n_q = q_ref.shape[1] // tq
    qs = [q_ref[0, j * tq:(j + 1) * tq, :] for j in range(n_q)]

    def body(i, carry):
        k0 = pl.multiple_of(i * tk, tk)
        k = k_ref[0, pl.ds(k0, tk), :]
        vt = vt_ref[0, :, pl.ds(k0, tk)]
        out = []
        scores = [_dot_nt(k, qs[j]) for j in range(n_q)]
        for j in range(n_q):
            m, acc = carry[j]
            s = scores[j]
            m_new = jnp.maximum(m, jnp.max(s, axis=0, keepdims=True))
            alpha = jnp.exp2(m - m_new)
            p = jnp.exp2(s - m_new).astype(BF16)
            out.append((m_new, alpha * acc + _dot(vt, p)))
        return tuple(out)

    init = tuple((jnp.full((1, tq), NEG_INF, F32), jnp.zeros((V_ROWS, tq), F32)) for _ in range(n_q))
    res = lax.fori_loop(0, seq // tk, body, init)
    for j in range(n_q):
        acc = res[j][1]
        o_ref[0, :, j * tq:(j + 1) * tq] = (acc[:V_HEAD] / acc[V_HEAD:V_HEAD + 1]).astype(BF16)


def _mla_attn(q, k, vt, tq_block, tq, tk):
    b, s, _ = q.shape
    return pl.pallas_call(
        functools.partial(_mla_attn_kernel, tq=tq, tk=tk),
        grid=(b, B_HEADS, s // tq_block),
        in_specs=[pl.BlockSpec((1, tq_block, HEAD_PAD), lambda bi, h, qi: (bi, qi, h)),
                  pl.BlockSpec((1, s, HEAD_PAD), lambda bi, h, qi: (bi, 0, h)),
                  pl.BlockSpec((1, V_ROWS, s), lambda bi, h, qi: (bi, h, 0))],
        out_specs=pl.BlockSpec((1, V_HEAD, tq_block), lambda bi, h, qi: (bi, h, qi)),
        out_shape=jax.ShapeDtypeStruct((b, B_HEADS * V_HEAD, s), BF16),
        compiler_params=_cparams(("parallel", "parallel", "arbitrary")),
        name="mla_attn",
    )(q, k, vt)


def _proj_t_ln_kernel(ot_ref, x_ref, w_ref, g_ref, b_ref, out_ref):
    y = ALPHA * x_ref[0] + _dot_tn(ot_ref[0], w_ref[...])
    out_ref[0] = _layer_norm(y, g_ref[...], b_ref[...])


def _proj_t_ln(ot, x3d, w_bf16, ln_g, ln_b, tm):
    b, s, d = x3d.shape
    k = ot.shape[1]
    return pl.pallas_call(
        _proj_t_ln_kernel,
        grid=(b, s // tm),
        in_specs=[pl.BlockSpec((1, k, tm), lambda bi, si: (bi, 0, si)),
                  pl.BlockSpec((1, tm, d), lambda bi, si: (bi, si, 0)),
                  _resident((k, d)), _resident((1, d)), _resident((1, d))],
        out_specs=pl.BlockSpec((1, tm, d), lambda bi, si: (bi, si, 0)),
        out_shape=jax.ShapeDtypeStruct((b, s, d), F32),
        compiler_params=_cparams(("parallel", "parallel")),
        name="proj_t_ln",
    )(ot, x3d, w_bf16, ln_g, ln_b)


def _encoder(x, bias, w_qkv_a, w_o_a, w_dkv_b, g_q_b, g_kv_b, w_uq_b, w_ukv_b, w_o_b,
             ffn_w_in, ffn_conv_w, ffn_conv_b, ffn_w_out, ln_g, ln_b):
    b, s, d = x.shape
    tm = 512
    for i in range(DEPTH):
        j = i // N_MIXERS
        g1, b1 = ln_g[i, 0][None], ln_b[i, 0][None]
        g2, b2 = ln_g[i, 1][None], ln_b[i, 1][None]
        if i % N_MIXERS == 0:
            x2d = x.reshape(b * s, d)
            qkv = _qkv_proj(x2d, w_qkv_a[j].astype(BF16), tm=256)
            outs = [_dilated_attn(qkv, bias, g, b, s) for g in range(A_GROUPS)]
            x = _merge_proj_ln([o for o, _ in outs], [l for _, l in outs], x2d,
                               w_o_a[j].astype(BF16), g1, b1, tm).reshape(b, s, d)
        else:
            q, k, vt = _mla_proj(x, w_dkv_b[j], g_q_b[j], g_kv_b[j], w_uq_b[j], w_ukv_b[j], tm)
            ot = _mla_attn(q, k, vt, tq_block=min(1024, s), tq=256, tk=256)
            x = _proj_t_ln(ot, x, w_o_b[j].astype(BF16), g1, b1, tm)
        x = _ffn_ln(x, ffn_w_in[i], ffn_conv_w[i], ffn_conv_b[i], ffn_w_out[i], g2, b2, tm)
    return x


def kernel(x_prompt, x_sample, rel_bias, w_qkv_a, w_o_a, w_dkv_b, g_q_b, g_kv_b, w_uq_b, w_ukv_b, w_o_b,
           ffn_w_in, ffn_conv_w, ffn_conv_b, ffn_w_out, ln_g, ln_b):
    bias = _bias_tiles(rel_bias)
    args = (bias, w_qkv_a, w_o_a, w_dkv_b, g_q_b, g_kv_b, w_uq_b, w_ukv_b, w_o_b,
            ffn_w_in, ffn_conv_w, ffn_conv_b, ffn_w_out, ln_g, ln_b)
    return (_encoder(x_prompt, *args), _encoder(x_sample, *args))
```

```python
import functools
import math

import jax
import jax.numpy as jnp
import numpy as np
from jax import lax
from jax.experimental import pallas as pl
from jax.experimental.pallas import tpu as pltpu

F32 = jnp.float32
BF16 = jnp.bfloat16

D_MODEL = 1024
DEPTH = 2
N_MIXERS = 2

A_WINDOWS = (128, 512, 2048)
A_DILATIONS = (1, 4, 16)
A_GROUPS = 3
A_HEADS = 8
A_HEAD_DIM = 128
A_WIDTH = A_HEADS * A_HEAD_DIM
A_HALF = 64
NUM_BUCKETS = 32
MAX_DISTANCE = max(A_WINDOWS) // 2

B_HEADS = 16
Q_LORA = 384
KV_LORA = 256
QK_NOPE = 64
QK_ROPE = 32
V_HEAD = 64
ROPE_THETA = 10000.0
HEAD_PAD = 128
V_ROWS = V_HEAD + 16

D_FF = 2816
FF_CHUNK = 256

ALPHA = (2.0 * DEPTH) ** 0.25
LN_EPS = 1e-5
RMS_EPS = 1e-6
NEG_INF = -1e30

LANES = 128
SUBLANES = 8
VMEM_LIMIT = 56 * 1024 * 1024

QK_SCALE_LOG2E = np.float32((QK_NOPE + QK_ROPE) ** -0.5 * math.log2(math.e))


def _cparams(sem):
    return pltpu.CompilerParams(dimension_semantics=sem, vmem_limit_bytes=VMEM_LIMIT)


def _resident(shape):
    nd = len(shape)
    return pl.BlockSpec(shape, lambda *_: (0,) * nd, pipeline_mode=pl.Buffered(1))


def _layer_norm(y, g, b):
    mu = jnp.mean(y, axis=-1, keepdims=True)
    yc = y - mu
    var = jnp.mean(yc * yc, axis=-1, keepdims=True)
    return yc * lax.rsqrt(var + LN_EPS) * g + b


def _dot(a, b):
    return jnp.dot(a, b, preferred_element_type=F32)


def _dot_nt(a, b):
    return lax.dot_general(a, b, (((1,), (1,)), ((), ())), preferred_element_type=F32)


def _dot_tn(a, b):
    return lax.dot_general(a, b, (((0,), (0,)), ((), ())), preferred_element_type=F32)


def _qkv_proj_kernel(x_ref, w_ref, o_ref, *, n_chunk):
    xb = x_ref[...].astype(BF16)
    n_total = w_ref.shape[1]
    for n0 in range(0, n_total, n_chunk):
        o_ref[:, n0:n0 + n_chunk] = _dot(xb, w_ref[:, n0:n0 + n_chunk]).astype(BF16)


def _qkv_proj(x2d, w_bf16, tm):
    t, d = x2d.shape
    n = w_bf16.shape[1]
    return pl.pallas_call(
        functools.partial(_qkv_proj_kernel, n_chunk=1024),
        grid=(t // tm,),
        in_specs=[pl.BlockSpec((tm, d), lambda i: (i, 0)), _resident((d, n))],
        out_specs=pl.BlockSpec((tm, n), lambda i: (i, 0)),
        out_shape=jax.ShapeDtypeStruct((t, n), BF16),
        compiler_params=_cparams(("parallel",)),
        name="qkv_proj",
    )(x2d, w_bf16)


def _t5_bucket(rel):
    nb = NUM_BUCKETS // 2
    max_exact = nb // 2
    ret = jnp.where(rel > 0, nb, 0)
    n = jnp.abs(rel)
    large = max_exact + (jnp.log(jnp.maximum(n, 1).astype(F32) / max_exact)
                         / math.log(MAX_DISTANCE / max_exact) * (nb - max_exact)).astype(jnp.int32)
    large = jnp.minimum(large, nb - 1)
    return ret + jnp.where(n < max_exact, n, large)


def _band_bucket_index():
    qi = jnp.arange(2 * A_HALF)[:, None]
    kj = jnp.arange(4 * A_HALF)[None, :]
    rel = kj - A_HALF - qi
    tiles = [jnp.where(jnp.abs(rel) <= A_HALF, _t5_bucket(rel * dil), -1) for dil in A_DILATIONS]
    return jnp.stack(tiles, axis=0).astype(jnp.int32)


def _bias_tile_kernel(tab_ref, idx_ref, o_ref):
    col = pl.program_id(0) * A_HEADS + pl.program_id(1)
    idx = idx_ref[0]
    acc = jnp.full(idx.shape, NEG_INF, F32)
    for bucket in range(NUM_BUCKETS):
        acc = jnp.where(idx == bucket, tab_ref[bucket, col], acc)
    o_ref[0, 0] = acc


def _bias_tiles(rel_bias):
    idx = _band_bucket_index()
    tq, tk = idx.shape[1:]
    return pl.pallas_call(
        _bias_tile_kernel,
        grid=(A_GROUPS, A_HEADS),
        in_specs=[pl.BlockSpec(memory_space=pltpu.SMEM),
                  pl.BlockSpec((1, tq, tk), lambda g, h: (g, 0, 0))],
        out_specs=pl.BlockSpec((1, 1, tq, tk), lambda g, h: (g, h, 0, 0)),
        out_shape=jax.ShapeDtypeStruct((A_GROUPS, A_HEADS, tq, tk), F32),
        compiler_params=_cparams(("arbitrary", "arbitrary")),
        name="bias_tiles",
    )(rel_bias, idx)


def _dilated_attn_kernel(q_ref, kl_ref, km_ref, kr_ref, vl_ref, vm_ref, vr_ref, bias_ref,
                         o_ref, lse_ref, kbuf, vbuf, *, tq, sub_len):
    li = pl.program_id(2)
    kbuf[0:A_HALF] = kl_ref[0]
    kbuf[A_HALF:A_HALF + tq] = km_ref[0]
    kbuf[A_HALF + tq:] = kr_ref[0]
    vbuf[0:A_HALF] = vl_ref[0]
    vbuf[A_HALF:A_HALF + tq] = vm_ref[0]
    vbuf[A_HALF + tq:] = vr_ref[0]

    scale = A_HEAD_DIM ** -0.5
    qt, kt = 2 * A_HALF, 4 * A_HALF
    lane = lax.broadcasted_iota(jnp.int32, (qt, LANES), 1)
    for t in range(tq // qt):
        pos = li * tq + t * qt - A_HALF + lax.broadcasted_iota(jnp.int32, (1, kt), 1)
        edge = jnp.where((pos >= 0) & (pos < sub_len), 0.0, NEG_INF).astype(F32)
        lse_tile = jnp.zeros((qt, LANES), F32)
        for h in range(A_HEADS):
            cols = slice(h * A_HEAD_DIM, (h + 1) * A_HEAD_DIM)
            q = q_ref[0, t * qt:(t + 1) * qt, cols]
            k = kbuf[t * qt:t * qt + kt, cols]
            v = vbuf[t * qt:t * qt + kt, cols]
            s = _dot_nt(q, k) * scale + bias_ref[0, h] + edge
            m = jnp.max(s, axis=-1, keepdims=True)
            p = jnp.exp(s - m)
            den = jnp.sum(p, axis=-1, keepdims=True)
            o = _dot(p.astype(BF16), v) / den
            o_ref[0, t * qt:(t + 1) * qt, cols] = o
            lse_tile = jnp.where(lane == h, m + jnp.log(den), lse_tile)
        lse_ref[0, t * qt:(t + 1) * qt, :] = lse_tile


def _dilated_attn(qkv, bias, g, batch, seq):
    dil = A_DILATIONS[g]
    sub_len = seq // dil
    tq = min(256, sub_len)
    n_t = sub_len // tq
    hb = tq // A_HALF
    n_hb = sub_len // A_HALF
    n_col = qkv.shape[1] // A_WIDTH
    qkv_v = qkv.reshape(batch, sub_len, dil * qkv.shape[1])

    def col(which):
        return lambda b, r, i: (b, i, r * n_col + g * 3 + which)

    def col_left(which):
        return lambda b, r, i: (b, jnp.maximum(i * hb - 1, 0), r * n_col + g * 3 + which)

    def col_right(which):
        return lambda b, r, i: (b, jnp.minimum((i + 1) * hb, n_hb - 1), r * n_col + g * 3 + which)

    main = lambda which: pl.BlockSpec((1, tq, A_WIDTH), col(which))
    left = lambda which: pl.BlockSpec((1, A_HALF, A_WIDTH), col_left(which))
    right = lambda which: pl.BlockSpec((1, A_HALF, A_WIDTH), col_right(which))

    o, lse = pl.pallas_call(
        functools.partial(_dilated_attn_kernel, tq=tq, sub_len=sub_len),
        grid=(batch, dil, n_t),
        in_specs=[main(0), left(1), main(1), right(1), left(2), main(2), right(2),
                  pl.BlockSpec((1, A_HEADS, 2 * A_HALF, 4 * A_HALF), lambda b, r, i: (g, 0, 0, 0))],
        out_specs=[pl.BlockSpec((1, tq, A_WIDTH), lambda b, r, i: (b, i, r)),
                   pl.BlockSpec((1, tq, LANES), lambda b, r, i: (b, i, r))],
        out_shape=[jax.ShapeDtypeStruct((batch, sub_len, dil * A_WIDTH), F32),
                   jax.ShapeDtypeStruct((batch, sub_len, dil * LANES), F32)],
        scratch_shapes=[pltpu.VMEM((tq + 2 * A_HALF, A_WIDTH), BF16),
                        pltpu.VMEM((tq + 2 * A_HALF, A_WIDTH), BF16)],
        compiler_params=_cparams(("parallel", "parallel", "parallel")),
        name=f"dilated_attn_g{g}",
    )(qkv_v, qkv_v, qkv_v, qkv_v, qkv_v, qkv_v, qkv_v, bias)
    return o.reshape(batch * seq, A_WIDTH), lse.reshape(batch * seq, LANES)


def _merge_proj_ln_kernel(o0_ref, o1_ref, o2_ref, l0_ref, l1_ref, l2_ref, x_ref, w_ref, g_ref, b_ref,
                          out_ref, merged):
    l0, l1, l2 = l0_ref[...], l1_ref[...], l2_ref[...]
    m = jnp.maximum(jnp.maximum(l0, l1), l2)
    e0, e1, e2 = jnp.exp(l0 - m), jnp.exp(l1 - m), jnp.exp(l2 - m)
    den = e0 + e1 + e2
    w0, w1, w2 = e0 / den, e1 / den, e2 / den
    for h in range(A_HEADS):
        cols = slice(h * A_HEAD_DIM, (h + 1) * A_HEAD_DIM)
        mix = (w0[:, h:h + 1] * o0_ref[:, cols] + w1[:, h:h + 1] * o1_ref[:, cols]
               + w2[:, h:h + 1] * o2_ref[:, cols])
        merged[:, cols] = mix.astype(BF16)
    y = ALPHA * x_ref[...] + _dot(merged[...], w_ref[...])
    out_ref[...] = _layer_norm(y, g_ref[...], b_ref[...])


def _merge_proj_ln(os, lses, x2d, w_bf16, ln_g, ln_b, tm):
    t, d = x2d.shape
    row = lambda w: pl.BlockSpec((tm, w), lambda i: (i, 0))
    return pl.pallas_call(
        _merge_proj_ln_kernel,
        grid=(t // tm,),
        in_specs=[row(A_WIDTH)] * 3 + [row(LANES)] * 3 + [row(d), _resident((A_WIDTH, d)),
                                                           _resident((1, d)), _resident((1, d))],
        out_specs=row(d),
        out_shape=jax.ShapeDtypeStruct((t, d), F32),
        scratch_shapes=[pltpu.VMEM((tm, A_WIDTH), BF16)],
        compiler_params=_cparams(("parallel",)),
        name="merge_proj_ln",
    )(*os, *lses, x2d, w_bf16, ln_g, ln_b)


def _ffn_ln_kernel(x_ref, xp_ref, xn_ref, win_ref, cw_ref, cb_ref, wout_ref, g_ref, b_ref,
                   out_ref, act, *, tm):
    si = pl.program_id(1)
    n_s = pl.num_programs(1)
    x = x_ref[0]
    xb = x.astype(BF16)
    prev_ok = (si > 0).astype(F32)
    next_ok = (si < n_s - 1).astype(F32)
    xe = jnp.concatenate([xp_ref[0] * prev_ok, xn_ref[0] * next_ok], axis=0).astype(BF16)
    row = lax.broadcasted_iota(jnp.int32, (tm, 2 * FF_CHUNK), 0)
    for c in range(D_FF // FF_CHUNK):
        cols = slice(c * 2 * FF_CHUNK, (c + 1) * 2 * FF_CHUNK)
        w = win_ref[:, cols]
        h = _dot(xb, w)
        he = _dot(xe, w)
        up = jnp.where(row == 0, he[SUBLANES - 1:SUBLANES], pltpu.roll(h, 1, 0))
        dn = jnp.where(row == tm - 1, he[SUBLANES:SUBLANES + 1], pltpu.roll(h, tm - 1, 0))
        cw = cw_ref[:, cols]
        hc = up * cw[0:1] + h * cw[1:2] + dn * cw[2:3] + cb_ref[:, cols]
        a = hc[:, :FF_CHUNK]
        gate = hc[:, FF_CHUNK:]
        gelu = 0.5 * gate * (1.0 + lax.erf(gate * np.float32(math.sqrt(0.5))))
        act[:, c * FF_CHUNK:(c + 1) * FF_CHUNK] = (a * gelu).astype(BF16)
    y = ALPHA * x + _dot(act[...], wout_ref[...])
    out_ref[0] = _layer_norm(y, g_ref[...], b_ref[...])


def _chunk_interleave(w):
    lead = w.shape[:-1]
    w = w.reshape(lead + (2, D_FF // FF_CHUNK, FF_CHUNK))
    w = jnp.swapaxes(w, -3, -2)
    return w.reshape(lead + (2 * D_FF,))


def _ffn_ln(x3d, w_in, conv_w, conv_b, w_out, ln_g, ln_b, tm):
    b, s, d = x3d.shape
    nb8 = tm // SUBLANES
    n8 = s // SUBLANES
    win = _chunk_interleave(w_in).astype(BF16)
    cw = _chunk_interleave(conv_w)
    cb = _chunk_interleave(conv_b)[None]
    return pl.pallas_call(
        functools.partial(_ffn_ln_kernel, tm=tm),
        grid=(b, s // tm),
        in_specs=[pl.BlockSpec((1, tm, d), lambda bi, si: (bi, si, 0)),
                  pl.BlockSpec((1, SUBLANES, d), lambda bi, si: (bi, jnp.maximum(si * nb8 - 1, 0), 0)),
                  pl.BlockSpec((1, SUBLANES, d), lambda bi, si: (bi, jnp.minimum((si + 1) * nb8, n8 - 1), 0)),
                  _resident((d, 2 * D_FF)), _resident((3, 2 * D_FF)), _resident((1, 2 * D_FF)),
                  _resident((D_FF, d)), _resident((1, d)), _resident((1, d))],
        out_specs=pl.BlockSpec((1, tm, d), lambda bi, si: (bi, si, 0)),
        out_shape=jax.ShapeDtypeStruct((b, s, d), F32),
        scratch_shapes=[pltpu.VMEM((tm, D_FF), BF16)],
        compiler_params=_cparams(("parallel", "parallel")),
        name="ffn_ln",
    )(x3d, x3d, x3d, win, cw, cb, w_out.astype(BF16), ln_g, ln_b)


def _rope_tables(seq):
    inv = 1.0 / (ROPE_THETA ** (jnp.arange(0, QK_ROPE, 2, dtype=F32) / QK_ROPE))
    ang = jnp.arange(seq, dtype=F32)[:, None] * inv[None, :]
    cos, sin = jnp.cos(ang), jnp.sin(ang)
    half = QK_ROPE // 2
    ones = jnp.ones((seq, QK_NOPE), F32)
    z_nope = jnp.zeros((seq, QK_NOPE), F32)
    z_half = jnp.zeros((seq, half), F32)
    z_pad = jnp.zeros((seq, HEAD_PAD - QK_NOPE - QK_ROPE), F32)
    cmul = jnp.concatenate([ones, cos, cos, z_pad], axis=1)
    s_first = jnp.concatenate([z_nope, -sin, z_half, z_pad], axis=1)
    s_second = jnp.concatenate([z_nope, z_half, sin, z_pad], axis=1)
    return cmul, s_first, s_second


def _rope(t, cmul, s_first, s_second):
    half = QK_ROPE // 2
    return (t * cmul + pltpu.roll(t, HEAD_PAD - half, 1) * s_first
            + pltpu.roll(t, half, 1) * s_second)


def _rms_norm(x, g):
    ms = jnp.mean(x * x, axis=-1, keepdims=True)
    return x * lax.rsqrt(ms + RMS_EPS) * g


def _mla_proj_kernel(x_ref, wd_ref, gq_ref, gkv_ref, wuq_ref, wuk_ref, wuvt_ref,
                     cm_ref, s1_ref, s2_ref, q_ref, k_ref, vt_ref):
    xb = x_ref[0].astype(BF16)
    c = _dot(xb, wd_ref[...])
    cqn = _rms_norm(c[:, :Q_LORA], gq_ref[...]).astype(BF16)
    ckvn = _rms_norm(c[:, Q_LORA:Q_LORA + KV_LORA], gkv_ref[...]).astype(BF16)
    cm, s1, s2 = cm_ref[...], s1_ref[...], s2_ref[...]
    kr = _rope(c[:, Q_LORA + KV_LORA:], cm, s1, s2)
    q = _dot(cqn, wuq_ref[...])
    k = _dot(ckvn, wuk_ref[...])
    for h in range(B_HEADS):
        cols = slice(h * HEAD_PAD, (h + 1) * HEAD_PAD)
        q_ref[0, :, cols] = (_rope(q[:, cols], cm, s1, s2) * QK_SCALE_LOG2E).astype(BF16)
        k_ref[0, :, cols] = (k[:, cols] + kr).astype(BF16)
    vt = _dot_nt(wuvt_ref[...], ckvn).astype(BF16)
    tm = vt.shape[1]
    row = lax.broadcasted_iota(jnp.int32, (V_ROWS - V_HEAD, tm), 0)
    ones_row = jnp.where(row == 0, 1.0, 0.0).astype(BF16)
    for h in range(B_HEADS):
        vt_ref[0, h * V_ROWS:h * V_ROWS + V_HEAD, :] = vt[h * V_HEAD:(h + 1) * V_HEAD]
        vt_ref[0, h * V_ROWS + V_HEAD:(h + 1) * V_ROWS, :] = ones_row


def _mla_weights(w_dkv, w_uq, w_ukv):
    d = w_dkv.shape[0]
    pad_l = jnp.zeros((d, QK_NOPE), F32)
    pad_r = jnp.zeros((d, HEAD_PAD - QK_NOPE - QK_ROPE), F32)
    wd = jnp.concatenate([w_dkv[:, :Q_LORA + KV_LORA], pad_l, w_dkv[:, Q_LORA + KV_LORA:], pad_r], axis=1)
    wq = w_uq.reshape(Q_LORA, B_HEADS, QK_NOPE + QK_ROPE)
    wq = jnp.pad(wq, ((0, 0), (0, 0), (0, HEAD_PAD - QK_NOPE - QK_ROPE))).reshape(Q_LORA, B_HEADS * HEAD_PAD)
    wkv = w_ukv.reshape(KV_LORA, B_HEADS, QK_NOPE + V_HEAD)
    wk = jnp.pad(wkv[:, :, :QK_NOPE], ((0, 0), (0, 0), (0, HEAD_PAD - QK_NOPE))).reshape(KV_LORA, B_HEADS * HEAD_PAD)
    wvt = wkv[:, :, QK_NOPE:].reshape(KV_LORA, B_HEADS * V_HEAD).T
    return wd.astype(BF16), wq.astype(BF16), wk.astype(BF16), wvt.astype(BF16)


def _mla_proj(x3d, w_dkv, g_q, g_kv, w_uq, w_ukv, tm):
    b, s, d = x3d.shape
    wd, wq, wk, wvt = _mla_weights(w_dkv, w_uq, w_ukv)
    cm, s1, s2 = _rope_tables(s)
    hq = B_HEADS * HEAD_PAD
    hv = B_HEADS * V_ROWS
    tab = pl.BlockSpec((tm, HEAD_PAD), lambda bi, si: (si, 0))
    return pl.pallas_call(
        _mla_proj_kernel,
        grid=(b, s // tm),
        in_specs=[pl.BlockSpec((1, tm, d), lambda bi, si: (bi, si, 0)),
                  _resident(wd.shape), _resident((1, Q_LORA)), _resident((1, KV_LORA)),
                  _resident(wq.shape), _resident(wk.shape), _resident(wvt.shape), tab, tab, tab],
        out_specs=[pl.BlockSpec((1, tm, hq), lambda bi, si: (bi, si, 0)),
                   pl.BlockSpec((1, tm, hq), lambda bi, si: (bi, si, 0)),
                   pl.BlockSpec((1, hv, tm), lambda bi, si: (bi, 0, si))],
        out_shape=[jax.ShapeDtypeStruct((b, s, hq), BF16),
                   jax.ShapeDtypeStruct((b, s, hq), BF16),
                   jax.ShapeDtypeStruct((b, hv, s), BF16)],
        compiler_params=_cparams(("parallel", "parallel")),
        name="mla_proj",
    )(x3d, wd, g_q[None], g_kv[None], wq, wk, wvt, cm, s1, s2)


def _mla_attn_kernel(q_ref, k_ref, vt_ref, o_ref, s_scr, p_scr, m_scr, a_scr, acc_scr, *, tq, tk):
    seq = k_ref.shape[1]
    nk = seq // tk
    n_q = q_ref.shape[1] // tq

    def run_step(t, par, do_qk=True, do_sm=True, do_pv=True):
        if do_pv:
            v0 = pl.multiple_of((t - 2) * tk, tk)
            vt = vt_ref[0, :, pl.ds(v0, tk)]
            pvs = [_dot(vt, p_scr[par, j]) for j in range(n_q)]
        if do_qk:
            k0 = pl.multiple_of(t * tk, tk)
            k = k_ref[0, pl.ds(k0, tk), :]
            scores = [_dot_nt(k, q_ref[0, j * tq:(j + 1) * tq, :]) for j in range(n_q)]
        for j in range(n_q):
            if do_pv:
                acc_scr[j] = a_scr[j] * acc_scr[j] + pvs[j]
            if do_sm:
                s = s_scr[1 - par, j]
                m = m_scr[j]
                m_new = jnp.maximum(m, jnp.max(s, axis=0, keepdims=True))
                a_scr[j] = jnp.exp2(m - m_new)
                p_scr[1 - par, j] = jnp.exp2(s - m_new).astype(BF16)
                m_scr[j] = m_new
        if do_qk:
            for j in range(n_q):
                s_scr[par, j] = scores[j]

    m_scr[...] = jnp.full(m_scr.shape, NEG_INF, F32)
    a_scr[...] = jnp.zeros(a_scr.shape, F32)
    acc_scr[...] = jnp.zeros(acc_scr.shape, F32)
    run_step(0, 0, do_sm=False, do_pv=False)
    run_step(1, 1, do_pv=False)

    def pair(u, carry):
        t = 2 * u + 2
        run_step(t, 0)
        run_step(t + 1, 1)
        return carry

    lax.fori_loop(0, (nk - 2) // 2, pair, 0)
    run_step(nk, nk % 2, do_qk=False)
    run_step(nk + 1, (nk + 1) % 2, do_qk=False, do_sm=False)
    for j in range(n_q):
        acc = acc_scr[j]
        o_ref[0, :, j * tq:(j + 1) * tq] = (acc[:V_HEAD] / acc[V_HEAD:V_HEAD + 1]).astype(BF16)


def _mla_attn(q, k, vt, tq_block, tq, tk):
    b, s, _ = q.shape
    n_q = tq_block // tq
    assert (s // tk) % 2 == 0 and s // tk >= 2
    return pl.pallas_call(
        functools.partial(_mla_attn_kernel, tq=tq, tk=tk),
        grid=(b, B_HEADS, s // tq_block),
        in_specs=[pl.BlockSpec((1, tq_block, HEAD_PAD), lambda bi, h, qi: (bi, qi, h)),
                  pl.BlockSpec((1, s, HEAD_PAD), lambda bi, h, qi: (bi, 0, h)),
                  pl.BlockSpec((1, V_ROWS, s), lambda bi, h, qi: (bi, h, 0))],
        out_specs=pl.BlockSpec((1, V_HEAD, tq_block), lambda bi, h, qi: (bi, h, qi)),
        out_shape=jax.ShapeDtypeStruct((b, B_HEADS * V_HEAD, s), BF16),
        scratch_shapes=[pltpu.VMEM((2, n_q, tk, tq), F32), pltpu.VMEM((2, n_q, tk, tq), BF16),
                        pltpu.VMEM((n_q, 1, tq), F32), pltpu.VMEM((n_q, 1, tq), F32),
                        pltpu.VMEM((n_q, V_ROWS, tq), F32)],
        compiler_params=_cparams(("parallel", "parallel", "arbitrary")),
        name="mla_attn",
    )(q, k, vt)


def _proj_t_ln_kernel(ot_ref, x_ref, w_ref, g_ref, b_ref, out_ref):
    y = ALPHA * x_ref[0] + _dot_tn(ot_ref[0], w_ref[...])
    out_ref[0] = _layer_norm(y, g_ref[...], b_ref[...])


def _proj_t_ln(ot, x3d, w_bf16, ln_g, ln_b, tm):
    b, s, d = x3d.shape
    k = ot.shape[1]
    return pl.pallas_call(
        _proj_t_ln_kernel,
        grid=(b, s // tm),
        in_specs=[pl.BlockSpec((1, k, tm), lambda bi, si: (bi, 0, si)),
                  pl.BlockSpec((1, tm, d), lambda bi, si: (bi, si, 0)),
                  _resident((k, d)), _resident((1, d)), _resident((1, d))],
        out_specs=pl.BlockSpec((1, tm, d), lambda bi, si: (bi, si, 0)),
        out_shape=jax.ShapeDtypeStruct((b, s, d), F32),
        compiler_params=_cparams(("parallel", "parallel")),
        name="proj_t_ln",
    )(ot, x3d, w_bf16, ln_g, ln_b)


def _encoder(x, bias, w_qkv_a, w_o_a, w_dkv_b, g_q_b, g_kv_b, w_uq_b, w_ukv_b, w_o_b,
             ffn_w_in, ffn_conv_w, ffn_conv_b, ffn_w_out, ln_g, ln_b):
    b, s, d = x.shape
    tm = 512
    for i in range(DEPTH):
        j = i // N_MIXERS
        g1, b1 = ln_g[i, 0][None], ln_b[i, 0][None]
        g2, b2 = ln_g[i, 1][None], ln_b[i, 1][None]
        if i % N_MIXERS == 0:
            x2d = x.reshape(b * s, d)
            qkv = _qkv_proj(x2d, w_qkv_a[j].astype(BF16), tm=256)
            outs = [_dilated_attn(qkv, bias, g, b, s) for g in range(A_GROUPS)]
            x = _merge_proj_ln([o for o, _ in outs], [l for _, l in outs], x2d,
                               w_o_a[j].astype(BF16), g1, b1, tm).reshape(b, s, d)
        else:
            q, k, vt = _mla_proj(x, w_dkv_b[j], g_q_b[j], g_kv_b[j], w_uq_b[j], w_ukv_b[j], tm)
            ot = _mla_attn(q, k, vt, tq_block=min(1024, s), tq=256, tk=256)
            x = _proj_t_ln(ot, x, w_o_b[j].astype(BF16), g1, b1, tm)
        x = _ffn_ln(x, ffn_w_in[i], ffn_conv_w[i], ffn_conv_b[i], ffn_w_out[i], g2, b2, tm)
    return x


def kernel(x_prompt, x_sample, rel_bias, w_qkv_a, w_o_a, w_dkv_b, g_q_b, g_kv_b, w_uq_b, w_ukv_b, w_o_b,
           ffn_w_in, ffn_conv_w, ffn_conv_b, ffn_w_out, ln_g, ln_b):
    bias = _bias_tiles(rel_bias)
    args = (bias, w_qkv_a, w_o_a, w_dkv_b, g_q_b, g_kv_b, w_uq_b, w_ukv_b, w_o_b,
            ffn_w_in, ffn_conv_w, ffn_conv_b, ffn_w_out, ln_g, ln_b)
    return (_encoder(x_prompt, *args), _encoder(x_sample, *args))
```

```python
import functools
import math

import jax
import jax.numpy as jnp
import numpy as np
from jax import lax
from jax.experimental import pallas as pl
from jax.experimental.pallas import tpu as pltpu

F32 = jnp.float32
BF16 = jnp.bfloat16

D_MODEL = 1024
DEPTH = 2
N_MIXERS = 2

A_WINDOWS = (128, 512, 2048)
A_DILATIONS = (1, 4, 16)
A_GROUPS = 3
A_HEADS = 8
A_HEAD_DIM = 128
A_WIDTH = A_HEADS * A_HEAD_DIM
A_HALF = 64
NUM_BUCKETS = 32
MAX_DISTANCE = max(A_WINDOWS) // 2

B_HEADS = 16
Q_LORA = 384
KV_LORA = 256
QK_NOPE = 64
QK_ROPE = 32
V_HEAD = 64
ROPE_THETA = 10000.0
HEAD_PAD = 128
V_ROWS = V_HEAD + 16

D_FF = 2816
FF_CHUNK = 256

ALPHA = (2.0 * DEPTH) ** 0.25
LN_EPS = 1e-5
RMS_EPS = 1e-6
NEG_INF = -1e30

LANES = 128
SUBLANES = 8
VMEM_LIMIT = 56 * 1024 * 1024

QK_SCALE_LOG2E = np.float32((QK_NOPE + QK_ROPE) ** -0.5 * math.log2(math.e))


def _cparams(sem):
    return pltpu.CompilerParams(dimension_semantics=sem, vmem_limit_bytes=VMEM_LIMIT)


def _resident(shape):
    nd = len(shape)
    return pl.BlockSpec(shape, lambda *_: (0,) * nd, pipeline_mode=pl.Buffered(1))


def _layer_norm(y, g, b):
    mu = jnp.mean(y, axis=-1, keepdims=True)
    yc = y - mu
    var = jnp.mean(yc * yc, axis=-1, keepdims=True)
    return yc * lax.rsqrt(var + LN_EPS) * g + b


def _dot(a, b):
    return jnp.dot(a, b, preferred_element_type=F32)


def _dot_nt(a, b):
    return lax.dot_general(a, b, (((1,), (1,)), ((), ())), preferred_element_type=F32)


def _dot_tn(a, b):
    return lax.dot_general(a, b, (((0,), (0,)), ((), ())), preferred_element_type=F32)


def _qkv_proj_kernel(x_ref, w_ref, *refs, tm):
    outs, res_scr = refs[:A_GROUPS], refs[A_GROUPS]
    xb = x_ref[0].astype(BF16)
    slot = 0
    for g, dil in enumerate(A_DILATIONS):
        for c in range(3):
            n0 = (g * 3 + c) * A_WIDTH
            cols = slice(c * A_WIDTH, (c + 1) * A_WIDTH)
            res = _dot(xb, w_ref[:, n0:n0 + A_WIDTH])
            if dil == 1:
                outs[g][0, 0, :, cols] = res.astype(BF16)
                continue
            for h in range(A_HEADS):
                res_scr[slot, h] = res[:, h * LANES:(h + 1) * LANES]
            for r in range(dil):
                for h in range(A_HEADS):
                    lanes = slice(c * A_WIDTH + h * LANES, c * A_WIDTH + (h + 1) * LANES)
                    outs[g][0, r, :, lanes] = res_scr[slot, h, pl.ds(r, tm // dil, stride=dil), :].astype(BF16)
            slot = 1 - slot


def _qkv_proj(x3d, w_bf16, tm):
    b, s, d = x3d.shape
    width = 3 * A_WIDTH
    return pl.pallas_call(
        functools.partial(_qkv_proj_kernel, tm=tm),
        grid=(b, s // tm),
        in_specs=[pl.BlockSpec((1, tm, d), lambda bi, i: (bi, i, 0)), _resident(w_bf16.shape)],
        out_specs=[pl.BlockSpec((1, dil, tm // dil, width), lambda bi, i: (bi, 0, i, 0))
                   for dil in A_DILATIONS],
        out_shape=[jax.ShapeDtypeStruct((b, dil, s // dil, width), BF16) for dil in A_DILATIONS],
        scratch_shapes=[pltpu.VMEM((2, A_HEADS, tm, LANES), F32)],
        compiler_params=_cparams(("parallel", "parallel")),
        name="qkv_proj",
    )(x3d, w_bf16)


def _t5_bucket(rel):
    nb = NUM_BUCKETS // 2
    max_exact = nb // 2
    ret = jnp.where(rel > 0, nb, 0)
    n = jnp.abs(rel)
    large = max_exact + (jnp.log(jnp.maximum(n, 1).astype(F32) / max_exact)
                         / math.log(MAX_DISTANCE / max_exact) * (nb - max_exact)).astype(jnp.int32)
    large = jnp.minimum(large, nb - 1)
    return ret + jnp.where(n < max_exact, n, large)


def _band_bucket_index():
    qi = jnp.arange(2 * A_HALF)[:, None]
    kj = jnp.arange(4 * A_HALF)[None, :]
    rel = kj - A_HALF - qi
    tiles = [jnp.where(jnp.abs(rel) <= A_HALF, _t5_bucket(rel * dil), -1) for dil in A_DILATIONS]
    return jnp.stack(tiles, axis=0).astype(jnp.int32)


def _bias_tile_kernel(tab_ref, idx_ref, o_ref):
    col = pl.program_id(0) * A_HEADS + pl.program_id(1)
    idx = idx_ref[0]
    acc = jnp.full(idx.shape, NEG_INF, F32)
    for bucket in range(NUM_BUCKETS):
        acc = jnp.where(idx == bucket, tab_ref[bucket, col], acc)
    o_ref[0, 0] = acc


def _bias_tiles(rel_bias):
    idx = _band_bucket_index()
    tq, tk = idx.shape[1:]
    return pl.pallas_call(
        _bias_tile_kernel,
        grid=(A_GROUPS, A_HEADS),
        in_specs=[pl.BlockSpec(memory_space=pltpu.SMEM),
                  pl.BlockSpec((1, tq, tk), lambda g, h: (g, 0, 0))],
        out_specs=pl.BlockSpec((1, 1, tq, tk), lambda g, h: (g, h, 0, 0)),
        out_shape=jax.ShapeDtypeStruct((A_GROUPS, A_HEADS, tq, tk), F32),
        compiler_params=_cparams(("arbitrary", "arbitrary")),
        name="bias_tiles",
    )(rel_bias, idx)


def _dilated_attn_kernel(q_ref, kl_ref, km_ref, kr_ref, vl_ref, vm_ref, vr_ref, bias_ref,
                         o_ref, lse_ref, kbuf, vbuf, *, tq, sub_len):
    li = pl.program_id(2)
    kbuf[0:A_HALF] = kl_ref[0, 0]
    kbuf[A_HALF:A_HALF + tq] = km_ref[0, 0]
    kbuf[A_HALF + tq:] = kr_ref[0, 0]
    vbuf[0:A_HALF] = vl_ref[0, 0]
    vbuf[A_HALF:A_HALF + tq] = vm_ref[0, 0]
    vbuf[A_HALF + tq:] = vr_ref[0, 0]

    scale = A_HEAD_DIM ** -0.5
    qt, kt = 2 * A_HALF, 4 * A_HALF
    lane = lax.broadcasted_iota(jnp.int32, (qt, LANES), 1)
    for t in range(tq // qt):
        pos = li * tq + t * qt - A_HALF + lax.broadcasted_iota(jnp.int32, (1, kt), 1)
        edge = jnp.where((pos >= 0) & (pos < sub_len), 0.0, NEG_INF).astype(F32)
        lse_tile = jnp.zeros((qt, LANES), F32)
        for h in range(A_HEADS):
            cols = slice(h * A_HEAD_DIM, (h + 1) * A_HEAD_DIM)
            q = q_ref[0, 0, t * qt:(t + 1) * qt, cols]
            k = kbuf[t * qt:t * qt + kt, cols]
            v = vbuf[t * qt:t * qt + kt, cols]
            s = _dot_nt(q, k) * scale + bias_ref[0, h] + edge
            m = jnp.max(s, axis=-1, keepdims=True)
            p = jnp.exp(s - m)
            den = jnp.sum(p, axis=-1, keepdims=True)
            o = _dot(p.astype(BF16), v) / den
            o_ref[0, 0, t * qt:(t + 1) * qt, cols] = o.astype(BF16)
            lse_tile = jnp.where(lane == h, m + jnp.log(den), lse_tile)
        lse_ref[0, 0, t * qt:(t + 1) * qt, :] = lse_tile


def _dilated_attn(qkv_g, bias, g):
    batch, dil, sub_len, _ = qkv_g.shape
    tq = min(256, sub_len)
    n_t = sub_len // tq
    hb = tq // A_HALF
    n_hb = sub_len // A_HALF

    main = lambda which: pl.BlockSpec((1, 1, tq, A_WIDTH), lambda b, r, i: (b, r, i, which))
    left = lambda which: pl.BlockSpec(
        (1, 1, A_HALF, A_WIDTH), lambda b, r, i: (b, r, jnp.maximum(i * hb - 1, 0), which))
    right = lambda which: pl.BlockSpec(
        (1, 1, A_HALF, A_WIDTH), lambda b, r, i: (b, r, jnp.minimum((i + 1) * hb, n_hb - 1), which))

    return pl.pallas_call(
        functools.partial(_dilated_attn_kernel, tq=tq, sub_len=sub_len),
        grid=(batch, dil, n_t),
        in_specs=[main(0), left(1), main(1), right(1), left(2), main(2), right(2),
                  pl.BlockSpec((1, A_HEADS, 2 * A_HALF, 4 * A_HALF), lambda b, r, i: (g, 0, 0, 0))],
        out_specs=[pl.BlockSpec((1, 1, tq, A_WIDTH), lambda b, r, i: (b, r, i, 0)),
                   pl.BlockSpec((1, 1, tq, LANES), lambda b, r, i: (b, r, i, 0))],
        out_shape=[jax.ShapeDtypeStruct((batch, dil, sub_len, A_WIDTH), BF16),
                   jax.ShapeDtypeStruct((batch, dil, sub_len, LANES), F32)],
        scratch_shapes=[pltpu.VMEM((tq + 2 * A_HALF, A_WIDTH), BF16),
                        pltpu.VMEM((tq + 2 * A_HALF, A_WIDTH), BF16)],
        compiler_params=_cparams(("parallel", "parallel", "parallel")),
        name=f"dilated_attn_g{g}",
    )(qkv_g, qkv_g, qkv_g, qkv_g, qkv_g, qkv_g, qkv_g, bias)


def _merge_proj_ln_kernel(o0_ref, o1_ref, o2_ref, l0_ref, l1_ref, l2_ref, x_ref, w_ref, g_ref, b_ref,
                          out_ref, merged, o_scr, l_scr, *, tm):
    for g, (o_ref, l_ref, dil) in enumerate(zip((o0_ref, o1_ref, o2_ref), (l0_ref, l1_ref, l2_ref),
                                                A_DILATIONS)):
        for r in range(dil):
            rows = pl.ds(r, tm // dil, stride=dil) if dil > 1 else slice(None)
            l_scr[g, rows, :] = l_ref[0, r]
            for h in range(A_HEADS):
                o_scr[g, h, rows, :] = o_ref[0, r, :, h * A_HEAD_DIM:(h + 1) * A_HEAD_DIM].astype(F32)
    l0, l1, l2 = l_scr[0], l_scr[1], l_scr[2]
    m = jnp.maximum(jnp.maximum(l0, l1), l2)
    e0, e1, e2 = jnp.exp(l0 - m), jnp.exp(l1 - m), jnp.exp(l2 - m)
    den = e0 + e1 + e2
    w0, w1, w2 = e0 / den, e1 / den, e2 / den
    for h in range(A_HEADS):
        cols = slice(h * A_HEAD_DIM, (h + 1) * A_HEAD_DIM)
        mix = (w0[:, h:h + 1] * o_scr[0, h] + w1[:, h:h + 1] * o_scr[1, h]
               + w2[:, h:h + 1] * o_scr[2, h])
        merged[:, cols] = mix.astype(BF16)
    y = ALPHA * x_ref[0] + _dot(merged[...], w_ref[...])
    out_ref[0] = _layer_norm(y, g_ref[...], b_ref[...])


def _merge_proj_ln(os, lses, x3d, w_bf16, ln_g, ln_b, tm):
    b, s, d = x3d.shape
    plane = lambda w: [pl.BlockSpec((1, dil, tm // dil, w), lambda bi, i: (bi, 0, i, 0))
                       for dil in A_DILATIONS]
    row = pl.BlockSpec((1, tm, d), lambda bi, i: (bi, i, 0))
    return pl.pallas_call(
        functools.partial(_merge_proj_ln_kernel, tm=tm),
        grid=(b, s // tm),
        in_specs=plane(A_WIDTH) + plane(LANES) + [row, _resident((A_WIDTH, d)),
                                                  _resident((1, d)), _resident((1, d))],
        out_specs=row,
        out_shape=jax.ShapeDtypeStruct((b, s, d), F32),
        scratch_shapes=[pltpu.VMEM((tm, A_WIDTH), BF16),
                        pltpu.VMEM((A_GROUPS, A_HEADS, tm, A_HEAD_DIM), F32),
                        pltpu.VMEM((A_GROUPS, tm, LANES), F32)],
        compiler_params=_cparams(("parallel", "parallel")),
        name="merge_proj_ln",
    )(*os, *lses, x3d, w_bf16, ln_g, ln_b)


def _ffn_ln_kernel(x_ref, xp_ref, xn_ref, win_ref, cw_ref, cb_ref, wout_ref, g_ref, b_ref,
                   out_ref, act, *, tm):
    si = pl.program_id(1)
    n_s = pl.num_programs(1)
    x = x_ref[0]
    xb = x.astype(BF16)
    prev_ok = (si > 0).astype(F32)
    next_ok = (si < n_s - 1).astype(F32)
    xe = jnp.concatenate([xp_ref[0] * prev_ok, xn_ref[0] * next_ok], axis=0).astype(BF16)
    row = lax.broadcasted_iota(jnp.int32, (tm, 2 * FF_CHUNK), 0)
    for c in range(D_FF // FF_CHUNK):
        cols = slice(c * 2 * FF_CHUNK, (c + 1) * 2 * FF_CHUNK)
        w = win_ref[:, cols]
        h = _dot(xb, w)
        he = _dot(xe, w)
        up = jnp.where(row == 0, he[SUBLANES - 1:SUBLANES], pltpu.roll(h, 1, 0))
        dn = jnp.where(row == tm - 1, he[SUBLANES:SUBLANES + 1], pltpu.roll(h, tm - 1, 0))
        cw = cw_ref[:, cols]
        hc = up * cw[0:1] + h * cw[1:2] + dn * cw[2:3] + cb_ref[:, cols]
        a = hc[:, :FF_CHUNK]
        gate = hc[:, FF_CHUNK:]
        gelu = 0.5 * gate * (1.0 + lax.erf(gate * np.float32(math.sqrt(0.5))))
        act[:, c * FF_CHUNK:(c + 1) * FF_CHUNK] = (a * gelu).astype(BF16)
    y = ALPHA * x + _dot(act[...], wout_ref[...])
    out_ref[0] = _layer_norm(y, g_ref[...], b_ref[...])


def _chunk_interleave(w):
    lead = w.shape[:-1]
    w = w.reshape(lead + (2, D_FF // FF_CHUNK, FF_CHUNK))
    w = jnp.swapaxes(w, -3, -2)
    return w.reshape(lead + (2 * D_FF,))


def _ffn_ln(x3d, w_in, conv_w, conv_b, w_out, ln_g, ln_b, tm):
    b, s, d = x3d.shape
    nb8 = tm // SUBLANES
    n8 = s // SUBLANES
    win = _chunk_interleave(w_in).astype(BF16)
    cw = _chunk_interleave(conv_w)
    cb = _chunk_interleave(conv_b)[None]
    return pl.pallas_call(
        functools.partial(_ffn_ln_kernel, tm=tm),
        grid=(b, s // tm),
        in_specs=[pl.BlockSpec((1, tm, d), lambda bi, si: (bi, si, 0)),
                  pl.BlockSpec((1, SUBLANES, d), lambda bi, si: (bi, jnp.maximum(si * nb8 - 1, 0), 0)),
                  pl.BlockSpec((1, SUBLANES, d), lambda bi, si: (bi, jnp.minimum((si + 1) * nb8, n8 - 1), 0)),
                  _resident((d, 2 * D_FF)), _resident((3, 2 * D_FF)), _resident((1, 2 * D_FF)),
                  _resident((D_FF, d)), _resident((1, d)), _resident((1, d))],
        out_specs=pl.BlockSpec((1, tm, d), lambda bi, si: (bi, si, 0)),
        out_shape=jax.ShapeDtypeStruct((b, s, d), F32),
        scratch_shapes=[pltpu.VMEM((tm, D_FF), BF16)],
        compiler_params=_cparams(("parallel", "parallel")),
        name="ffn_ln",
    )(x3d, x3d, x3d, win, cw, cb, w_out.astype(BF16), ln_g, ln_b)


def _rope_tables(seq):
    inv = 1.0 / (ROPE_THETA ** (jnp.arange(0, QK_ROPE, 2, dtype=F32) / QK_ROPE))
    ang = jnp.arange(seq, dtype=F32)[:, None] * inv[None, :]
    cos, sin = jnp.cos(ang), jnp.sin(ang)
    half = QK_ROPE // 2
    ones = jnp.ones((seq, QK_NOPE), F32)
    z_nope = jnp.zeros((seq, QK_NOPE), F32)
    z_half = jnp.zeros((seq, half), F32)
    z_pad = jnp.zeros((seq, HEAD_PAD - QK_NOPE - QK_ROPE), F32)
    cmul = jnp.concatenate([ones, cos, cos, z_pad], axis=1)
    s_first = jnp.concatenate([z_nope, -sin, z_half, z_pad], axis=1)
    s_second = jnp.concatenate([z_nope, z_half, sin, z_pad], axis=1)
    return cmul, s_first, s_second


def _rope(t, cmul, s_first, s_second):
    half = QK_ROPE // 2
    return (t * cmul + pltpu.roll(t, HEAD_PAD - half, 1) * s_first
            + pltpu.roll(t, half, 1) * s_second)


def _rms_norm(x, g):
    ms = jnp.mean(x * x, axis=-1, keepdims=True)
    return x * lax.rsqrt(ms + RMS_EPS) * g


def _mla_proj_kernel(x_ref, wd_ref, gq_ref, gkv_ref, wuq_ref, wuk_ref, wuvt_ref,
                     cm_ref, s1_ref, s2_ref, q_ref, k_ref, vt_ref):
    xb = x_ref[0].astype(BF16)
    c = _dot(xb, wd_ref[...])
    cqn = _rms_norm(c[:, :Q_LORA], gq_ref[...]).astype(BF16)
    ckvn = _rms_norm(c[:, Q_LORA:Q_LORA + KV_LORA], gkv_ref[...]).astype(BF16)
    cm, s1, s2 = cm_ref[...], s1_ref[...], s2_ref[...]
    kr = _rope(c[:, Q_LORA + KV_LORA:], cm, s1, s2)
    q = _dot(cqn, wuq_ref[...])
    k = _dot(ckvn, wuk_ref[...])
    for h in range(B_HEADS):
        cols = slice(h * HEAD_PAD, (h + 1) * HEAD_PAD)
        q_ref[0, :, cols] = (_rope(q[:, cols], cm, s1, s2) * QK_SCALE_LOG2E).astype(BF16)
        k_ref[0, :, cols] = (k[:, cols] + kr).astype(BF16)
    vt = _dot_nt(wuvt_ref[...], ckvn).astype(BF16)
    tm = vt.shape[1]
    row = lax.broadcasted_iota(jnp.int32, (V_ROWS - V_HEAD, tm), 0)
    ones_row = jnp.where(row == 0, 1.0, 0.0).astype(BF16)
    for h in range(B_HEADS):
        vt_ref[0, h * V_ROWS:h * V_ROWS + V_HEAD, :] = vt[h * V_HEAD:(h + 1) * V_HEAD]
        vt_ref[0, h * V_ROWS + V_HEAD:(h + 1) * V_ROWS, :] = ones_row


def _mla_weights(w_dkv, w_uq, w_ukv):
    d = w_dkv.shape[0]
    pad_l = jnp.zeros((d, QK_NOPE), F32)
    pad_r = jnp.zeros((d, HEAD_PAD - QK_NOPE - QK_ROPE), F32)
    wd = jnp.concatenate([w_dkv[:, :Q_LORA + KV_LORA], pad_l, w_dkv[:, Q_LORA + KV_LORA:], pad_r], axis=1)
    wq = w_uq.reshape(Q_LORA, B_HEADS, QK_NOPE + QK_ROPE)
    wq = jnp.pad(wq, ((0, 0), (0, 0), (0, HEAD_PAD - QK_NOPE - QK_ROPE))).reshape(Q_LORA, B_HEADS * HEAD_PAD)
    wkv = w_ukv.reshape(KV_LORA, B_HEADS, QK_NOPE + V_HEAD)
    wk = jnp.pad(wkv[:, :, :QK_NOPE], ((0, 0), (0, 0), (0, HEAD_PAD - QK_NOPE))).reshape(KV_LORA, B_HEADS * HEAD_PAD)
    wvt = wkv[:, :, QK_NOPE:].reshape(KV_LORA, B_HEADS * V_HEAD).T
    return wd.astype(BF16), wq.astype(BF16), wk.astype(BF16), wvt.astype(BF16)


def _mla_proj(x3d, w_dkv, g_q, g_kv, w_uq, w_ukv, tm):
    b, s, d = x3d.shape
    wd, wq, wk, wvt = _mla_weights(w_dkv, w_uq, w_ukv)
    cm, s1, s2 = _rope_tables(s)
    hq = B_HEADS * HEAD_PAD
    hv = B_HEADS * V_ROWS
    tab = pl.BlockSpec((tm, HEAD_PAD), lambda bi, si: (si, 0))
    return pl.pallas_call(
        _mla_proj_kernel,
        grid=(b, s // tm),
        in_specs=[pl.BlockSpec((1, tm, d), lambda bi, si: (bi, si, 0)),
                  _resident(wd.shape), _resident((1, Q_LORA)), _resident((1, KV_LORA)),
                  _resident(wq.shape), _resident(wk.shape), _resident(wvt.shape), tab, tab, tab],
        out_specs=[pl.BlockSpec((1, tm, hq), lambda bi, si: (bi, si, 0)),
                   pl.BlockSpec((1, tm, hq), lambda bi, si: (bi, si, 0)),
                   pl.BlockSpec((1, hv, tm), lambda bi, si: (bi, 0, si))],
        out_shape=[jax.ShapeDtypeStruct((b, s, hq), BF16),
                   jax.ShapeDtypeStruct((b, s, hq), BF16),
                   jax.ShapeDtypeStruct((b, hv, s), BF16)],
        compiler_params=_cparams(("parallel", "parallel")),
        name="mla_proj",
    )(x3d, wd, g_q[None], g_kv[None], wq, wk, wvt, cm, s1, s2)


def _mla_attn_kernel(q_ref, k_ref, vt_ref, o_ref, s_scr, p_scr, m_scr, a_scr, acc_scr, *, tq, tk):
    seq = k_ref.shape[1]
    nk = seq // tk
    n_q = q_ref.shape[1] // tq

    def run_step(t, par, do_qk=True, do_sm=True, do_pv=True):
        if do_pv:
            v0 = pl.multiple_of((t - 2) * tk, tk)
            vt = vt_ref[0, :, pl.ds(v0, tk)]
            pvs = [_dot(vt, p_scr[par, j]) for j in range(n_q)]
        if do_qk:
            k0 = pl.multiple_of(t * tk, tk)
            k = k_ref[0, pl.ds(k0, tk), :]
            scores = [_dot_nt(k, q_ref[0, j * tq:(j + 1) * tq, :]) for j in range(n_q)]
        for j in range(n_q):
            if do_pv:
                acc_scr[j] = a_scr[j] * acc_scr[j] + pvs[j]
            if do_sm:
                s = s_scr[1 - par, j]
                m = m_scr[j]
                m_new = jnp.maximum(m, jnp.max(s, axis=0, keepdims=True))
                a_scr[j] = jnp.exp2(m - m_new)
                p_scr[1 - par, j] = jnp.exp2(s - m_new).astype(BF16)
                m_scr[j] = m_new
        if do_qk:
            for j in range(n_q):
                s_scr[par, j] = scores[j]

    m_scr[...] = jnp.full(m_scr.shape, NEG_INF, F32)
    a_scr[...] = jnp.zeros(a_scr.shape, F32)
    acc_scr[...] = jnp.zeros(acc_scr.shape, F32)
    run_step(0, 0, do_sm=False, do_pv=False)
    run_step(1, 1, do_pv=False)

    def pair(u, carry):
        t = 2 * u + 2
        run_step(t, 0)
        run_step(t + 1, 1)
        return carry

    lax.fori_loop(0, (nk - 2) // 2, pair, 0)
    run_step(nk, nk % 2, do_qk=False)
    run_step(nk + 1, (nk + 1) % 2, do_qk=False, do_sm=False)
    for j in range(n_q):
        acc = acc_scr[j]
        o_ref[0, :, j * tq:(j + 1) * tq] = (acc[:V_HEAD] / acc[V_HEAD:V_HEAD + 1]).astype(BF16)


def _mla_attn(q, k, vt, tq_block, tq, tk):
    b, s, _ = q.shape
    n_q = tq_block // tq
    assert (s // tk) % 2 == 0 and s // tk >= 2
    return pl.pallas_call(
        functools.partial(_mla_attn_kernel, tq=tq, tk=tk),
        grid=(b, B_HEADS, s // tq_block),
        in_specs=[pl.BlockSpec((1, tq_block, HEAD_PAD), lambda bi, h, qi: (bi, qi, h)),
                  pl.BlockSpec((1, s, HEAD_PAD), lambda bi, h, qi: (bi, 0, h)),
                  pl.BlockSpec((1, V_ROWS, s), lambda bi, h, qi: (bi, h, 0))],
        out_specs=pl.BlockSpec((1, V_HEAD, tq_block), lambda bi, h, qi: (bi, h, qi)),
        out_shape=jax.ShapeDtypeStruct((b, B_HEADS * V_HEAD, s), BF16),
        scratch_shapes=[pltpu.VMEM((2, n_q, tk, tq), F32), pltpu.VMEM((2, n_q, tk, tq), BF16),
                        pltpu.VMEM((n_q, 1, tq), F32), pltpu.VMEM((n_q, 1, tq), F32),
                        pltpu.VMEM((n_q, V_ROWS, tq), F32)],
        compiler_params=_cparams(("parallel", "parallel", "arbitrary")),
        name="mla_attn",
    )(q, k, vt)


def _proj_t_ln_kernel(ot_ref, x_ref, w_ref, g_ref, b_ref, out_ref):
    y = ALPHA * x_ref[0] + _dot_tn(ot_ref[0], w_ref[...])
    out_ref[0] = _layer_norm(y, g_ref[...], b_ref[...])


def _proj_t_ln(ot, x3d, w_bf16, ln_g, ln_b, tm):
    b, s, d = x3d.shape
    k = ot.shape[1]
    return pl.pallas_call(
        _proj_t_ln_kernel,
        grid=(b, s // tm),
        in_specs=[pl.BlockSpec((1, k, tm), lambda bi, si: (bi, 0, si)),
                  pl.BlockSpec((1, tm, d), lambda bi, si: (bi, si, 0)),
                  _resident((k, d)), _resident((1, d)), _resident((1, d))],
        out_specs=pl.BlockSpec((1, tm, d), lambda bi, si: (bi, si, 0)),
        out_shape=jax.ShapeDtypeStruct((b, s, d), F32),
        compiler_params=_cparams(("parallel", "parallel")),
        name="proj_t_ln",
    )(ot, x3d, w_bf16, ln_g, ln_b)


def _encoder(x, bias, w_qkv_a, w_o_a, w_dkv_b, g_q_b, g_kv_b, w_uq_b, w_ukv_b, w_o_b,
             ffn_w_in, ffn_conv_w, ffn_conv_b, ffn_w_out, ln_g, ln_b):
    b, s, d = x.shape
    tm = 512
    for i in range(DEPTH):
        j = i // N_MIXERS
        g1, b1 = ln_g[i, 0][None], ln_b[i, 0][None]
        g2, b2 = ln_g[i, 1][None], ln_b[i, 1][None]
        if i % N_MIXERS == 0:
            qkv = _qkv_proj(x, w_qkv_a[j].astype(BF16), tm=256)
            outs = [_dilated_attn(qkv[g], bias, g) for g in range(A_GROUPS)]
            x = _merge_proj_ln([o for o, _ in outs], [l for _, l in outs], x,
                               w_o_a[j].astype(BF16), g1, b1, tm)
        else:
            q, k, vt = _mla_proj(x, w_dkv_b[j], g_q_b[j], g_kv_b[j], w_uq_b[j], w_ukv_b[j], tm)
            ot = _mla_attn(q, k, vt, tq_block=min(1024, s), tq=256, tk=256)
            x = _proj_t_ln(ot, x, w_o_b[j].astype(BF16), g1, b1, tm)
        x = _ffn_ln(x, ffn_w_in[i], ffn_conv_w[i], ffn_conv_b[i], ffn_w_out[i], g2, b2, tm)
    return x


def kernel(x_prompt, x_sample, rel_bias, w_qkv_a, w_o_a, w_dkv_b, g_q_b, g_kv_b, w_uq_b, w_ukv_b, w_o_b,
           ffn_w_in, ffn_conv_w, ffn_conv_b, ffn_w_out, ln_g, ln_b):
    bias = _bias_tiles(rel_bias)
    args = (bias, w_qkv_a, w_o_a, w_dkv_b, g_q_b, g_kv_b, w_uq_b, w_ukv_b, w_o_b,
            ffn_w_in, ffn_conv_w, ffn_conv_b, ffn_w_out, ln_g, ln_b)
    return (_encoder(x_prompt, *args), _encoder(x_sample, *args))
```

```python
import functools
import math

import jax
import jax.numpy as jnp
import numpy as np
from jax import lax
from jax.experimental import pallas as pl
from jax.experimental.pallas import tpu as pltpu

F32 = jnp.float32
BF16 = jnp.bfloat16

D_MODEL = 1024
DEPTH = 2
N_MIXERS = 2

A_WINDOWS = (128, 512, 2048)
A_DILATIONS = (1, 4, 16)
A_GROUPS = 3
A_HEADS = 8
A_HEAD_DIM = 128
A_WIDTH = A_HEADS * A_HEAD_DIM
A_HALF = 64
NUM_BUCKETS = 32
MAX_DISTANCE = max(A_WINDOWS) // 2

B_HEADS = 16
Q_LORA = 384
KV_LORA = 256
QK_NOPE = 64
QK_ROPE = 32
V_HEAD = 64
ROPE_THETA = 10000.0
HEAD_PAD = 128
V_ROWS = V_HEAD + 16
SM_ROWS = 64

D_FF = 2816
FF_CHUNK = 256

ALPHA = (2.0 * DEPTH) ** 0.25
LN_EPS = 1e-5
RMS_EPS = 1e-6
NEG_INF = -1e30

LANES = 128
SUBLANES = 8
VMEM_LIMIT = 56 * 1024 * 1024

QK_SCALE_LOG2E = np.float32((QK_NOPE + QK_ROPE) ** -0.5 * math.log2(math.e))


def _cparams(sem):
    return pltpu.CompilerParams(dimension_semantics=sem, vmem_limit_bytes=VMEM_LIMIT)


def _resident(shape):
    nd = len(shape)
    return pl.BlockSpec(shape, lambda *_: (0,) * nd, pipeline_mode=pl.Buffered(1))


def _layer_norm(y, g, b):
    mu = jnp.mean(y, axis=-1, keepdims=True)
    yc = y - mu
    var = jnp.mean(yc * yc, axis=-1, keepdims=True)
    return yc * lax.rsqrt(var + LN_EPS) * g + b


def _dot(a, b):
    return jnp.dot(a, b, preferred_element_type=F32)


def _dot_nt(a, b):
    return lax.dot_general(a, b, (((1,), (1,)), ((), ())), preferred_element_type=F32)


def _dot_tn(a, b):
    return lax.dot_general(a, b, (((0,), (0,)), ((), ())), preferred_element_type=F32)


def _qkv_proj_kernel(x_ref, w_ref, *refs, tm):
    outs, res_scr = refs[:A_GROUPS], refs[A_GROUPS]
    xb = x_ref[0].astype(BF16)
    slot = 0
    for g, dil in enumerate(A_DILATIONS):
        for c in range(3):
            n0 = (g * 3 + c) * A_WIDTH
            cols = slice(c * A_WIDTH, (c + 1) * A_WIDTH)
            res = _dot(xb, w_ref[:, n0:n0 + A_WIDTH])
            if dil == 1:
                outs[g][0, 0, :, cols] = res.astype(BF16)
                continue
            for h in range(A_HEADS):
                res_scr[slot, h] = res[:, h * LANES:(h + 1) * LANES]
            for r in range(dil):
                for h in range(A_HEADS):
                    lanes = slice(c * A_WIDTH + h * LANES, c * A_WIDTH + (h + 1) * LANES)
                    outs[g][0, r, :, lanes] = res_scr[slot, h, pl.ds(r, tm // dil, stride=dil), :].astype(BF16)
            slot = 1 - slot


def _qkv_proj(x3d, w_bf16, tm):
    b, s, d = x3d.shape
    width = 3 * A_WIDTH
    return pl.pallas_call(
        functools.partial(_qkv_proj_kernel, tm=tm),
        grid=(b, s // tm),
        in_specs=[pl.BlockSpec((1, tm, d), lambda bi, i: (bi, i, 0)), _resident(w_bf16.shape)],
        out_specs=[pl.BlockSpec((1, dil, tm // dil, width), lambda bi, i: (bi, 0, i, 0))
                   for dil in A_DILATIONS],
        out_shape=[jax.ShapeDtypeStruct((b, dil, s // dil, width), BF16) for dil in A_DILATIONS],
        scratch_shapes=[pltpu.VMEM((2, A_HEADS, tm, LANES), F32)],
        compiler_params=_cparams(("parallel", "parallel")),
        name="qkv_proj",
    )(x3d, w_bf16)


def _t5_bucket(rel):
    nb = NUM_BUCKETS // 2
    max_exact = nb // 2
    ret = jnp.where(rel > 0, nb, 0)
    n = jnp.abs(rel)
    large = max_exact + (jnp.log(jnp.maximum(n, 1).astype(F32) / max_exact)
                         / math.log(MAX_DISTANCE / max_exact) * (nb - max_exact)).astype(jnp.int32)
    large = jnp.minimum(large, nb - 1)
    return ret + jnp.where(n < max_exact, n, large)


def _band_bucket_index():
    qi = jnp.arange(2 * A_HALF)[:, None]
    kj = jnp.arange(4 * A_HALF)[None, :]
    rel = kj - A_HALF - qi
    tiles = [jnp.where(jnp.abs(rel) <= A_HALF, _t5_bucket(rel * dil), -1) for dil in A_DILATIONS]
    return jnp.stack(tiles, axis=0).astype(jnp.int32)


def _bias_tile_kernel(tab_ref, idx_ref, o_ref):
    col = pl.program_id(0) * A_HEADS + pl.program_id(1)
    idx = idx_ref[0]
    acc = jnp.full(idx.shape, NEG_INF, F32)
    for bucket in range(NUM_BUCKETS):
        acc = jnp.where(idx == bucket, tab_ref[bucket, col], acc)
    o_ref[0, 0] = acc


def _bias_tiles(rel_bias):
    idx = _band_bucket_index()
    tq, tk = idx.shape[1:]
    return pl.pallas_call(
        _bias_tile_kernel,
        grid=(A_GROUPS, A_HEADS),
        in_specs=[pl.BlockSpec(memory_space=pltpu.SMEM),
                  pl.BlockSpec((1, tq, tk), lambda g, h: (g, 0, 0))],
        out_specs=pl.BlockSpec((1, 1, tq, tk), lambda g, h: (g, h, 0, 0)),
        out_shape=jax.ShapeDtypeStruct((A_GROUPS, A_HEADS, tq, tk), F32),
        compiler_params=_cparams(("arbitrary", "arbitrary")),
        name="bias_tiles",
    )(rel_bias, idx)


def _dilated_attn_kernel(q_ref, kl_ref, km_ref, kr_ref, vl_ref, vm_ref, vr_ref, bias_ref,
                         o_ref, lse_ref, kbuf, vbuf, *, tq, sub_len):
    li = pl.program_id(2)
    kbuf[0:A_HALF] = kl_ref[0, 0]
    kbuf[A_HALF:A_HALF + tq] = km_ref[0, 0]
    kbuf[A_HALF + tq:] = kr_ref[0, 0]
    vbuf[0:A_HALF] = vl_ref[0, 0]
    vbuf[A_HALF:A_HALF + tq] = vm_ref[0, 0]
    vbuf[A_HALF + tq:] = vr_ref[0, 0]

    scale = A_HEAD_DIM ** -0.5
    qt, kt = 2 * A_HALF, 4 * A_HALF
    lane = lax.broadcasted_iota(jnp.int32, (qt, LANES), 1)
    for t in range(tq // qt):
        pos = li * tq + t * qt - A_HALF + lax.broadcasted_iota(jnp.int32, (1, kt), 1)
        edge = jnp.where((pos >= 0) & (pos < sub_len), 0.0, NEG_INF).astype(F32)
        lse_tile = jnp.zeros((qt, LANES), F32)
        for h in range(A_HEADS):
            cols = slice(h * A_HEAD_DIM, (h + 1) * A_HEAD_DIM)
            q = q_ref[0, 0, t * qt:(t + 1) * qt, cols]
            k = kbuf[t * qt:t * qt + kt, cols]
            v = vbuf[t * qt:t * qt + kt, cols]
            s = _dot_nt(q, k) * scale + bias_ref[0, h] + edge
            m = jnp.max(s, axis=-1, keepdims=True)
            p = jnp.exp(s - m)
            den = jnp.sum(p, axis=-1, keepdims=True)
            o = _dot(p.astype(BF16), v) / den
            o_ref[0, 0, t * qt:(t + 1) * qt, cols] = o.astype(BF16)
            lse_tile = jnp.where(lane == h, m + jnp.log(den), lse_tile)
        lse_ref[0, 0, t * qt:(t + 1) * qt, :] = lse_tile


def _dilated_attn(qkv_g, bias, g):
    batch, dil, sub_len, _ = qkv_g.shape
    tq = min(256, sub_len)
    n_t = sub_len // tq
    hb = tq // A_HALF
    n_hb = sub_len // A_HALF

    main = lambda which: pl.BlockSpec((1, 1, tq, A_WIDTH), lambda b, r, i: (b, r, i, which))
    left = lambda which: pl.BlockSpec(
        (1, 1, A_HALF, A_WIDTH), lambda b, r, i: (b, r, jnp.maximum(i * hb - 1, 0), which))
    right = lambda which: pl.BlockSpec(
        (1, 1, A_HALF, A_WIDTH), lambda b, r, i: (b, r, jnp.minimum((i + 1) * hb, n_hb - 1), which))

    return pl.pallas_call(
        functools.partial(_dilated_attn_kernel, tq=tq, sub_len=sub_len),
        grid=(batch, dil, n_t),
        in_specs=[main(0), left(1), main(1), right(1), left(2), main(2), right(2),
                  pl.BlockSpec((1, A_HEADS, 2 * A_HALF, 4 * A_HALF), lambda b, r, i: (g, 0, 0, 0))],
        out_specs=[pl.BlockSpec((1, 1, tq, A_WIDTH), lambda b, r, i: (b, r, i, 0)),
                   pl.BlockSpec((1, 1, tq, LANES), lambda b, r, i: (b, r, i, 0))],
        out_shape=[jax.ShapeDtypeStruct((batch, dil, sub_len, A_WIDTH), BF16),
                   jax.ShapeDtypeStruct((batch, dil, sub_len, LANES), F32)],
        scratch_shapes=[pltpu.VMEM((tq + 2 * A_HALF, A_WIDTH), BF16),
                        pltpu.VMEM((tq + 2 * A_HALF, A_WIDTH), BF16)],
        compiler_params=_cparams(("parallel", "parallel", "parallel")),
        name=f"dilated_attn_g{g}",
    )(qkv_g, qkv_g, qkv_g, qkv_g, qkv_g, qkv_g, qkv_g, bias)


def _merge_proj_ln_kernel(o0_ref, o1_ref, o2_ref, l0_ref, l1_ref, l2_ref, x_ref, w_ref, g_ref, b_ref,
                          out_ref, merged, o_scr, l_scr, *, tm):
    for g, (o_ref, l_ref, dil) in enumerate(zip((o0_ref, o1_ref, o2_ref), (l0_ref, l1_ref, l2_ref),
                                                A_DILATIONS)):
        for r in range(dil):
            rows = pl.ds(r, tm // dil, stride=dil) if dil > 1 else slice(None)
            l_scr[g, rows, :] = l_ref[0, r]
            for h in range(A_HEADS):
                o_scr[g, h, rows, :] = o_ref[0, r, :, h * A_HEAD_DIM:(h + 1) * A_HEAD_DIM].astype(F32)
    l0, l1, l2 = l_scr[0], l_scr[1], l_scr[2]
    m = jnp.maximum(jnp.maximum(l0, l1), l2)
    e0, e1, e2 = jnp.exp(l0 - m), jnp.exp(l1 - m), jnp.exp(l2 - m)
    den = e0 + e1 + e2
    w0, w1, w2 = e0 / den, e1 / den, e2 / den
    for h in range(A_HEADS):
        cols = slice(h * A_HEAD_DIM, (h + 1) * A_HEAD_DIM)
        mix = (w0[:, h:h + 1] * o_scr[0, h] + w1[:, h:h + 1] * o_scr[1, h]
               + w2[:, h:h + 1] * o_scr[2, h])
        merged[:, cols] = mix.astype(BF16)
    y = ALPHA * x_ref[0] + _dot(merged[...], w_ref[...])
    out_ref[0] = _layer_norm(y, g_ref[...], b_ref[...])


def _merge_proj_ln(os, lses, x3d, w_bf16, ln_g, ln_b, tm):
    b, s, d = x3d.shape
    plane = lambda w: [pl.BlockSpec((1, dil, tm // dil, w), lambda bi, i: (bi, 0, i, 0))
                       for dil in A_DILATIONS]
    row = pl.BlockSpec((1, tm, d), lambda bi, i: (bi, i, 0))
    return pl.pallas_call(
        functools.partial(_merge_proj_ln_kernel, tm=tm),
        grid=(b, s // tm),
        in_specs=plane(A_WIDTH) + plane(LANES) + [row, _resident((A_WIDTH, d)),
                                                  _resident((1, d)), _resident((1, d))],
        out_specs=row,
        out_shape=jax.ShapeDtypeStruct((b, s, d), F32),
        scratch_shapes=[pltpu.VMEM((tm, A_WIDTH), BF16),
                        pltpu.VMEM((A_GROUPS, A_HEADS, tm, A_HEAD_DIM), F32),
                        pltpu.VMEM((A_GROUPS, tm, LANES), F32)],
        compiler_params=_cparams(("parallel", "parallel")),
        name="merge_proj_ln",
    )(*os, *lses, x3d, w_bf16, ln_g, ln_b)


def _ffn_ln_kernel(x_ref, xp_ref, xn_ref, win_ref, cw_ref, cb_ref, wout_ref, g_ref, b_ref,
                   out_ref, act, *, tm):
    si = pl.program_id(1)
    n_s = pl.num_programs(1)
    x = x_ref[0]
    xb = x.astype(BF16)
    prev_ok = (si > 0).astype(F32)
    next_ok = (si < n_s - 1).astype(F32)
    xe = jnp.concatenate([xp_ref[0] * prev_ok, xn_ref[0] * next_ok], axis=0).astype(BF16)
    row = lax.broadcasted_iota(jnp.int32, (tm, 2 * FF_CHUNK), 0)
    for c in range(D_FF // FF_CHUNK):
        cols = slice(c * 2 * FF_CHUNK, (c + 1) * 2 * FF_CHUNK)
        w = win_ref[:, cols]
        h = _dot(xb, w)
        he = _dot(xe, w)
        up = jnp.where(row == 0, he[SUBLANES - 1:SUBLANES], pltpu.roll(h, 1, 0))
        dn = jnp.where(row == tm - 1, he[SUBLANES:SUBLANES + 1], pltpu.roll(h, tm - 1, 0))
        cw = cw_ref[:, cols]
        hc = up * cw[0:1] + h * cw[1:2] + dn * cw[2:3] + cb_ref[:, cols]
        a = hc[:, :FF_CHUNK]
        gate = hc[:, FF_CHUNK:]
        gelu = 0.5 * gate * (1.0 + lax.erf(gate * np.float32(math.sqrt(0.5))))
        act[:, c * FF_CHUNK:(c + 1) * FF_CHUNK] = (a * gelu).astype(BF16)
    y = ALPHA * x + _dot(act[...], wout_ref[...])
    out_ref[0] = _layer_norm(y, g_ref[...], b_ref[...])


def _chunk_interleave(w):
    lead = w.shape[:-1]
    w = w.reshape(lead + (2, D_FF // FF_CHUNK, FF_CHUNK))
    w = jnp.swapaxes(w, -3, -2)
    return w.reshape(lead + (2 * D_FF,))


def _ffn_ln(x3d, w_in, conv_w, conv_b, w_out, ln_g, ln_b, tm):
    b, s, d = x3d.shape
    nb8 = tm // SUBLANES
    n8 = s // SUBLANES
    win = _chunk_interleave(w_in).astype(BF16)
    cw = _chunk_interleave(conv_w)
    cb = _chunk_interleave(conv_b)[None]
    return pl.pallas_call(
        functools.partial(_ffn_ln_kernel, tm=tm),
        grid=(b, s // tm),
        in_specs=[pl.BlockSpec((1, tm, d), lambda bi, si: (bi, si, 0)),
                  pl.BlockSpec((1, SUBLANES, d), lambda bi, si: (bi, jnp.maximum(si * nb8 - 1, 0), 0)),
                  pl.BlockSpec((1, SUBLANES, d), lambda bi, si: (bi, jnp.minimum((si + 1) * nb8, n8 - 1), 0)),
                  _resident((d, 2 * D_FF)), _resident((3, 2 * D_FF)), _resident((1, 2 * D_FF)),
                  _resident((D_FF, d)), _resident((1, d)), _resident((1, d))],
        out_specs=pl.BlockSpec((1, tm, d), lambda bi, si: (bi, si, 0)),
        out_shape=jax.ShapeDtypeStruct((b, s, d), F32),
        scratch_shapes=[pltpu.VMEM((tm, D_FF), BF16)],
        compiler_params=_cparams(("parallel", "parallel")),
        name="ffn_ln",
    )(x3d, x3d, x3d, win, cw, cb, w_out.astype(BF16), ln_g, ln_b)


def _rope_tables(seq):
    inv = 1.0 / (ROPE_THETA ** (jnp.arange(0, QK_ROPE, 2, dtype=F32) / QK_ROPE))
    ang = jnp.arange(seq, dtype=F32)[:, None] * inv[None, :]
    cos, sin = jnp.cos(ang), jnp.sin(ang)
    half = QK_ROPE // 2
    ones = jnp.ones((seq, QK_NOPE), F32)
    z_nope = jnp.zeros((seq, QK_NOPE), F32)
    z_half = jnp.zeros((seq, half), F32)
    z_pad = jnp.zeros((seq, HEAD_PAD - QK_NOPE - QK_ROPE), F32)
    cmul = jnp.concatenate([ones, cos, cos, z_pad], axis=1)
    s_first = jnp.concatenate([z_nope, -sin, z_half, z_pad], axis=1)
    s_second = jnp.concatenate([z_nope, z_half, sin, z_pad], axis=1)
    return cmul, s_first, s_second


def _rope(t, cmul, s_first, s_second):
    half = QK_ROPE // 2
    return (t * cmul + pltpu.roll(t, HEAD_PAD - half, 1) * s_first
            + pltpu.roll(t, half, 1) * s_second)


def _rms_norm(x, g):
    ms = jnp.mean(x * x, axis=-1, keepdims=True)
    return x * lax.rsqrt(ms + RMS_EPS) * g


def _mla_proj_kernel(x_ref, wd_ref, gq_ref, gkv_ref, wuq_ref, wuk_ref, wuvt_ref,
                     cm_ref, s1_ref, s2_ref, q_ref, k_ref, vt_ref):
    xb = x_ref[0].astype(BF16)
    c = _dot(xb, wd_ref[...])
    cqn = _rms_norm(c[:, :Q_LORA], gq_ref[...]).astype(BF16)
    ckvn = _rms_norm(c[:, Q_LORA:Q_LORA + KV_LORA], gkv_ref[...]).astype(BF16)
    cm, s1, s2 = cm_ref[...], s1_ref[...], s2_ref[...]
    kr = _rope(c[:, Q_LORA + KV_LORA:], cm, s1, s2)
    q = _dot(cqn, wuq_ref[...])
    k = _dot(ckvn, wuk_ref[...])
    for h in range(B_HEADS):
        cols = slice(h * HEAD_PAD, (h + 1) * HEAD_PAD)
        q_ref[0, :, cols] = (_rope(q[:, cols], cm, s1, s2) * QK_SCALE_LOG2E).astype(BF16)
        k_ref[0, :, cols] = (k[:, cols] + kr).astype(BF16)
    vt = _dot_nt(wuvt_ref[...], ckvn).astype(BF16)
    tm = vt.shape[1]
    row = lax.broadcasted_iota(jnp.int32, (V_ROWS - V_HEAD, tm), 0)
    ones_row = jnp.where(row == 0, 1.0, 0.0).astype(BF16)
    for h in range(B_HEADS):
        vt_ref[0, h * V_ROWS:h * V_ROWS + V_HEAD, :] = vt[h * V_HEAD:(h + 1) * V_HEAD]
        vt_ref[0, h * V_ROWS + V_HEAD:(h + 1) * V_ROWS, :] = ones_row


def _mla_weights(w_dkv, w_uq, w_ukv):
    d = w_dkv.shape[0]
    pad_l = jnp.zeros((d, QK_NOPE), F32)
    pad_r = jnp.zeros((d, HEAD_PAD - QK_NOPE - QK_ROPE), F32)
    wd = jnp.concatenate([w_dkv[:, :Q_LORA + KV_LORA], pad_l, w_dkv[:, Q_LORA + KV_LORA:], pad_r], axis=1)
    wq = w_uq.reshape(Q_LORA, B_HEADS, QK_NOPE + QK_ROPE)
    wq = jnp.pad(wq, ((0, 0), (0, 0), (0, HEAD_PAD - QK_NOPE - QK_ROPE))).reshape(Q_LORA, B_HEADS * HEAD_PAD)
    wkv = w_ukv.reshape(KV_LORA, B_HEADS, QK_NOPE + V_HEAD)
    wk = jnp.pad(wkv[:, :, :QK_NOPE], ((0, 0), (0, 0), (0, HEAD_PAD - QK_NOPE))).reshape(KV_LORA, B_HEADS * HEAD_PAD)
    wvt = wkv[:, :, QK_NOPE:].reshape(KV_LORA, B_HEADS * V_HEAD).T
    return wd.astype(BF16), wq.astype(BF16), wk.astype(BF16), wvt.astype(BF16)


def _mla_proj(x3d, w_dkv, g_q, g_kv, w_uq, w_ukv, tm):
    b, s, d = x3d.shape
    wd, wq, wk, wvt = _mla_weights(w_dkv, w_uq, w_ukv)
    cm, s1, s2 = _rope_tables(s)
    hq = B_HEADS * HEAD_PAD
    hv = B_HEADS * V_ROWS
    tab = pl.BlockSpec((tm, HEAD_PAD), lambda bi, si: (si, 0))
    return pl.pallas_call(
        _mla_proj_kernel,
        grid=(b, s // tm),
        in_specs=[pl.BlockSpec((1, tm, d), lambda bi, si: (bi, si, 0)),
                  _resident(wd.shape), _resident((1, Q_LORA)), _resident((1, KV_LORA)),
                  _resident(wq.shape), _resident(wk.shape), _resident(wvt.shape), tab, tab, tab],
        out_specs=[pl.BlockSpec((1, tm, hq), lambda bi, si: (bi, si, 0)),
                   pl.BlockSpec((1, tm, hq), lambda bi, si: (bi, si, 0)),
                   pl.BlockSpec((1, hv, tm), lambda bi, si: (bi, 0, si))],
        out_shape=[jax.ShapeDtypeStruct((b, s, hq), BF16),
                   jax.ShapeDtypeStruct((b, s, hq), BF16),
                   jax.ShapeDtypeStruct((b, hv, s), BF16)],
        compiler_params=_cparams(("parallel", "parallel")),
        name="mla_proj",
    )(x3d, wd, g_q[None], g_kv[None], wq, wk, wvt, cm, s1, s2)


def _mla_attn_kernel(q_ref, k_ref, vt_ref, o_ref, s_scr, p_scr, cmax_scr, m_scr, a_scr, acc_scr, *, tq, tk,
                     unroll):
    seq = k_ref.shape[1]
    nk = seq // tk
    n_q = q_ref.shape[1] // tq

    def run_step(t, par, do_qk=True, do_sm=True, do_pv=True):
        if do_pv:
            v0 = pl.multiple_of((t - 2) * tk, tk)
            vt = vt_ref[0, :, pl.ds(v0, tk)]
        if do_qk:
            k0 = pl.multiple_of(t * tk, tk)
            k = k_ref[0, pl.ds(k0, tk), :]
        pvs, scores = [], []
        if do_pv:
            pvs = [_dot(vt, p_scr[par, j]) for j in range(n_q)]
        if do_qk:
            scores = [_dot_nt(k, q_ref[0, j * tq:(j + 1) * tq, :]) for j in range(n_q)]
        for j in range(n_q):
            if do_pv:
                acc_scr[j] = a_scr[j] * acc_scr[j] + pvs[j]
            if do_sm:
                m = m_scr[j]
                m_new = jnp.maximum(m, cmax_scr[1 - par, j])
                a_scr[j] = jnp.exp2(m - m_new)
                m_scr[j] = m_new
                for r0 in range(0, tk, SM_ROWS):
                    rows = slice(r0, r0 + SM_ROWS)
                    p_scr[1 - par, j, rows, :] = jnp.exp2(s_scr[1 - par, j, rows, :] - m_new).astype(BF16)
        if do_qk:
            for j in range(n_q):
                s_scr[par, j] = scores[j]
                cmax_scr[par, j] = jnp.max(scores[j], axis=0, keepdims=True)

    m_scr[...] = jnp.full(m_scr.shape, NEG_INF, F32)
    a_scr[...] = jnp.zeros(a_scr.shape, F32)
    acc_scr[...] = jnp.zeros(acc_scr.shape, F32)
    run_step(0, 0, do_sm=False, do_pv=False)
    run_step(1, 1, do_pv=False)

    def steps(u, carry):
        t = unroll * u + 2
        for i in range(unroll):
            run_step(t + i, i % 2)
        return carry

    lax.fori_loop(0, (nk - 2) // unroll, steps, 0)
    run_step(nk, nk % 2, do_qk=False)
    run_step(nk + 1, (nk + 1) % 2, do_qk=False, do_sm=False)
    for j in range(n_q):
        acc = acc_scr[j]
        o_ref[0, :, j * tq:(j + 1) * tq] = (acc[:V_HEAD] / acc[V_HEAD:V_HEAD + 1]).astype(BF16)


def _mla_attn(q, k, vt, tq_block, tq, tk):
    b, s, _ = q.shape
    n_q = tq_block // tq
    nk = s // tk
    unroll = max(u for u in (2, 4, 6) if (nk - 2) % u == 0)
    assert nk >= 4 and (nk - 2) % unroll == 0
    return pl.pallas_call(
        functools.partial(_mla_attn_kernel, tq=tq, tk=tk, unroll=unroll),
        grid=(b, B_HEADS, s // tq_block),
        in_specs=[pl.BlockSpec((1, tq_block, HEAD_PAD), lambda bi, h, qi: (bi, qi, h)),
                  pl.BlockSpec((1, s, HEAD_PAD), lambda bi, h, qi: (bi, 0, h)),
                  pl.BlockSpec((1, V_ROWS, s), lambda bi, h, qi: (bi, h, 0))],
        out_specs=pl.BlockSpec((1, V_HEAD, tq_block), lambda bi, h, qi: (bi, h, qi)),
        out_shape=jax.ShapeDtypeStruct((b, B_HEADS * V_HEAD, s), BF16),
        scratch_shapes=[pltpu.VMEM((2, n_q, tk, tq), F32), pltpu.VMEM((2, n_q, tk, tq), BF16),
                        pltpu.VMEM((2, n_q, 1, tq), F32),
                        pltpu.VMEM((n_q, 1, tq), F32), pltpu.VMEM((n_q, 1, tq), F32),
                        pltpu.VMEM((n_q, V_ROWS, tq), F32)],
        compiler_params=_cparams(("parallel", "parallel", "arbitrary")),
        name="mla_attn",
    )(q, k, vt)


def _proj_t_ln_kernel(ot_ref, x_ref, w_ref, g_ref, b_ref, out_ref):
    y = ALPHA * x_ref[0] + _dot_tn(ot_ref[0], w_ref[...])
    out_ref[0] = _layer_norm(y, g_ref[...], b_ref[...])


def _proj_t_ln(ot, x3d, w_bf16, ln_g, ln_b, tm):
    b, s, d = x3d.shape
    k = ot.shape[1]
    return pl.pallas_call(
        _proj_t_ln_kernel,
        grid=(b, s // tm),
        in_specs=[pl.BlockSpec((1, k, tm), lambda bi, si: (bi, 0, si)),
                  pl.BlockSpec((1, tm, d), lambda bi, si: (bi, si, 0)),
                  _resident((k, d)), _resident((1, d)), _resident((1, d))],
        out_specs=pl.BlockSpec((1, tm, d), lambda bi, si: (bi, si, 0)),
        out_shape=jax.ShapeDtypeStruct((b, s, d), F32),
        compiler_params=_cparams(("parallel", "parallel")),
        name="proj_t_ln",
    )(ot, x3d, w_bf16, ln_g, ln_b)


def _encoder(x, bias, w_qkv_a, w_o_a, w_dkv_b, g_q_b, g_kv_b, w_uq_b, w_ukv_b, w_o_b,
             ffn_w_in, ffn_conv_w, ffn_conv_b, ffn_w_out, ln_g, ln_b):
    b, s, d = x.shape
    tm = 512
    for i in range(DEPTH):
        j = i // N_MIXERS
        g1, b1 = ln_g[i, 0][None], ln_b[i, 0][None]
        g2, b2 = ln_g[i, 1][None], ln_b[i, 1][None]
        if i % N_MIXERS == 0:
            qkv = _qkv_proj(x, w_qkv_a[j].astype(BF16), tm=256)
            outs = [_dilated_attn(qkv[g], bias, g) for g in range(A_GROUPS)]
            x = _merge_proj_ln([o for o, _ in outs], [l for _, l in outs], x,
                               w_o_a[j].astype(BF16), g1, b1, tm)
        else:
            q, k, vt = _mla_proj(x, w_dkv_b[j], g_q_b[j], g_kv_b[j], w_uq_b[j], w_ukv_b[j], tm)
            ot = _mla_attn(q, k, vt, tq_block=min(1024, s), tq=256, tk=256)
            x = _proj_t_ln(ot, x, w_o_b[j].astype(BF16), g1, b1, tm)
        x = _ffn_ln(x, ffn_w_in[i], ffn_conv_w[i], ffn_conv_b[i], ffn_w_out[i], g2, b2, tm)
    return x


def kernel(x_prompt, x_sample, rel_bias, w_qkv_a, w_o_a, w_dkv_b, g_q_b, g_kv_b, w_uq_b, w_ukv_b, w_o_b,
           ffn_w_in, ffn_conv_w, ffn_conv_b, ffn_w_out, ln_g, ln_b):
    bias = _bias_tiles(rel_bias)
    args = (bias, w_qkv_a, w_o_a, w_dkv_b, g_q_b, g_kv_b, w_uq_b, w_ukv_b, w_o_b,
            ffn_w_in, ffn_conv_w, ffn_conv_b, ffn_w_out, ln_g, ln_b)
    return (_encoder(x_prompt, *args), _encoder(x_sample, *args))
```

```python
import functools
import math

import jax
import jax.numpy as jnp
import numpy as np
from jax import lax
from jax.experimental import pallas as pl
from jax.experimental.pallas import tpu as pltpu

F32 = jnp.float32
BF16 = jnp.bfloat16

D_MODEL = 1024
DEPTH = 2
N_MIXERS = 2

A_WINDOWS = (128, 512, 2048)
A_DILATIONS = (1, 4, 16)
A_GROUPS = 3
A_HEADS = 8
A_HEAD_DIM = 128
A_WIDTH = A_HEADS * A_HEAD_DIM
A_HALF = 64
NUM_BUCKETS = 32
MAX_DISTANCE = max(A_WINDOWS) // 2

B_HEADS = 16
Q_LORA = 384
KV_LORA = 256
QK_NOPE = 64
QK_ROPE = 32
V_HEAD = 64
ROPE_THETA = 10000.0
HEAD_PAD = 128
V_ROWS = V_HEAD + 16
SM_ROWS = 64
N_SLOTS = 2
MAX_UNROLL = 6

D_FF = 2816
FF_CHUNK = 256

ALPHA = (2.0 * DEPTH) ** 0.25
LN_EPS = 1e-5
RMS_EPS = 1e-6
NEG_INF = -1e30

LANES = 128
SUBLANES = 8
VMEM_LIMIT = 56 * 1024 * 1024

QK_SCALE_LOG2E = np.float32((QK_NOPE + QK_ROPE) ** -0.5 * math.log2(math.e))


def _cparams(sem):
    return pltpu.CompilerParams(dimension_semantics=sem, vmem_limit_bytes=VMEM_LIMIT)


def _resident(shape):
    nd = len(shape)
    return pl.BlockSpec(shape, lambda *_: (0,) * nd, pipeline_mode=pl.Buffered(1))


def _layer_norm(y, g, b):
    mu = jnp.mean(y, axis=-1, keepdims=True)
    yc = y - mu
    var = jnp.mean(yc * yc, axis=-1, keepdims=True)
    return yc * lax.rsqrt(var + LN_EPS) * g + b


def _dot(a, b):
    return jnp.dot(a, b, preferred_element_type=F32)


def _dot_nt(a, b):
    return lax.dot_general(a, b, (((1,), (1,)), ((), ())), preferred_element_type=F32)


def _dot_tn(a, b):
    return lax.dot_general(a, b, (((0,), (0,)), ((), ())), preferred_element_type=F32)


def _qkv_proj_kernel(x_ref, w_ref, *refs, tm):
    outs, res_scr = refs[:A_GROUPS], refs[A_GROUPS]
    xb = x_ref[0].astype(BF16)
    slot = 0
    for g, dil in enumerate(A_DILATIONS):
        for c in range(3):
            n0 = (g * 3 + c) * A_WIDTH
            cols = slice(c * A_WIDTH, (c + 1) * A_WIDTH)
            res = _dot(xb, w_ref[:, n0:n0 + A_WIDTH])
            if dil == 1:
                outs[g][0, 0, :, cols] = res.astype(BF16)
                continue
            for h in range(A_HEADS):
                res_scr[slot, h] = res[:, h * LANES:(h + 1) * LANES]
            for r in range(dil):
                for h in range(A_HEADS):
                    lanes = slice(c * A_WIDTH + h * LANES, c * A_WIDTH + (h + 1) * LANES)
                    outs[g][0, r, :, lanes] = res_scr[slot, h, pl.ds(r, tm // dil, stride=dil), :].astype(BF16)
            slot = 1 - slot


def _qkv_proj(x3d, w_bf16, tm):
    b, s, d = x3d.shape
    width = 3 * A_WIDTH
    return pl.pallas_call(
        functools.partial(_qkv_proj_kernel, tm=tm),
        grid=(b, s // tm),
        in_specs=[pl.BlockSpec((1, tm, d), lambda bi, i: (bi, i, 0)), _resident(w_bf16.shape)],
        out_specs=[pl.BlockSpec((1, dil, tm // dil, width), lambda bi, i: (bi, 0, i, 0))
                   for dil in A_DILATIONS],
        out_shape=[jax.ShapeDtypeStruct((b, dil, s // dil, width), BF16) for dil in A_DILATIONS],
        scratch_shapes=[pltpu.VMEM((2, A_HEADS, tm, LANES), F32)],
        compiler_params=_cparams(("parallel", "parallel")),
        name="qkv_proj",
    )(x3d, w_bf16)


def _t5_bucket(rel):
    nb = NUM_BUCKETS // 2
    max_exact = nb // 2
    ret = jnp.where(rel > 0, nb, 0)
    n = jnp.abs(rel)
    large = max_exact + (jnp.log(jnp.maximum(n, 1).astype(F32) / max_exact)
                         / math.log(MAX_DISTANCE / max_exact) * (nb - max_exact)).astype(jnp.int32)
    large = jnp.minimum(large, nb - 1)
    return ret + jnp.where(n < max_exact, n, large)


def _band_bucket_index():
    qi = jnp.arange(2 * A_HALF)[:, None]
    kj = jnp.arange(4 * A_HALF)[None, :]
    rel = kj - A_HALF - qi
    tiles = [jnp.where(jnp.abs(rel) <= A_HALF, _t5_bucket(rel * dil), -1) for dil in A_DILATIONS]
    return jnp.stack(tiles, axis=0).astype(jnp.int32)


def _bias_tile_kernel(tab_ref, idx_ref, o_ref):
    col = pl.program_id(0) * A_HEADS + pl.program_id(1)
    idx = idx_ref[0]
    acc = jnp.full(idx.shape, NEG_INF, F32)
    for bucket in range(NUM_BUCKETS):
        acc = jnp.where(idx == bucket, tab_ref[bucket, col], acc)
    o_ref[0, 0] = acc


def _bias_tiles(rel_bias):
    idx = _band_bucket_index()
    tq, tk = idx.shape[1:]
    return pl.pallas_call(
        _bias_tile_kernel,
        grid=(A_GROUPS, A_HEADS),
        in_specs=[pl.BlockSpec(memory_space=pltpu.SMEM),
                  pl.BlockSpec((1, tq, tk), lambda g, h: (g, 0, 0))],
        out_specs=pl.BlockSpec((1, 1, tq, tk), lambda g, h: (g, h, 0, 0)),
        out_shape=jax.ShapeDtypeStruct((A_GROUPS, A_HEADS, tq, tk), F32),
        compiler_params=_cparams(("arbitrary", "arbitrary")),
        name="bias_tiles",
    )(rel_bias, idx)


def _dilated_attn_kernel(q_ref, kl_ref, km_ref, kr_ref, vl_ref, vm_ref, vr_ref, bias_ref,
                         o_ref, lse_ref, kbuf, vbuf, *, tq, sub_len):
    li = pl.program_id(2)
    kbuf[0:A_HALF] = kl_ref[0, 0]
    kbuf[A_HALF:A_HALF + tq] = km_ref[0, 0]
    kbuf[A_HALF + tq:] = kr_ref[0, 0]
    vbuf[0:A_HALF] = vl_ref[0, 0]
    vbuf[A_HALF:A_HALF + tq] = vm_ref[0, 0]
    vbuf[A_HALF + tq:] = vr_ref[0, 0]

    scale = A_HEAD_DIM ** -0.5
    qt, kt = 2 * A_HALF, 4 * A_HALF
    lane = lax.broadcasted_iota(jnp.int32, (qt, LANES), 1)
    for t in range(tq // qt):
        pos = li * tq + t * qt - A_HALF + lax.broadcasted_iota(jnp.int32, (1, kt), 1)
        edge = jnp.where((pos >= 0) & (pos < sub_len), 0.0, NEG_INF).astype(F32)
        lse_tile = jnp.zeros((qt, LANES), F32)
        for h in range(A_HEADS):
            cols = slice(h * A_HEAD_DIM, (h + 1) * A_HEAD_DIM)
            q = q_ref[0, 0, t * qt:(t + 1) * qt, cols]
            k = kbuf[t * qt:t * qt + kt, cols]
            v = vbuf[t * qt:t * qt + kt, cols]
            s = _dot_nt(q, k) * scale + bias_ref[0, h] + edge
            m = jnp.max(s, axis=-1, keepdims=True)
            p = jnp.exp(s - m)
            den = jnp.sum(p, axis=-1, keepdims=True)
            o = _dot(p.astype(BF16), v) / den
            o_ref[0, 0, t * qt:(t + 1) * qt, cols] = o.astype(BF16)
            lse_tile = jnp.where(lane == h, m + jnp.log(den), lse_tile)
        lse_ref[0, 0, t * qt:(t + 1) * qt, :] = lse_tile


def _dilated_attn(qkv_g, bias, g):
    batch, dil, sub_len, _ = qkv_g.shape
    tq = min(256, sub_len)
    n_t = sub_len // tq
    hb = tq // A_HALF
    n_hb = sub_len // A_HALF

    main = lambda which: pl.BlockSpec((1, 1, tq, A_WIDTH), lambda b, r, i: (b, r, i, which))
    left = lambda which: pl.BlockSpec(
        (1, 1, A_HALF, A_WIDTH), lambda b, r, i: (b, r, jnp.maximum(i * hb - 1, 0), which))
    right = lambda which: pl.BlockSpec(
        (1, 1, A_HALF, A_WIDTH), lambda b, r, i: (b, r, jnp.minimum((i + 1) * hb, n_hb - 1), which))

    return pl.pallas_call(
        functools.partial(_dilated_attn_kernel, tq=tq, sub_len=sub_len),
        grid=(batch, dil, n_t),
        in_specs=[main(0), left(1), main(1), right(1), left(2), main(2), right(2),
                  pl.BlockSpec((1, A_HEADS, 2 * A_HALF, 4 * A_HALF), lambda b, r, i: (g, 0, 0, 0))],
        out_specs=[pl.BlockSpec((1, 1, tq, A_WIDTH), lambda b, r, i: (b, r, i, 0)),
                   pl.BlockSpec((1, 1, tq, LANES), lambda b, r, i: (b, r, i, 0))],
        out_shape=[jax.ShapeDtypeStruct((batch, dil, sub_len, A_WIDTH), BF16),
                   jax.ShapeDtypeStruct((batch, dil, sub_len, LANES), F32)],
        scratch_shapes=[pltpu.VMEM((tq + 2 * A_HALF, A_WIDTH), BF16),
                        pltpu.VMEM((tq + 2 * A_HALF, A_WIDTH), BF16)],
        compiler_params=_cparams(("parallel", "parallel", "parallel")),
        name=f"dilated_attn_g{g}",
    )(qkv_g, qkv_g, qkv_g, qkv_g, qkv_g, qkv_g, qkv_g, bias)


def _merge_proj_ln_kernel(o0_ref, o1_ref, o2_ref, l0_ref, l1_ref, l2_ref, x_ref, w_ref, g_ref, b_ref,
                          out_ref, merged, o_scr, l_scr, *, tm):
    for g, (o_ref, l_ref, dil) in enumerate(zip((o0_ref, o1_ref, o2_ref), (l0_ref, l1_ref, l2_ref),
                                                A_DILATIONS)):
        for r in range(dil):
            rows = pl.ds(r, tm // dil, stride=dil) if dil > 1 else slice(None)
            l_scr[g, rows, :] = l_ref[0, r]
            for h in range(A_HEADS):
                o_scr[g, h, rows, :] = o_ref[0, r, :, h * A_HEAD_DIM:(h + 1) * A_HEAD_DIM].astype(F32)
    l0, l1, l2 = l_scr[0], l_scr[1], l_scr[2]
    m = jnp.maximum(jnp.maximum(l0, l1), l2)
    e0, e1, e2 = jnp.exp(l0 - m), jnp.exp(l1 - m), jnp.exp(l2 - m)
    den = e0 + e1 + e2
    w0, w1, w2 = e0 / den, e1 / den, e2 / den
    for h in range(A_HEADS):
        cols = slice(h * A_HEAD_DIM, (h + 1) * A_HEAD_DIM)
        mix = (w0[:, h:h + 1] * o_scr[0, h] + w1[:, h:h + 1] * o_scr[1, h]
               + w2[:, h:h + 1] * o_scr[2, h])
        merged[:, cols] = mix.astype(BF16)
    y = ALPHA * x_ref[0] + _dot(merged[...], w_ref[...])
    out_ref[0] = _layer_norm(y, g_ref[...], b_ref[...])


def _merge_proj_ln(os, lses, x3d, w_bf16, ln_g, ln_b, tm):
    b, s, d = x3d.shape
    plane = lambda w: [pl.BlockSpec((1, dil, tm // dil, w), lambda bi, i: (bi, 0, i, 0))
                       for dil in A_DILATIONS]
    row = pl.BlockSpec((1, tm, d), lambda bi, i: (bi, i, 0))
    return pl.pallas_call(
        functools.partial(_merge_proj_ln_kernel, tm=tm),
        grid=(b, s // tm),
        in_specs=plane(A_WIDTH) + plane(LANES) + [row, _resident((A_WIDTH, d)),
                                                  _resident((1, d)), _resident((1, d))],
        out_specs=row,
        out_shape=jax.ShapeDtypeStruct((b, s, d), F32),
        scratch_shapes=[pltpu.VMEM((tm, A_WIDTH), BF16),
                        pltpu.VMEM((A_GROUPS, A_HEADS, tm, A_HEAD_DIM), F32),
                        pltpu.VMEM((A_GROUPS, tm, LANES), F32)],
        compiler_params=_cparams(("parallel", "parallel")),
        name="merge_proj_ln",
    )(*os, *lses, x3d, w_bf16, ln_g, ln_b)


def _ffn_ln_kernel(x_ref, xp_ref, xn_ref, win_ref, cw_ref, cb_ref, wout_ref, g_ref, b_ref,
                   out_ref, act, *, tm):
    si = pl.program_id(1)
    n_s = pl.num_programs(1)
    x = x_ref[0]
    xb = x.astype(BF16)
    prev_ok = (si > 0).astype(F32)
    next_ok = (si < n_s - 1).astype(F32)
    xe = jnp.concatenate([xp_ref[0] * prev_ok, xn_ref[0] * next_ok], axis=0).astype(BF16)
    row = lax.broadcasted_iota(jnp.int32, (tm, 2 * FF_CHUNK), 0)
    for c in range(D_FF // FF_CHUNK):
        cols = slice(c * 2 * FF_CHUNK, (c + 1) * 2 * FF_CHUNK)
        w = win_ref[:, cols]
        h = _dot(xb, w)
        he = _dot(xe, w)
        up = jnp.where(row == 0, he[SUBLANES - 1:SUBLANES], pltpu.roll(h, 1, 0))
        dn = jnp.where(row == tm - 1, he[SUBLANES:SUBLANES + 1], pltpu.roll(h, tm - 1, 0))
        cw = cw_ref[:, cols]
        hc = up * cw[0:1] + h * cw[1:2] + dn * cw[2:3] + cb_ref[:, cols]
        a = hc[:, :FF_CHUNK]
        gate = hc[:, FF_CHUNK:]
        gelu = 0.5 * gate * (1.0 + lax.erf(gate * np.float32(math.sqrt(0.5))))
        act[:, c * FF_CHUNK:(c + 1) * FF_CHUNK] = (a * gelu).astype(BF16)
    y = ALPHA * x + _dot(act[...], wout_ref[...])
    out_ref[0] = _layer_norm(y, g_ref[...], b_ref[...])


def _chunk_interleave(w):
    lead = w.shape[:-1]
    w = w.reshape(lead + (2, D_FF // FF_CHUNK, FF_CHUNK))
    w = jnp.swapaxes(w, -3, -2)
    return w.reshape(lead + (2 * D_FF,))


def _ffn_ln(x3d, w_in, conv_w, conv_b, w_out, ln_g, ln_b, tm):
    b, s, d = x3d.shape
    nb8 = tm // SUBLANES
    n8 = s // SUBLANES
    win = _chunk_interleave(w_in).astype(BF16)
    cw = _chunk_interleave(conv_w)
    cb = _chunk_interleave(conv_b)[None]
    return pl.pallas_call(
        functools.partial(_ffn_ln_kernel, tm=tm),
        grid=(b, s // tm),
        in_specs=[pl.BlockSpec((1, tm, d), lambda bi, si: (bi, si, 0)),
                  pl.BlockSpec((1, SUBLANES, d), lambda bi, si: (bi, jnp.maximum(si * nb8 - 1, 0), 0)),
                  pl.BlockSpec((1, SUBLANES, d), lambda bi, si: (bi, jnp.minimum((si + 1) * nb8, n8 - 1), 0)),
                  _resident((d, 2 * D_FF)), _resident((3, 2 * D_FF)), _resident((1, 2 * D_FF)),
                  _resident((D_FF, d)), _resident((1, d)), _resident((1, d))],
        out_specs=pl.BlockSpec((1, tm, d), lambda bi, si: (bi, si, 0)),
        out_shape=jax.ShapeDtypeStruct((b, s, d), F32),
        scratch_shapes=[pltpu.VMEM((tm, D_FF), BF16)],
        compiler_params=_cparams(("parallel", "parallel")),
        name="ffn_ln",
    )(x3d, x3d, x3d, win, cw, cb, w_out.astype(BF16), ln_g, ln_b)


def _rope_tables(seq):
    inv = 1.0 / (ROPE_THETA ** (jnp.arange(0, QK_ROPE, 2, dtype=F32) / QK_ROPE))
    ang = jnp.arange(seq, dtype=F32)[:, None] * inv[None, :]
    cos, sin = jnp.cos(ang), jnp.sin(ang)
    half = QK_ROPE // 2
    ones = jnp.ones((seq, QK_NOPE), F32)
    z_nope = jnp.zeros((seq, QK_NOPE), F32)
    z_half = jnp.zeros((seq, half), F32)
    z_pad = jnp.zeros((seq, HEAD_PAD - QK_NOPE - QK_ROPE), F32)
    cmul = jnp.concatenate([ones, cos, cos, z_pad], axis=1)
    s_first = jnp.concatenate([z_nope, -sin, z_half, z_pad], axis=1)
    s_second = jnp.concatenate([z_nope, z_half, sin, z_pad], axis=1)
    return cmul, s_first, s_second


def _rope(t, cmul, s_first, s_second):
    half = QK_ROPE // 2
    return (t * cmul + pltpu.roll(t, HEAD_PAD - half, 1) * s_first
            + pltpu.roll(t, half, 1) * s_second)


def _rms_norm(x, g):
    ms = jnp.mean(x * x, axis=-1, keepdims=True)
    return x * lax.rsqrt(ms + RMS_EPS) * g


def _mla_proj_kernel(x_ref, wd_ref, gq_ref, gkv_ref, wuqt_ref, wuk_ref, wuvt_ref,
                     cm_ref, s1_ref, s2_ref, cmt_ref, s1t_ref, s2t_ref, qt_ref, k_ref, vt_ref):
    xb = x_ref[0].astype(BF16)
    c = _dot(xb, wd_ref[...])
    cqn = _rms_norm(c[:, :Q_LORA], gq_ref[...]).astype(BF16)
    ckvn = _rms_norm(c[:, Q_LORA:Q_LORA + KV_LORA], gkv_ref[...]).astype(BF16)
    cm, s1, s2 = cm_ref[...], s1_ref[...], s2_ref[...]
    kr = _rope(c[:, Q_LORA + KV_LORA:], cm, s1, s2)
    k = _dot(ckvn, wuk_ref[...])
    for h in range(B_HEADS):
        cols = slice(h * HEAD_PAD, (h + 1) * HEAD_PAD)
        k_ref[0, :, cols] = (k[:, cols] + kr).astype(BF16)
    qt = _dot_nt(wuqt_ref[...], cqn)
    cmt, s1t, s2t = cmt_ref[...], s1t_ref[...], s2t_ref[...]
    half = QK_ROPE // 2
    for h in range(B_HEADS):
        rows = slice(h * HEAD_PAD, (h + 1) * HEAD_PAD)
        t = qt[rows]
        rot = t * cmt + pltpu.roll(t, HEAD_PAD - half, 0) * s1t + pltpu.roll(t, half, 0) * s2t
        qt_ref[0, rows, :] = (rot * QK_SCALE_LOG2E).astype(BF16)
    vt = _dot_nt(wuvt_ref[...], ckvn).astype(BF16)
    tm = vt.shape[1]
    row = lax.broadcasted_iota(jnp.int32, (V_ROWS - V_HEAD, tm), 0)
    ones_row = jnp.where(row == 0, 1.0, 0.0).astype(BF16)
    for h in range(B_HEADS):
        vt_ref[0, h * V_ROWS:h * V_ROWS + V_HEAD, :] = vt[h * V_HEAD:(h + 1) * V_HEAD]
        vt_ref[0, h * V_ROWS + V_HEAD:(h + 1) * V_ROWS, :] = ones_row


def _mla_weights(w_dkv, w_uq, w_ukv):
    d = w_dkv.shape[0]
    pad_l = jnp.zeros((d, QK_NOPE), F32)
    pad_r = jnp.zeros((d, HEAD_PAD - QK_NOPE - QK_ROPE), F32)
    wd = jnp.concatenate([w_dkv[:, :Q_LORA + KV_LORA], pad_l, w_dkv[:, Q_LORA + KV_LORA:], pad_r], axis=1)
    wq = w_uq.reshape(Q_LORA, B_HEADS, QK_NOPE + QK_ROPE)
    wq = jnp.pad(wq, ((0, 0), (0, 0), (0, HEAD_PAD - QK_NOPE - QK_ROPE))).reshape(Q_LORA, B_HEADS * HEAD_PAD)
    wkv = w_ukv.reshape(KV_LORA, B_HEADS, QK_NOPE + V_HEAD)
    wk = jnp.pad(wkv[:, :, :QK_NOPE], ((0, 0), (0, 0), (0, HEAD_PAD - QK_NOPE))).reshape(KV_LORA, B_HEADS * HEAD_PAD)
    wvt = wkv[:, :, QK_NOPE:].reshape(KV_LORA, B_HEADS * V_HEAD).T
    return wd.astype(BF16), wq.T.astype(BF16), wk.astype(BF16), wvt.astype(BF16)


def _mla_proj(x3d, w_dkv, g_q, g_kv, w_uq, w_ukv, tm):
    b, s, d = x3d.shape
    wd, wqt, wk, wvt = _mla_weights(w_dkv, w_uq, w_ukv)
    tabs = _rope_tables(s)
    tabs_t = [t.T for t in tabs]
    hq = B_HEADS * HEAD_PAD
    hv = B_HEADS * V_ROWS
    tab = pl.BlockSpec((tm, HEAD_PAD), lambda bi, si: (si, 0))
    tab_t = pl.BlockSpec((HEAD_PAD, tm), lambda bi, si: (0, si))
    return pl.pallas_call(
        _mla_proj_kernel,
        grid=(b, s // tm),
        in_specs=[pl.BlockSpec((1, tm, d), lambda bi, si: (bi, si, 0)),
                  _resident(wd.shape), _resident((1, Q_LORA)), _resident((1, KV_LORA)),
                  _resident(wqt.shape), _resident(wk.shape), _resident(wvt.shape),
                  tab, tab, tab, tab_t, tab_t, tab_t],
        out_specs=[pl.BlockSpec((1, hq, tm), lambda bi, si: (bi, 0, si)),
                   pl.BlockSpec((1, tm, hq), lambda bi, si: (bi, si, 0)),
                   pl.BlockSpec((1, hv, tm), lambda bi, si: (bi, 0, si))],
        out_shape=[jax.ShapeDtypeStruct((b, hq, s), BF16),
                   jax.ShapeDtypeStruct((b, s, hq), BF16),
                   jax.ShapeDtypeStruct((b, hv, s), BF16)],
        compiler_params=_cparams(("parallel", "parallel")),
        name="mla_proj",
    )(x3d, wd, g_q[None], g_kv[None], wqt, wk, wvt, *tabs, *tabs_t)


def _mla_attn_kernel(qt_ref, k_ref, vt_ref, o_ref, s_scr, p_scr, cmax_scr, m_scr, a_scr, acc_scr, *, tq, tk,
                     unroll):
    seq = k_ref.shape[1]
    nk = seq // tk
    n_q = qt_ref.shape[2] // tq

    def run_step(t, t_static, do_qk=True, do_sm=True, do_pv=True):
        s_qk, s_sm, s_pv = t_static % N_SLOTS, (t_static - 1) % N_SLOTS, (t_static - 2) % N_SLOTS
        if do_pv:
            v0 = pl.multiple_of((t - 2) * tk, tk)
            vt = vt_ref[0, :, pl.ds(v0, tk)]
        if do_qk:
            k0 = pl.multiple_of(t * tk, tk)
            k = k_ref[0, pl.ds(k0, tk), :]
        pvs = []
        if do_pv:
            pvs = [_dot(vt, p_scr[s_pv, j]) for j in range(n_q)]
        if do_qk:
            for j in range(n_q):
                sc = _dot(k, qt_ref[0, :, j * tq:(j + 1) * tq])
                s_scr[s_qk, j] = sc
                cmax_scr[s_qk, j] = jnp.max(sc, axis=0, keepdims=True)
        for j in range(n_q):
            if do_pv:
                acc_scr[j] = a_scr[j] * acc_scr[j] + pvs[j]
            if do_sm:
                m = m_scr[j]
                m_new = jnp.maximum(m, cmax_scr[s_sm, j])
                a_scr[j] = jnp.exp2(m - m_new)
                m_scr[j] = m_new
                for r0 in range(0, tk, SM_ROWS):
                    rows = slice(r0, r0 + SM_ROWS)
                    p_scr[s_sm, j, rows, :] = jnp.exp2(s_scr[s_sm, j, rows, :] - m_new).astype(BF16)

    m_scr[...] = jnp.full(m_scr.shape, NEG_INF, F32)
    a_scr[...] = jnp.zeros(a_scr.shape, F32)
    acc_scr[...] = jnp.zeros(acc_scr.shape, F32)
    run_step(0, 0, do_sm=False, do_pv=False)
    run_step(1, 1, do_pv=False)

    def steps(u, carry):
        t = unroll * u + 2
        for i in range(unroll):
            run_step(t + i, 2 + i)
        return carry

    lax.fori_loop(0, (nk - 2) // unroll, steps, 0)
    run_step(nk, nk, do_qk=False)
    run_step(nk + 1, nk + 1, do_qk=False, do_sm=False)
    for j in range(n_q):
        acc = acc_scr[j]
        o_ref[0, :, j * tq:(j + 1) * tq] = (acc[:V_HEAD] / acc[V_HEAD:V_HEAD + 1]).astype(BF16)


def _mla_attn(qt, k, vt, tq_block, tq, tk):
    b, s, _ = k.shape
    n_q = tq_block // tq
    nk = s // tk
    unroll = max(u for u in range(N_SLOTS, MAX_UNROLL + 1, N_SLOTS) if (nk - 2) % u == 0)
    assert nk >= 2 + unroll
    return pl.pallas_call(
        functools.partial(_mla_attn_kernel, tq=tq, tk=tk, unroll=unroll),
        grid=(b, B_HEADS, s // tq_block),
        in_specs=[pl.BlockSpec((1, HEAD_PAD, tq_block), lambda bi, h, qi: (bi, h, qi)),
                  pl.BlockSpec((1, s, HEAD_PAD), lambda bi, h, qi: (bi, 0, h)),
                  pl.BlockSpec((1, V_ROWS, s), lambda bi, h, qi: (bi, h, 0))],
        out_specs=pl.BlockSpec((1, V_HEAD, tq_block), lambda bi, h, qi: (bi, h, qi)),
        out_shape=jax.ShapeDtypeStruct((b, B_HEADS * V_HEAD, s), BF16),
        scratch_shapes=[pltpu.VMEM((N_SLOTS, n_q, tk, tq), F32), pltpu.VMEM((N_SLOTS, n_q, tk, tq), BF16),
                        pltpu.VMEM((N_SLOTS, n_q, 1, tq), F32),
                        pltpu.VMEM((n_q, 1, tq), F32), pltpu.VMEM((n_q, 1, tq), F32),
                        pltpu.VMEM((n_q, V_ROWS, tq), F32)],
        compiler_params=_cparams(("parallel", "parallel", "arbitrary")),
        name="mla_attn",
    )(qt, k, vt)


def _proj_t_ln_kernel(ot_ref, x_ref, w_ref, g_ref, b_ref, out_ref):
    y = ALPHA * x_ref[0] + _dot_tn(ot_ref[0], w_ref[...])
    out_ref[0] = _layer_norm(y, g_ref[...], b_ref[...])


def _proj_t_ln(ot, x3d, w_bf16, ln_g, ln_b, tm):
    b, s, d = x3d.shape
    k = ot.shape[1]
    return pl.pallas_call(
        _proj_t_ln_kernel,
        grid=(b, s // tm),
        in_specs=[pl.BlockSpec((1, k, tm), lambda bi, si: (bi, 0, si)),
                  pl.BlockSpec((1, tm, d), lambda bi, si: (bi, si, 0)),
                  _resident((k, d)), _resident((1, d)), _resident((1, d))],
        out_specs=pl.BlockSpec((1, tm, d), lambda bi, si: (bi, si, 0)),
        out_shape=jax.ShapeDtypeStruct((b, s, d), F32),
        compiler_params=_cparams(("parallel", "parallel")),
        name="proj_t_ln",
    )(ot, x3d, w_bf16, ln_g, ln_b)


def _encoder(x, bias, w_qkv_a, w_o_a, w_dkv_b, g_q_b, g_kv_b, w_uq_b, w_ukv_b, w_o_b,
             ffn_w_in, ffn_conv_w, ffn_conv_b, ffn_w_out, ln_g, ln_b):
    b, s, d = x.shape
    tm = 512
    for i in range(DEPTH):
        j = i // N_MIXERS
        g1, b1 = ln_g[i, 0][None], ln_b[i, 0][None]
        g2, b2 = ln_g[i, 1][None], ln_b[i, 1][None]
        if i % N_MIXERS == 0:
            qkv = _qkv_proj(x, w_qkv_a[j].astype(BF16), tm=256)
            outs = [_dilated_attn(qkv[g], bias, g) for g in range(A_GROUPS)]
            x = _merge_proj_ln([o for o, _ in outs], [l for _, l in outs], x,
                               w_o_a[j].astype(BF16), g1, b1, tm)
        else:
            q, k, vt = _mla_proj(x, w_dkv_b[j], g_q_b[j], g_kv_b[j], w_uq_b[j], w_ukv_b[j], tm)
            ot = _mla_attn(q, k, vt, tq_block=min(1024, s), tq=512, tk=256)
            x = _proj_t_ln(ot, x, w_o_b[j].astype(BF16), g1, b1, tm)
        x = _ffn_ln(x, ffn_w_in[i], ffn_conv_w[i], ffn_conv_b[i], ffn_w_out[i], g2, b2, tm)
    return x


def kernel(x_prompt, x_sample, rel_bias, w_qkv_a, w_o_a, w_dkv_b, g_q_b, g_kv_b, w_uq_b, w_ukv_b, w_o_b,
           ffn_w_in, ffn_conv_w, ffn_conv_b, ffn_w_out, ln_g, ln_b):
    bias = _bias_tiles(rel_bias)
    args = (bias, w_qkv_a, w_o_a, w_dkv_b, g_q_b, g_kv_b, w_uq_b, w_ukv_b, w_o_b,
            ffn_w_in, ffn_conv_w, ffn_conv_b, ffn_w_out, ln_g, ln_b)
    return (_encoder(x_prompt, *args), _encoder(x_sample, *args))
```

```python
import functools
import math

import jax
import jax.numpy as jnp
import numpy as np
from jax import lax
from jax.experimental import pallas as pl
from jax.experimental.pallas import tpu as pltpu

F32 = jnp.float32
BF16 = jnp.bfloat16

D_MODEL = 1024
DEPTH = 2
N_MIXERS = 2

A_WINDOWS = (128, 512, 2048)
A_DILATIONS = (1, 4, 16)
A_GROUPS = 3
A_HEADS = 8
A_HEAD_DIM = 128
A_WIDTH = A_HEADS * A_HEAD_DIM
A_HALF = 64
NUM_BUCKETS = 32
MAX_DISTANCE = max(A_WINDOWS) // 2

B_HEADS = 16
Q_LORA = 384
KV_LORA = 256
QK_NOPE = 64
QK_ROPE = 32
V_HEAD = 64
ROPE_THETA = 10000.0
HEAD_PAD = 128
V_ROWS = V_HEAD + 16
SM_ROWS = 64
N_SLOTS = 2
MAX_UNROLL = 8

D_FF = 2816
FF_CHUNK = 256

ALPHA = (2.0 * DEPTH) ** 0.25
LN_EPS = 1e-5
RMS_EPS = 1e-6
NEG_INF = -1e30

LANES = 128
SUBLANES = 8
VMEM_LIMIT = 56 * 1024 * 1024

QK_SCALE_LOG2E = np.float32((QK_NOPE + QK_ROPE) ** -0.5 * math.log2(math.e))


def _cparams(sem):
    return pltpu.CompilerParams(dimension_semantics=sem, vmem_limit_bytes=VMEM_LIMIT)


def _resident(shape):
    nd = len(shape)
    return pl.BlockSpec(shape, lambda *_: (0,) * nd, pipeline_mode=pl.Buffered(1))


def _layer_norm(y, g, b):
    mu = jnp.mean(y, axis=-1, keepdims=True)
    yc = y - mu
    var = jnp.mean(yc * yc, axis=-1, keepdims=True)
    return yc * lax.rsqrt(var + LN_EPS) * g + b


def _dot(a, b):
    return jnp.dot(a, b, preferred_element_type=F32)


def _dot_nt(a, b):
    return lax.dot_general(a, b, (((1,), (1,)), ((), ())), preferred_element_type=F32)


def _dot_tn(a, b):
    return lax.dot_general(a, b, (((0,), (0,)), ((), ())), preferred_element_type=F32)


def _qkv_proj_kernel(x_ref, w_ref, *refs, tm):
    outs, res_scr = refs[:A_GROUPS], refs[A_GROUPS]
    xb = x_ref[0].astype(BF16)
    slot = 0
    for g, dil in enumerate(A_DILATIONS):
        for c in range(3):
            n0 = (g * 3 + c) * A_WIDTH
            cols = slice(c * A_WIDTH, (c + 1) * A_WIDTH)
            res = _dot(xb, w_ref[:, n0:n0 + A_WIDTH])
            if dil == 1:
                outs[g][0, 0, :, cols] = res.astype(BF16)
                continue
            for h in range(A_HEADS):
                res_scr[slot, h] = res[:, h * LANES:(h + 1) * LANES]
            for r in range(dil):
                for h in range(A_HEADS):
                    lanes = slice(c * A_WIDTH + h * LANES, c * A_WIDTH + (h + 1) * LANES)
                    outs[g][0, r, :, lanes] = res_scr[slot, h, pl.ds(r, tm // dil, stride=dil), :].astype(BF16)
            slot = 1 - slot


def _qkv_proj(x3d, w_bf16, tm):
    b, s, d = x3d.shape
    width = 3 * A_WIDTH
    return pl.pallas_call(
        functools.partial(_qkv_proj_kernel, tm=tm),
        grid=(b, s // tm),
        in_specs=[pl.BlockSpec((1, tm, d), lambda bi, i: (bi, i, 0)), _resident(w_bf16.shape)],
        out_specs=[pl.BlockSpec((1, dil, tm // dil, width), lambda bi, i: (bi, 0, i, 0))
                   for dil in A_DILATIONS],
        out_shape=[jax.ShapeDtypeStruct((b, dil, s // dil, width), BF16) for dil in A_DILATIONS],
        scratch_shapes=[pltpu.VMEM((2, A_HEADS, tm, LANES), F32)],
        compiler_params=_cparams(("parallel", "parallel")),
        name="qkv_proj",
    )(x3d, w_bf16)


def _t5_bucket(rel):
    nb = NUM_BUCKETS // 2
    max_exact = nb // 2
    ret = jnp.where(rel > 0, nb, 0)
    n = jnp.abs(rel)
    large = max_exact + (jnp.log(jnp.maximum(n, 1).astype(F32) / max_exact)
                         / math.log(MAX_DISTANCE / max_exact) * (nb - max_exact)).astype(jnp.int32)
    large = jnp.minimum(large, nb - 1)
    return ret + jnp.where(n < max_exact, n, large)


def _band_bucket_index():
    qi = jnp.arange(2 * A_HALF)[:, None]
    kj = jnp.arange(4 * A_HALF)[None, :]
    rel = kj - A_HALF - qi
    tiles = [jnp.where(jnp.abs(rel) <= A_HALF, _t5_bucket(rel * dil), -1) for dil in A_DILATIONS]
    return jnp.stack(tiles, axis=0).astype(jnp.int32)


def _bias_tile_kernel(tab_ref, idx_ref, o_ref):
    col = pl.program_id(0) * A_HEADS + pl.program_id(1)
    idx = idx_ref[0]
    acc = jnp.full(idx.shape, NEG_INF, F32)
    for bucket in range(NUM_BUCKETS):
        acc = jnp.where(idx == bucket, tab_ref[bucket, col], acc)
    o_ref[0, 0] = acc


def _bias_tiles(rel_bias):
    idx = _band_bucket_index()
    tq, tk = idx.shape[1:]
    return pl.pallas_call(
        _bias_tile_kernel,
        grid=(A_GROUPS, A_HEADS),
        in_specs=[pl.BlockSpec(memory_space=pltpu.SMEM),
                  pl.BlockSpec((1, tq, tk), lambda g, h: (g, 0, 0))],
        out_specs=pl.BlockSpec((1, 1, tq, tk), lambda g, h: (g, h, 0, 0)),
        out_shape=jax.ShapeDtypeStruct((A_GROUPS, A_HEADS, tq, tk), F32),
        compiler_params=_cparams(("arbitrary", "arbitrary")),
        name="bias_tiles",
    )(rel_bias, idx)


def _dilated_attn_kernel(q_ref, kl_ref, km_ref, kr_ref, vl_ref, vm_ref, vr_ref, bias_ref,
                         o_ref, lse_ref, kbuf, vbuf, *, tq, sub_len):
    li = pl.program_id(2)
    kbuf[0:A_HALF] = kl_ref[0, 0]
    kbuf[A_HALF:A_HALF + tq] = km_ref[0, 0]
    kbuf[A_HALF + tq:] = kr_ref[0, 0]
    vbuf[0:A_HALF] = vl_ref[0, 0]
    vbuf[A_HALF:A_HALF + tq] = vm_ref[0, 0]
    vbuf[A_HALF + tq:] = vr_ref[0, 0]

    scale = A_HEAD_DIM ** -0.5
    qt, kt = 2 * A_HALF, 4 * A_HALF
    lane = lax.broadcasted_iota(jnp.int32, (qt, LANES), 1)
    for t in range(tq // qt):
        pos = li * tq + t * qt - A_HALF + lax.broadcasted_iota(jnp.int32, (1, kt), 1)
        edge = jnp.where((pos >= 0) & (pos < sub_len), 0.0, NEG_INF).astype(F32)
        lse_tile = jnp.zeros((qt, LANES), F32)
        for h in range(A_HEADS):
            cols = slice(h * A_HEAD_DIM, (h + 1) * A_HEAD_DIM)
            q = q_ref[0, 0, t * qt:(t + 1) * qt, cols]
            k = kbuf[t * qt:t * qt + kt, cols]
            v = vbuf[t * qt:t * qt + kt, cols]
            s = _dot_nt(q, k) * scale + bias_ref[0, h] + edge
            m = jnp.max(s, axis=-1, keepdims=True)
            p = jnp.exp(s - m)
            den = jnp.sum(p, axis=-1, keepdims=True)
            o = _dot(p.astype(BF16), v) / den
            o_ref[0, 0, t * qt:(t + 1) * qt, cols] = o.astype(BF16)
            lse_tile = jnp.where(lane == h, m + jnp.log(den), lse_tile)
        lse_ref[0, 0, t * qt:(t + 1) * qt, :] = lse_tile


def _dilated_attn(qkv_g, bias, g):
    batch, dil, sub_len, _ = qkv_g.shape
    tq = min(256, sub_len)
    n_t = sub_len // tq
    hb = tq // A_HALF
    n_hb = sub_len // A_HALF

    main = lambda which: pl.BlockSpec((1, 1, tq, A_WIDTH), lambda b, r, i: (b, r, i, which))
    left = lambda which: pl.BlockSpec(
        (1, 1, A_HALF, A_WIDTH), lambda b, r, i: (b, r, jnp.maximum(i * hb - 1, 0), which))
    right = lambda which: pl.BlockSpec(
        (1, 1, A_HALF, A_WIDTH), lambda b, r, i: (b, r, jnp.minimum((i + 1) * hb, n_hb - 1), which))

    return pl.pallas_call(
        functools.partial(_dilated_attn_kernel, tq=tq, sub_len=sub_len),
        grid=(batch, dil, n_t),
        in_specs=[main(0), left(1), main(1), right(1), left(2), main(2), right(2),
                  pl.BlockSpec((1, A_HEADS, 2 * A_HALF, 4 * A_HALF), lambda b, r, i: (g, 0, 0, 0))],
        out_specs=[pl.BlockSpec((1, 1, tq, A_WIDTH), lambda b, r, i: (b, r, i, 0)),
                   pl.BlockSpec((1, 1, tq, LANES), lambda b, r, i: (b, r, i, 0))],
        out_shape=[jax.ShapeDtypeStruct((batch, dil, sub_len, A_WIDTH), BF16),
                   jax.ShapeDtypeStruct((batch, dil, sub_len, LANES), F32)],
        scratch_shapes=[pltpu.VMEM((tq + 2 * A_HALF, A_WIDTH), BF16),
                        pltpu.VMEM((tq + 2 * A_HALF, A_WIDTH), BF16)],
        compiler_params=_cparams(("parallel", "parallel", "parallel")),
        name=f"dilated_attn_g{g}",
    )(qkv_g, qkv_g, qkv_g, qkv_g, qkv_g, qkv_g, qkv_g, bias)


def _merge_proj_ln_kernel(o0_ref, o1_ref, o2_ref, l0_ref, l1_ref, l2_ref, x_ref, w_ref, g_ref, b_ref,
                          out_ref, merged, o_scr, l_scr, *, tm):
    for g, (o_ref, l_ref, dil) in enumerate(zip((o0_ref, o1_ref, o2_ref), (l0_ref, l1_ref, l2_ref),
                                                A_DILATIONS)):
        for r in range(dil):
            rows = pl.ds(r, tm // dil, stride=dil) if dil > 1 else slice(None)
            l_scr[g, rows, :] = l_ref[0, r]
            for h in range(A_HEADS):
                o_scr[g, h, rows, :] = o_ref[0, r, :, h * A_HEAD_DIM:(h + 1) * A_HEAD_DIM].astype(F32)
    l0, l1, l2 = l_scr[0], l_scr[1], l_scr[2]
    m = jnp.maximum(jnp.maximum(l0, l1), l2)
    e0, e1, e2 = jnp.exp(l0 - m), jnp.exp(l1 - m), jnp.exp(l2 - m)
    den = e0 + e1 + e2
    w0, w1, w2 = e0 / den, e1 / den, e2 / den
    for h in range(A_HEADS):
        cols = slice(h * A_HEAD_DIM, (h + 1) * A_HEAD_DIM)
        mix = (w0[:, h:h + 1] * o_scr[0, h] + w1[:, h:h + 1] * o_scr[1, h]
               + w2[:, h:h + 1] * o_scr[2, h])
        merged[:, cols] = mix.astype(BF16)
    y = ALPHA * x_ref[0] + _dot(merged[...], w_ref[...])
    out_ref[0] = _layer_norm(y, g_ref[...], b_ref[...])


def _merge_proj_ln(os, lses, x3d, w_bf16, ln_g, ln_b, tm):
    b, s, d = x3d.shape
    plane = lambda w: [pl.BlockSpec((1, dil, tm // dil, w), lambda bi, i: (bi, 0, i, 0))
                       for dil in A_DILATIONS]
    row = pl.BlockSpec((1, tm, d), lambda bi, i: (bi, i, 0))
    return pl.pallas_call(
        functools.partial(_merge_proj_ln_kernel, tm=tm),
        grid=(b, s // tm),
        in_specs=plane(A_WIDTH) + plane(LANES) + [row, _resident((A_WIDTH, d)),
                                                  _resident((1, d)), _resident((1, d))],
        out_specs=row,
        out_shape=jax.ShapeDtypeStruct((b, s, d), F32),
        scratch_shapes=[pltpu.VMEM((tm, A_WIDTH), BF16),
                        pltpu.VMEM((A_GROUPS, A_HEADS, tm, A_HEAD_DIM), F32),
                        pltpu.VMEM((A_GROUPS, tm, LANES), F32)],
        compiler_params=_cparams(("parallel", "parallel")),
        name="merge_proj_ln",
    )(*os, *lses, x3d, w_bf16, ln_g, ln_b)


def _ffn_ln_kernel(x_ref, xp_ref, xn_ref, win_ref, cw_ref, cb_ref, wout_ref, g_ref, b_ref,
                   out_ref, act, *, tm):
    si = pl.program_id(1)
    n_s = pl.num_programs(1)
    x = x_ref[0]
    xb = x.astype(BF16)
    prev_ok = (si > 0).astype(F32)
    next_ok = (si < n_s - 1).astype(F32)
    xe = jnp.concatenate([xp_ref[0] * prev_ok, xn_ref[0] * next_ok], axis=0).astype(BF16)
    row = lax.broadcasted_iota(jnp.int32, (tm, 2 * FF_CHUNK), 0)
    for c in range(D_FF // FF_CHUNK):
        cols = slice(c * 2 * FF_CHUNK, (c + 1) * 2 * FF_CHUNK)
        w = win_ref[:, cols]
        h = _dot(xb, w)
        he = _dot(xe, w)
        up = jnp.where(row == 0, he[SUBLANES - 1:SUBLANES], pltpu.roll(h, 1, 0))
        dn = jnp.where(row == tm - 1, he[SUBLANES:SUBLANES + 1], pltpu.roll(h, tm - 1, 0))
        cw = cw_ref[:, cols]
        hc = up * cw[0:1] + h * cw[1:2] + dn * cw[2:3] + cb_ref[:, cols]
        a = hc[:, :FF_CHUNK]
        gate = hc[:, FF_CHUNK:]
        gelu = 0.5 * gate * (1.0 + lax.erf(gate * np.float32(math.sqrt(0.5))))
        act[:, c * FF_CHUNK:(c + 1) * FF_CHUNK] = (a * gelu).astype(BF16)
    y = ALPHA * x + _dot(act[...], wout_ref[...])
    out_ref[0] = _layer_norm(y, g_ref[...], b_ref[...])


def _chunk_interleave(w):
    lead = w.shape[:-1]
    w = w.reshape(lead + (2, D_FF // FF_CHUNK, FF_CHUNK))
    w = jnp.swapaxes(w, -3, -2)
    return w.reshape(lead + (2 * D_FF,))


def _ffn_ln(x3d, w_in, conv_w, conv_b, w_out, ln_g, ln_b, tm):
    b, s, d = x3d.shape
    nb8 = tm // SUBLANES
    n8 = s // SUBLANES
    win = _chunk_interleave(w_in).astype(BF16)
    cw = _chunk_interleave(conv_w)
    cb = _chunk_interleave(conv_b)[None]
    return pl.pallas_call(
        functools.partial(_ffn_ln_kernel, tm=tm),
        grid=(b, s // tm),
        in_specs=[pl.BlockSpec((1, tm, d), lambda bi, si: (bi, si, 0)),
                  pl.BlockSpec((1, SUBLANES, d), lambda bi, si: (bi, jnp.maximum(si * nb8 - 1, 0), 0)),
                  pl.BlockSpec((1, SUBLANES, d), lambda bi, si: (bi, jnp.minimum((si + 1) * nb8, n8 - 1), 0)),
                  _resident((d, 2 * D_FF)), _resident((3, 2 * D_FF)), _resident((1, 2 * D_FF)),
                  _resident((D_FF, d)), _resident((1, d)), _resident((1, d))],
        out_specs=pl.BlockSpec((1, tm, d), lambda bi, si: (bi, si, 0)),
        out_shape=jax.ShapeDtypeStruct((b, s, d), F32),
        scratch_shapes=[pltpu.VMEM((tm, D_FF), BF16)],
        compiler_params=_cparams(("parallel", "parallel")),
        name="ffn_ln",
    )(x3d, x3d, x3d, win, cw, cb, w_out.astype(BF16), ln_g, ln_b)


def _rope_tables(seq):
    inv = 1.0 / (ROPE_THETA ** (jnp.arange(0, QK_ROPE, 2, dtype=F32) / QK_ROPE))
    ang = jnp.arange(seq, dtype=F32)[:, None] * inv[None, :]
    cos, sin = jnp.cos(ang), jnp.sin(ang)
    half = QK_ROPE // 2
    ones = jnp.ones((seq, QK_NOPE), F32)
    z_nope = jnp.zeros((seq, QK_NOPE), F32)
    z_half = jnp.zeros((seq, half), F32)
    z_pad = jnp.zeros((seq, HEAD_PAD - QK_NOPE - QK_ROPE), F32)
    cmul = jnp.concatenate([ones, cos, cos, z_pad], axis=1)
    s_first = jnp.concatenate([z_nope, -sin, z_half, z_pad], axis=1)
    s_second = jnp.concatenate([z_nope, z_half, sin, z_pad], axis=1)
    return cmul, s_first, s_second


def _rope(t, cmul, s_first, s_second):
    half = QK_ROPE // 2
    return (t * cmul + pltpu.roll(t, HEAD_PAD - half, 1) * s_first
            + pltpu.roll(t, half, 1) * s_second)


def _rms_norm(x, g):
    ms = jnp.mean(x * x, axis=-1, keepdims=True)
    return x * lax.rsqrt(ms + RMS_EPS) * g


def _mla_proj_kernel(x_ref, wd_ref, gq_ref, gkv_ref, wuqt_ref, wuk_ref, wuvt_ref,
                     cm_ref, s1_ref, s2_ref, cmt_ref, s1t_ref, s2t_ref, qt_ref, k_ref, vt_ref):
    xb = x_ref[0].astype(BF16)
    c = _dot(xb, wd_ref[...])
    cqn = _rms_norm(c[:, :Q_LORA], gq_ref[...]).astype(BF16)
    ckvn = _rms_norm(c[:, Q_LORA:Q_LORA + KV_LORA], gkv_ref[...]).astype(BF16)
    cm, s1, s2 = cm_ref[...], s1_ref[...], s2_ref[...]
    kr = _rope(c[:, Q_LORA + KV_LORA:], cm, s1, s2)
    k = _dot(ckvn, wuk_ref[...])
    for h in range(B_HEADS):
        cols = slice(h * HEAD_PAD, (h + 1) * HEAD_PAD)
        k_ref[0, :, cols] = (k[:, cols] + kr).astype(BF16)
    qt = _dot_nt(wuqt_ref[...], cqn)
    cmt, s1t, s2t = cmt_ref[...], s1t_ref[...], s2t_ref[...]
    half = QK_ROPE // 2
    for h in range(B_HEADS):
        rows = slice(h * HEAD_PAD, (h + 1) * HEAD_PAD)
        t = qt[rows]
        rot = t * cmt + pltpu.roll(t, HEAD_PAD - half, 0) * s1t + pltpu.roll(t, half, 0) * s2t
        qt_ref[0, rows, :] = (rot * QK_SCALE_LOG2E).astype(BF16)
    vt = _dot_nt(wuvt_ref[...], ckvn).astype(BF16)
    tm = vt.shape[1]
    row = lax.broadcasted_iota(jnp.int32, (V_ROWS - V_HEAD, tm), 0)
    ones_row = jnp.where(row == 0, 1.0, 0.0).astype(BF16)
    for h in range(B_HEADS):
        vt_ref[0, h * V_ROWS:h * V_ROWS + V_HEAD, :] = vt[h * V_HEAD:(h + 1) * V_HEAD]
        vt_ref[0, h * V_ROWS + V_HEAD:(h + 1) * V_ROWS, :] = ones_row


def _mla_weights(w_dkv, w_uq, w_ukv):
    d = w_dkv.shape[0]
    pad_l = jnp.zeros((d, QK_NOPE), F32)
    pad_r = jnp.zeros((d, HEAD_PAD - QK_NOPE - QK_ROPE), F32)
    wd = jnp.concatenate([w_dkv[:, :Q_LORA + KV_LORA], pad_l, w_dkv[:, Q_LORA + KV_LORA:], pad_r], axis=1)
    wq = w_uq.reshape(Q_LORA, B_HEADS, QK_NOPE + QK_ROPE)
    wq = jnp.pad(wq, ((0, 0), (0, 0), (0, HEAD_PAD - QK_NOPE - QK_ROPE))).reshape(Q_LORA, B_HEADS * HEAD_PAD)
    wkv = w_ukv.reshape(KV_LORA, B_HEADS, QK_NOPE + V_HEAD)
    wk = jnp.pad(wkv[:, :, :QK_NOPE], ((0, 0), (0, 0), (0, HEAD_PAD - QK_NOPE))).reshape(KV_LORA, B_HEADS * HEAD_PAD)
    wvt = wkv[:, :, QK_NOPE:].reshape(KV_LORA, B_HEADS * V_HEAD).T
    return wd.astype(BF16), wq.T.astype(BF16), wk.astype(BF16), wvt.astype(BF16)


def _mla_proj(x3d, w_dkv, g_q, g_kv, w_uq, w_ukv, tm):
    b, s, d = x3d.shape
    wd, wqt, wk, wvt = _mla_weights(w_dkv, w_uq, w_ukv)
    tabs = _rope_tables(s)
    tabs_t = [t.T for t in tabs]
    hq = B_HEADS * HEAD_PAD
    hv = B_HEADS * V_ROWS
    tab = pl.BlockSpec((tm, HEAD_PAD), lambda bi, si: (si, 0))
    tab_t = pl.BlockSpec((HEAD_PAD, tm), lambda bi, si: (0, si))
    return pl.pallas_call(
        _mla_proj_kernel,
        grid=(b, s // tm),
        in_specs=[pl.BlockSpec((1, tm, d), lambda bi, si: (bi, si, 0)),
                  _resident(wd.shape), _resident((1, Q_LORA)), _resident((1, KV_LORA)),
                  _resident(wqt.shape), _resident(wk.shape), _resident(wvt.shape),
                  tab, tab, tab, tab_t, tab_t, tab_t],
        out_specs=[pl.BlockSpec((1, hq, tm), lambda bi, si: (bi, 0, si)),
                   pl.BlockSpec((1, tm, hq), lambda bi, si: (bi, si, 0)),
                   pl.BlockSpec((1, hv, tm), lambda bi, si: (bi, 0, si))],
        out_shape=[jax.ShapeDtypeStruct((b, hq, s), BF16),
                   jax.ShapeDtypeStruct((b, s, hq), BF16),
                   jax.ShapeDtypeStruct((b, hv, s), BF16)],
        compiler_params=_cparams(("parallel", "parallel")),
        name="mla_proj",
    )(x3d, wd, g_q[None], g_kv[None], wqt, wk, wvt, *tabs, *tabs_t)


def _mla_attn_kernel(qt_ref, k_ref, vt_ref, o_ref, s_scr, p_scr, cmax_scr, m_all, a_all, acc_all, *, tq, tk,
                     n_q, unroll):
    seq = k_ref.shape[1]
    nk = seq // tk
    blk_q = n_q * tq
    n_blk = seq // blk_q
    total = n_blk * nk
    trips_per_blk = nk // unroll

    def finalize(blk):
        par = blk % 2
        for j in range(n_q):
            acc = acc_all[par, j]
            q0 = pl.multiple_of(blk * blk_q + j * tq, tq)
            o_ref[0, :, pl.ds(q0, tq)] = (acc[:V_HEAD] / acc[V_HEAD:V_HEAD + 1]).astype(BF16)

    def run_step(t, t_static, do_qk=True, do_sm=True):
        s_qk, s_sm, s_pv = t_static % N_SLOTS, (t_static - 1) % N_SLOTS, (t_static - 2) % N_SLOTS
        n_sm, n_pv = t - 1 + nk, t - 2 + nk
        par_sm, first_sm = (n_sm // nk + 1) % 2, (n_sm % nk) == 0
        par_pv = (n_pv // nk + 1) % 2
        v0 = pl.multiple_of((n_pv % nk) * tk, tk)
        vt = vt_ref[0, :, pl.ds(v0, tk)]
        pvs = [_dot(vt, p_scr[s_pv, j]) for j in range(n_q)]
        if do_qk:
            k0 = pl.multiple_of((t % nk) * tk, tk)
            k = k_ref[0, pl.ds(k0, tk), :]
            for j in range(n_q):
                q0 = pl.multiple_of((t // nk) * blk_q + j * tq, tq)
                sc = _dot(k, qt_ref[0, :, pl.ds(q0, tq)])
                s_scr[s_qk, j] = sc
                cmax_scr[s_qk, j] = jnp.max(sc, axis=0, keepdims=True)
        for j in range(n_q):
            acc_all[par_pv, j] = a_all[par_pv, j] * acc_all[par_pv, j] + pvs[j]
            if do_sm:
                m = jnp.where(first_sm, NEG_INF, m_all[par_sm, j])
                m_new = jnp.maximum(m, cmax_scr[s_sm, j])
                a_all[par_sm, j] = jnp.exp2(m - m_new)
                m_all[par_sm, j] = m_new
                for r0 in range(0, tk, SM_ROWS):
                    rows = slice(r0, r0 + SM_ROWS)
                    p_scr[s_sm, j, rows, :] = jnp.exp2(s_scr[s_sm, j, rows, :] - m_new).astype(BF16)

    s_scr[...] = jnp.zeros(s_scr.shape, F32)
    p_scr[...] = jnp.zeros(p_scr.shape, BF16)
    cmax_scr[...] = jnp.zeros(cmax_scr.shape, F32)
    m_all[...] = jnp.full(m_all.shape, NEG_INF, F32)
    a_all[...] = jnp.zeros(a_all.shape, F32)
    acc_all[...] = jnp.zeros(acc_all.shape, F32)

    def trip(u, carry):
        @pl.when((u % trips_per_blk == 1) & (u > trips_per_blk))
        def _():
            finalize(u // trips_per_blk - 1)

        for i in range(unroll):
            run_step(unroll * u + i, i)
        return carry

    lax.fori_loop(0, total // unroll, trip, 0)
    run_step(total, total, do_qk=False)
    run_step(total + 1, total + 1, do_qk=False, do_sm=False)
    finalize(n_blk - 1)


def _mla_attn(qt, k, vt, n_q, tq, tk):
    b, s, _ = k.shape
    nk = s // tk
    unroll = max(u for u in range(N_SLOTS, MAX_UNROLL + 1, N_SLOTS) if nk % u == 0 and nk // u >= 2)
    assert s % (n_q * tq) == 0 and N_SLOTS == 2
    return pl.pallas_call(
        functools.partial(_mla_attn_kernel, tq=tq, tk=tk, n_q=n_q, unroll=unroll),
        grid=(b, B_HEADS),
        in_specs=[pl.BlockSpec((1, HEAD_PAD, s), lambda bi, h: (bi, h, 0)),
                  pl.BlockSpec((1, s, HEAD_PAD), lambda bi, h: (bi, 0, h)),
                  pl.BlockSpec((1, V_ROWS, s), lambda bi, h: (bi, h, 0))],
        out_specs=pl.BlockSpec((1, V_HEAD, s), lambda bi, h: (bi, h, 0)),
        out_shape=jax.ShapeDtypeStruct((b, B_HEADS * V_HEAD, s), BF16),
        scratch_shapes=[pltpu.VMEM((N_SLOTS, n_q, tk, tq), F32), pltpu.VMEM((N_SLOTS, n_q, tk, tq), BF16),
                        pltpu.VMEM((N_SLOTS, n_q, 1, tq), F32),
                        pltpu.VMEM((2, n_q, 1, tq), F32), pltpu.VMEM((2, n_q, 1, tq), F32),
                        pltpu.VMEM((2, n_q, V_ROWS, tq), F32)],
        compiler_params=_cparams(("parallel", "parallel")),
        name="mla_attn",
    )(qt, k, vt)


def _proj_t_ln_kernel(ot_ref, x_ref, w_ref, g_ref, b_ref, out_ref):
    y = ALPHA * x_ref[0] + _dot_tn(ot_ref[0], w_ref[...])
    out_ref[0] = _layer_norm(y, g_ref[...], b_ref[...])


def _proj_t_ln(ot, x3d, w_bf16, ln_g, ln_b, tm):
    b, s, d = x3d.shape
    k = ot.shape[1]
    return pl.pallas_call(
        _proj_t_ln_kernel,
        grid=(b, s // tm),
        in_specs=[pl.BlockSpec((1, k, tm), lambda bi, si: (bi, 0, si)),
                  pl.BlockSpec((1, tm, d), lambda bi, si: (bi, si, 0)),
                  _resident((k, d)), _resident((1, d)), _resident((1, d))],
        out_specs=pl.BlockSpec((1, tm, d), lambda bi, si: (bi, si, 0)),
        out_shape=jax.ShapeDtypeStruct((b, s, d), F32),
        compiler_params=_cparams(("parallel", "parallel")),
        name="proj_t_ln",
    )(ot, x3d, w_bf16, ln_g, ln_b)


def _encoder(x, bias, w_qkv_a, w_o_a, w_dkv_b, g_q_b, g_kv_b, w_uq_b, w_ukv_b, w_o_b,
             ffn_w_in, ffn_conv_w, ffn_conv_b, ffn_w_out, ln_g, ln_b):
    b, s, d = x.shape
    tm = 512
    for i in range(DEPTH):
        j = i // N_MIXERS
        g1, b1 = ln_g[i, 0][None], ln_b[i, 0][None]
        g2, b2 = ln_g[i, 1][None], ln_b[i, 1][None]
        if i % N_MIXERS == 0:
            qkv = _qkv_proj(x, w_qkv_a[j].astype(BF16), tm=256)
            outs = [_dilated_attn(qkv[g], bias, g) for g in range(A_GROUPS)]
            x = _merge_proj_ln([o for o, _ in outs], [l for _, l in outs], x,
                               w_o_a[j].astype(BF16), g1, b1, tm)
        else:
            q, k, vt = _mla_proj(x, w_dkv_b[j], g_q_b[j], g_kv_b[j], w_uq_b[j], w_ukv_b[j], tm)
            ot = _mla_attn(q, k, vt, n_q=2, tq=512, tk=256)
            x = _proj_t_ln(ot, x, w_o_b[j].astype(BF16), g1, b1, tm)
        x = _ffn_ln(x, ffn_w_in[i], ffn_conv_w[i], ffn_conv_b[i], ffn_w_out[i], g2, b2, tm)
    return x


def kernel(x_prompt, x_sample, rel_bias, w_qkv_a, w_o_a, w_dkv_b, g_q_b, g_kv_b, w_uq_b, w_ukv_b, w_o_b,
           ffn_w_in, ffn_conv_w, ffn_conv_b, ffn_w_out, ln_g, ln_b):
    bias = _bias_tiles(rel_bias)
    args = (bias, w_qkv_a, w_o_a, w_dkv_b, g_q_b, g_kv_b, w_uq_b, w_ukv_b, w_o_b,
            ffn_w_in, ffn_conv_w, ffn_conv_b, ffn_w_out, ln_g, ln_b)
    return (_encoder(x_prompt, *args), _encoder(x_sample, *args))
```

```python
import functools
import math

import jax
import jax.numpy as jnp
import numpy as np
from jax import lax
from jax.experimental import pallas as pl
from jax.experimental.pallas import tpu as pltpu

F32 = jnp.float32
BF16 = jnp.bfloat16

D_MODEL = 1024
DEPTH = 2
N_MIXERS = 2

A_WINDOWS = (128, 512, 2048)
A_DILATIONS = (1, 4, 16)
A_GROUPS = 3
A_HEADS = 8
A_HEAD_DIM = 128
A_WIDTH = A_HEADS * A_HEAD_DIM
A_HALF = 64
NUM_BUCKETS = 32
MAX_DISTANCE = max(A_WINDOWS) // 2

B_HEADS = 16
Q_LORA = 384
KV_LORA = 256
QK_NOPE = 64
QK_ROPE = 32
V_HEAD = 64
ROPE_THETA = 10000.0
HEAD_PAD = 128
V_ROWS = V_HEAD + 16
N_SLOTS = 2
MAX_UNROLL = 8

D_FF = 2816
FF_CHUNK = 256

ALPHA = (2.0 * DEPTH) ** 0.25
LN_EPS = 1e-5
RMS_EPS = 1e-6
NEG_INF = -1e30

LANES = 128
SUBLANES = 8
VMEM_LIMIT = 56 * 1024 * 1024

QK_SCALE_LOG2E = np.float32((QK_NOPE + QK_ROPE) ** -0.5 * math.log2(math.e))


def _cparams(sem):
    return pltpu.CompilerParams(dimension_semantics=sem, vmem_limit_bytes=VMEM_LIMIT)


def _resident(shape):
    nd = len(shape)
    return pl.BlockSpec(shape, lambda *_: (0,) * nd, pipeline_mode=pl.Buffered(1))


def _layer_norm(y, g, b):
    mu = jnp.mean(y, axis=-1, keepdims=True)
    yc = y - mu
    var = jnp.mean(yc * yc, axis=-1, keepdims=True)
    return yc * lax.rsqrt(var + LN_EPS) * g + b


def _dot(a, b):
    return jnp.dot(a, b, preferred_element_type=F32)


def _dot_nt(a, b):
    return lax.dot_general(a, b, (((1,), (1,)), ((), ())), preferred_element_type=F32)


def _dot_tn(a, b):
    return lax.dot_general(a, b, (((0,), (0,)), ((), ())), preferred_element_type=F32)


def _qkv_proj_kernel(x_ref, w_ref, *refs, tm):
    outs, res_scr = refs[:A_GROUPS], refs[A_GROUPS]
    xb = x_ref[0].astype(BF16)
    slot = 0
    for g, dil in enumerate(A_DILATIONS):
        for c in range(3):
            n0 = (g * 3 + c) * A_WIDTH
            cols = slice(c * A_WIDTH, (c + 1) * A_WIDTH)
            res = _dot(xb, w_ref[:, n0:n0 + A_WIDTH])
            if dil == 1:
                outs[g][0, 0, :, cols] = res.astype(BF16)
                continue
            for h in range(A_HEADS):
                res_scr[slot, h] = res[:, h * LANES:(h + 1) * LANES]
            for r in range(dil):
                for h in range(A_HEADS):
                    lanes = slice(c * A_WIDTH + h * LANES, c * A_WIDTH + (h + 1) * LANES)
                    outs[g][0, r, :, lanes] = res_scr[slot, h, pl.ds(r, tm // dil, stride=dil), :].astype(BF16)
            slot = 1 - slot


def _qkv_proj(x3d, w_bf16, tm):
    b, s, d = x3d.shape
    width = 3 * A_WIDTH
    return pl.pallas_call(
        functools.partial(_qkv_proj_kernel, tm=tm),
        grid=(b, s // tm),
        in_specs=[pl.BlockSpec((1, tm, d), lambda bi, i: (bi, i, 0)), _resident(w_bf16.shape)],
        out_specs=[pl.BlockSpec((1, dil, tm // dil, width), lambda bi, i: (bi, 0, i, 0))
                   for dil in A_DILATIONS],
        out_shape=[jax.ShapeDtypeStruct((b, dil, s // dil, width), BF16) for dil in A_DILATIONS],
        scratch_shapes=[pltpu.VMEM((2, A_HEADS, tm, LANES), F32)],
        compiler_params=_cparams(("parallel", "parallel")),
        name="qkv_proj",
    )(x3d, w_bf16)


def _t5_bucket(rel):
    nb = NUM_BUCKETS // 2
    max_exact = nb // 2
    ret = jnp.where(rel > 0, nb, 0)
    n = jnp.abs(rel)
    large = max_exact + (jnp.log(jnp.maximum(n, 1).astype(F32) / max_exact)
                         / math.log(MAX_DISTANCE / max_exact) * (nb - max_exact)).astype(jnp.int32)
    large = jnp.minimum(large, nb - 1)
    return ret + jnp.where(n < max_exact, n, large)


def _band_bucket_index():
    qi = jnp.arange(2 * A_HALF)[:, None]
    kj = jnp.arange(4 * A_HALF)[None, :]
    rel = kj - A_HALF - qi
    tiles = [jnp.where(jnp.abs(rel) <= A_HALF, _t5_bucket(rel * dil), -1) for dil in A_DILATIONS]
    return jnp.stack(tiles, axis=0).astype(jnp.int32)


def _bias_tile_kernel(tab_ref, idx_ref, o_ref):
    col = pl.program_id(0) * A_HEADS + pl.program_id(1)
    idx = idx_ref[0]
    acc = jnp.full(idx.shape, NEG_INF, F32)
    for bucket in range(NUM_BUCKETS):
        acc = jnp.where(idx == bucket, tab_ref[bucket, col], acc)
    o_ref[0, 0] = acc


def _bias_tiles(rel_bias):
    idx = _band_bucket_index()
    tq, tk = idx.shape[1:]
    return pl.pallas_call(
        _bias_tile_kernel,
        grid=(A_GROUPS, A_HEADS),
        in_specs=[pl.BlockSpec(memory_space=pltpu.SMEM),
                  pl.BlockSpec((1, tq, tk), lambda g, h: (g, 0, 0))],
        out_specs=pl.BlockSpec((1, 1, tq, tk), lambda g, h: (g, h, 0, 0)),
        out_shape=jax.ShapeDtypeStruct((A_GROUPS, A_HEADS, tq, tk), F32),
        compiler_params=_cparams(("arbitrary", "arbitrary")),
        name="bias_tiles",
    )(rel_bias, idx)


def _dilated_attn_kernel(q_ref, kl_ref, km_ref, kr_ref, vl_ref, vm_ref, vr_ref, bias_ref,
                         o_ref, lse_ref, kbuf, vbuf, *, tq, sub_len):
    li = pl.program_id(2)
    kbuf[0:A_HALF] = kl_ref[0, 0]
    kbuf[A_HALF:A_HALF + tq] = km_ref[0, 0]
    kbuf[A_HALF + tq:] = kr_ref[0, 0]
    vbuf[0:A_HALF] = vl_ref[0, 0]
    vbuf[A_HALF:A_HALF + tq] = vm_ref[0, 0]
    vbuf[A_HALF + tq:] = vr_ref[0, 0]

    scale = A_HEAD_DIM ** -0.5
    qt, kt = 2 * A_HALF, 4 * A_HALF
    lane = lax.broadcasted_iota(jnp.int32, (qt, LANES), 1)
    for t in range(tq // qt):
        pos = li * tq + t * qt - A_HALF + lax.broadcasted_iota(jnp.int32, (1, kt), 1)
        edge = jnp.where((pos >= 0) & (pos < sub_len), 0.0, NEG_INF).astype(F32)
        lse_tile = jnp.zeros((qt, LANES), F32)
        for h in range(A_HEADS):
            cols = slice(h * A_HEAD_DIM, (h + 1) * A_HEAD_DIM)
            q = q_ref[0, 0, t * qt:(t + 1) * qt, cols]
            k = kbuf[t * qt:t * qt + kt, cols]
            v = vbuf[t * qt:t * qt + kt, cols]
            s = _dot_nt(q, k) * scale + bias_ref[0, h] + edge
            m = jnp.max(s, axis=-1, keepdims=True)
            p = jnp.exp(s - m)
            den = jnp.sum(p, axis=-1, keepdims=True)
            o = _dot(p.astype(BF16), v) / den
            o_ref[0, 0, t * qt:(t + 1) * qt, cols] = o.astype(BF16)
            lse_tile = jnp.where(lane == h, m + jnp.log(den), lse_tile)
        lse_ref[0, 0, t * qt:(t + 1) * qt, :] = lse_tile


def _dilated_attn(qkv_g, bias, g):
    batch, dil, sub_len, _ = qkv_g.shape
    tq = min(256, sub_len)
    n_t = sub_len // tq
    hb = tq // A_HALF
    n_hb = sub_len // A_HALF

    main = lambda which: pl.BlockSpec((1, 1, tq, A_WIDTH), lambda b, r, i: (b, r, i, which))
    left = lambda which: pl.BlockSpec(
        (1, 1, A_HALF, A_WIDTH), lambda b, r, i: (b, r, jnp.maximum(i * hb - 1, 0), which))
    right = lambda which: pl.BlockSpec(
        (1, 1, A_HALF, A_WIDTH), lambda b, r, i: (b, r, jnp.minimum((i + 1) * hb, n_hb - 1), which))

    return pl.pallas_call(
        functools.partial(_dilated_attn_kernel, tq=tq, sub_len=sub_len),
        grid=(batch, dil, n_t),
        in_specs=[main(0), left(1), main(1), right(1), left(2), main(2), right(2),
                  pl.BlockSpec((1, A_HEADS, 2 * A_HALF, 4 * A_HALF), lambda b, r, i: (g, 0, 0, 0))],
        out_specs=[pl.BlockSpec((1, 1, tq, A_WIDTH), lambda b, r, i: (b, r, i, 0)),
                   pl.BlockSpec((1, 1, tq, LANES), lambda b, r, i: (b, r, i, 0))],
        out_shape=[jax.ShapeDtypeStruct((batch, dil, sub_len, A_WIDTH), BF16),
                   jax.ShapeDtypeStruct((batch, dil, sub_len, LANES), F32)],
        scratch_shapes=[pltpu.VMEM((tq + 2 * A_HALF, A_WIDTH), BF16),
                        pltpu.VMEM((tq + 2 * A_HALF, A_WIDTH), BF16)],
        compiler_params=_cparams(("parallel", "parallel", "parallel")),
        name=f"dilated_attn_g{g}",
    )(qkv_g, qkv_g, qkv_g, qkv_g, qkv_g, qkv_g, qkv_g, bias)


def _merge_proj_ln_kernel(o0_ref, o1_ref, o2_ref, l0_ref, l1_ref, l2_ref, x_ref, w_ref, g_ref, b_ref,
                          out_ref, merged, o_scr, l_scr, *, tm):
    for g, (o_ref, l_ref, dil) in enumerate(zip((o0_ref, o1_ref, o2_ref), (l0_ref, l1_ref, l2_ref),
                                                A_DILATIONS)):
        for r in range(dil):
            rows = pl.ds(r, tm // dil, stride=dil) if dil > 1 else slice(None)
            l_scr[g, rows, :] = l_ref[0, r]
            for h in range(A_HEADS):
                o_scr[g, h, rows, :] = o_ref[0, r, :, h * A_HEAD_DIM:(h + 1) * A_HEAD_DIM].astype(F32)
    l0, l1, l2 = l_scr[0], l_scr[1], l_scr[2]
    m = jnp.maximum(jnp.maximum(l0, l1), l2)
    e0, e1, e2 = jnp.exp(l0 - m), jnp.exp(l1 - m), jnp.exp(l2 - m)
    den = e0 + e1 + e2
    w0, w1, w2 = e0 / den, e1 / den, e2 / den
    for h in range(A_HEADS):
        cols = slice(h * A_HEAD_DIM, (h + 1) * A_HEAD_DIM)
        mix = (w0[:, h:h + 1] * o_scr[0, h] + w1[:, h:h + 1] * o_scr[1, h]
               + w2[:, h:h + 1] * o_scr[2, h])
        merged[:, cols] = mix.astype(BF16)
    y = ALPHA * x_ref[0] + _dot(merged[...], w_ref[...])
    out_ref[0] = _layer_norm(y, g_ref[...], b_ref[...])


def _merge_proj_ln(os, lses, x3d, w_bf16, ln_g, ln_b, tm):
    b, s, d = x3d.shape
    plane = lambda w: [pl.BlockSpec((1, dil, tm // dil, w), lambda bi, i: (bi, 0, i, 0))
                       for dil in A_DILATIONS]
    row = pl.BlockSpec((1, tm, d), lambda bi, i: (bi, i, 0))
    return pl.pallas_call(
        functools.partial(_merge_proj_ln_kernel, tm=tm),
        grid=(b, s // tm),
        in_specs=plane(A_WIDTH) + plane(LANES) + [row, _resident((A_WIDTH, d)),
                                                  _resident((1, d)), _resident((1, d))],
        out_specs=row,
        out_shape=jax.ShapeDtypeStruct((b, s, d), F32),
        scratch_shapes=[pltpu.VMEM((tm, A_WIDTH), BF16),
                        pltpu.VMEM((A_GROUPS, A_HEADS, tm, A_HEAD_DIM), F32),
                        pltpu.VMEM((A_GROUPS, tm, LANES), F32)],
        compiler_params=_cparams(("parallel", "parallel")),
        name="merge_proj_ln",
    )(*os, *lses, x3d, w_bf16, ln_g, ln_b)


def _ffn_ln_kernel(x_ref, xp_ref, xn_ref, win_ref, cw_ref, cb_ref, wout_ref, g_ref, b_ref,
                   out_ref, act, *, tm):
    si = pl.program_id(1)
    n_s = pl.num_programs(1)
    x = x_ref[0]
    xb = x.astype(BF16)
    prev_ok = (si > 0).astype(F32)
    next_ok = (si < n_s - 1).astype(F32)
    xe = jnp.concatenate([xp_ref[0] * prev_ok, xn_ref[0] * next_ok], axis=0).astype(BF16)
    row = lax.broadcasted_iota(jnp.int32, (tm, 2 * FF_CHUNK), 0)
    for c in range(D_FF // FF_CHUNK):
        cols = slice(c * 2 * FF_CHUNK, (c + 1) * 2 * FF_CHUNK)
        w = win_ref[:, cols]
        h = _dot(xb, w)
        he = _dot(xe, w)
        up = jnp.where(row == 0, he[SUBLANES - 1:SUBLANES], pltpu.roll(h, 1, 0))
        dn = jnp.where(row == tm - 1, he[SUBLANES:SUBLANES + 1], pltpu.roll(h, tm - 1, 0))
        cw = cw_ref[:, cols]
        hc = up * cw[0:1] + h * cw[1:2] + dn * cw[2:3] + cb_ref[:, cols]
        a = hc[:, :FF_CHUNK]
        gate = hc[:, FF_CHUNK:]
        gelu = 0.5 * gate * (1.0 + lax.erf(gate * np.float32(math.sqrt(0.5))))
        act[:, c * FF_CHUNK:(c + 1) * FF_CHUNK] = (a * gelu).astype(BF16)
    y = ALPHA * x + _dot(act[...], wout_ref[...])
    out_ref[0] = _layer_norm(y, g_ref[...], b_ref[...])


def _chunk_interleave(w):
    lead = w.shape[:-1]
    w = w.reshape(lead + (2, D_FF // FF_CHUNK, FF_CHUNK))
    w = jnp.swapaxes(w, -3, -2)
    return w.reshape(lead + (2 * D_FF,))


def _ffn_ln(x3d, w_in, conv_w, conv_b, w_out, ln_g, ln_b, tm):
    b, s, d = x3d.shape
    nb8 = tm // SUBLANES
    n8 = s // SUBLANES
    win = _chunk_interleave(w_in).astype(BF16)
    cw = _chunk_interleave(conv_w)
    cb = _chunk_interleave(conv_b)[None]
    return pl.pallas_call(
        functools.partial(_ffn_ln_kernel, tm=tm),
        grid=(b, s // tm),
        in_specs=[pl.BlockSpec((1, tm, d), lambda bi, si: (bi, si, 0)),
                  pl.BlockSpec((1, SUBLANES, d), lambda bi, si: (bi, jnp.maximum(si * nb8 - 1, 0), 0)),
                  pl.BlockSpec((1, SUBLANES, d), lambda bi, si: (bi, jnp.minimum((si + 1) * nb8, n8 - 1), 0)),
                  _resident((d, 2 * D_FF)), _resident((3, 2 * D_FF)), _resident((1, 2 * D_FF)),
                  _resident((D_FF, d)), _resident((1, d)), _resident((1, d))],
        out_specs=pl.BlockSpec((1, tm, d), lambda bi, si: (bi, si, 0)),
        out_shape=jax.ShapeDtypeStruct((b, s, d), F32),
        scratch_shapes=[pltpu.VMEM((tm, D_FF), BF16)],
        compiler_params=_cparams(("parallel", "parallel")),
        name="ffn_ln",
    )(x3d, x3d, x3d, win, cw, cb, w_out.astype(BF16), ln_g, ln_b)


def _rope_tables(seq):
    inv = 1.0 / (ROPE_THETA ** (jnp.arange(0, QK_ROPE, 2, dtype=F32) / QK_ROPE))
    ang = jnp.arange(seq, dtype=F32)[:, None] * inv[None, :]
    cos, sin = jnp.cos(ang), jnp.sin(ang)
    half = QK_ROPE // 2
    ones = jnp.ones((seq, QK_NOPE), F32)
    z_nope = jnp.zeros((seq, QK_NOPE), F32)
    z_half = jnp.zeros((seq, half), F32)
    z_pad = jnp.zeros((seq, HEAD_PAD - QK_NOPE - QK_ROPE), F32)
    cmul = jnp.concatenate([ones, cos, cos, z_pad], axis=1)
    s_first = jnp.concatenate([z_nope, -sin, z_half, z_pad], axis=1)
    s_second = jnp.concatenate([z_nope, z_half, sin, z_pad], axis=1)
    return cmul, s_first, s_second


def _rope(t, cmul, s_first, s_second):
    half = QK_ROPE // 2
    return (t * cmul + pltpu.roll(t, HEAD_PAD - half, 1) * s_first
            + pltpu.roll(t, half, 1) * s_second)


def _rms_norm(x, g):
    ms = jnp.mean(x * x, axis=-1, keepdims=True)
    return x * lax.rsqrt(ms + RMS_EPS) * g


def _mla_proj_kernel(x_ref, wd_ref, gq_ref, gkv_ref, wuqt_ref, wuk_ref, wuvt_ref,
                     cm_ref, s1_ref, s2_ref, cmt_ref, s1t_ref, s2t_ref, qt_ref, k_ref, vt_ref):
    xb = x_ref[0].astype(BF16)
    c = _dot(xb, wd_ref[...])
    cqn = _rms_norm(c[:, :Q_LORA], gq_ref[...]).astype(BF16)
    ckvn = _rms_norm(c[:, Q_LORA:Q_LORA + KV_LORA], gkv_ref[...]).astype(BF16)
    cm, s1, s2 = cm_ref[...], s1_ref[...], s2_ref[...]
    kr = _rope(c[:, Q_LORA + KV_LORA:], cm, s1, s2)
    k = _dot(ckvn, wuk_ref[...])
    for h in range(B_HEADS):
        cols = slice(h * HEAD_PAD, (h + 1) * HEAD_PAD)
        k_ref[0, :, cols] = (k[:, cols] + kr).astype(BF16)
    qt = _dot_nt(wuqt_ref[...], cqn)
    cmt, s1t, s2t = cmt_ref[...], s1t_ref[...], s2t_ref[...]
    half = QK_ROPE // 2
    for h in range(B_HEADS):
        rows = slice(h * HEAD_PAD, (h + 1) * HEAD_PAD)
        t = qt[rows]
        rot = t * cmt + pltpu.roll(t, HEAD_PAD - half, 0) * s1t + pltpu.roll(t, half, 0) * s2t
        qt_ref[0, rows, :] = (rot * QK_SCALE_LOG2E).astype(BF16)
    vt = _dot_nt(wuvt_ref[...], ckvn).astype(BF16)
    tm = vt.shape[1]
    row = lax.broadcasted_iota(jnp.int32, (V_ROWS - V_HEAD, tm), 0)
    ones_row = jnp.where(row == 0, 1.0, 0.0).astype(BF16)
    for h in range(B_HEADS):
        vt_ref[0, h * V_ROWS:h * V_ROWS + V_HEAD, :] = vt[h * V_HEAD:(h + 1) * V_HEAD]
        vt_ref[0, h * V_ROWS + V_HEAD:(h + 1) * V_ROWS, :] = ones_row


def _mla_weights(w_dkv, w_uq, w_ukv):
    d = w_dkv.shape[0]
    pad_l = jnp.zeros((d, QK_NOPE), F32)
    pad_r = jnp.zeros((d, HEAD_PAD - QK_NOPE - QK_ROPE), F32)
    wd = jnp.concatenate([w_dkv[:, :Q_LORA + KV_LORA], pad_l, w_dkv[:, Q_LORA + KV_LORA:], pad_r], axis=1)
    wq = w_uq.reshape(Q_LORA, B_HEADS, QK_NOPE + QK_ROPE)
    wq = jnp.pad(wq, ((0, 0), (0, 0), (0, HEAD_PAD - QK_NOPE - QK_ROPE))).reshape(Q_LORA, B_HEADS * HEAD_PAD)
    wkv = w_ukv.reshape(KV_LORA, B_HEADS, QK_NOPE + V_HEAD)
    wk = jnp.pad(wkv[:, :, :QK_NOPE], ((0, 0), (0, 0), (0, HEAD_PAD - QK_NOPE))).reshape(KV_LORA, B_HEADS * HEAD_PAD)
    wvt = wkv[:, :, QK_NOPE:].reshape(KV_LORA, B_HEADS * V_HEAD).T
    return wd.astype(BF16), wq.T.astype(BF16), wk.astype(BF16), wvt.astype(BF16)


def _mla_proj(x3d, w_dkv, g_q, g_kv, w_uq, w_ukv, tm):
    b, s, d = x3d.shape
    wd, wqt, wk, wvt = _mla_weights(w_dkv, w_uq, w_ukv)
    tabs = _rope_tables(s)
    tabs_t = [t.T for t in tabs]
    hq = B_HEADS * HEAD_PAD
    hv = B_HEADS * V_ROWS
    tab = pl.BlockSpec((tm, HEAD_PAD), lambda bi, si: (si, 0))
    tab_t = pl.BlockSpec((HEAD_PAD, tm), lambda bi, si: (0, si))
    return pl.pallas_call(
        _mla_proj_kernel,
        grid=(b, s // tm),
        in_specs=[pl.BlockSpec((1, tm, d), lambda bi, si: (bi, si, 0)),
                  _resident(wd.shape), _resident((1, Q_LORA)), _resident((1, KV_LORA)),
                  _resident(wqt.shape), _resident(wk.shape), _resident(wvt.shape),
                  tab, tab, tab, tab_t, tab_t, tab_t],
        out_specs=[pl.BlockSpec((1, hq, tm), lambda bi, si: (bi, 0, si)),
                   pl.BlockSpec((1, tm, hq), lambda bi, si: (bi, si, 0)),
                   pl.BlockSpec((1, hv, tm), lambda bi, si: (bi, 0, si))],
        out_shape=[jax.ShapeDtypeStruct((b, hq, s), BF16),
                   jax.ShapeDtypeStruct((b, s, hq), BF16),
                   jax.ShapeDtypeStruct((b, hv, s), BF16)],
        compiler_params=_cparams(("parallel", "parallel")),
        name="mla_proj",
    )(x3d, wd, g_q[None], g_kv[None], wqt, wk, wvt, *tabs, *tabs_t)


def _mla_attn_kernel(qt_ref, k_ref, vt_ref, o_ref, s_scr, cmax_scr, m_all, acc_all, *, tq, tk, n_q, unroll):
    seq = k_ref.shape[1]
    nk = seq // tk
    blk_q = n_q * tq
    n_blk = seq // blk_q
    total = n_blk * nk
    trips_per_blk = nk // unroll

    def finalize(blk):
        par = blk % 2
        for j in range(n_q):
            acc = acc_all[par, j]
            q0 = pl.multiple_of(blk * blk_q + j * tq, tq)
            o_ref[0, :, pl.ds(q0, tq)] = (acc[:V_HEAD] / acc[V_HEAD:V_HEAD + 1]).astype(BF16)

    def run_step(t, t_static, do_qk=True):
        s_qk, s_sm = t_static % N_SLOTS, (t_static - 1) % N_SLOTS
        n_sm = t - 1 + nk
        par_sm, first_sm = (n_sm // nk + 1) % 2, (n_sm % nk) == 0
        v0 = pl.multiple_of((n_sm % nk) * tk, tk)
        vt = vt_ref[0, :, pl.ds(v0, tk)]
        if do_qk:
            k0 = pl.multiple_of((t % nk) * tk, tk)
            k = k_ref[0, pl.ds(k0, tk), :]
            for j in range(n_q):
                q0 = pl.multiple_of((t // nk) * blk_q + j * tq, tq)
                sc = _dot(k, qt_ref[0, :, pl.ds(q0, tq)])
                s_scr[s_qk, j] = sc
                cmax_scr[s_qk, j] = jnp.max(sc, axis=0, keepdims=True)
        for j in range(n_q):
            m = jnp.where(first_sm, NEG_INF, m_all[par_sm, j])
            m_new = jnp.maximum(m, cmax_scr[s_sm, j])
            alpha = jnp.exp2(m - m_new)
            m_all[par_sm, j] = m_new
            p = jnp.exp2(s_scr[s_sm, j] - m_new).astype(BF16)
            acc_all[par_sm, j] = alpha * acc_all[par_sm, j] + _dot(vt, p)

    s_scr[...] = jnp.zeros(s_scr.shape, F32)
    cmax_scr[...] = jnp.zeros(cmax_scr.shape, F32)
    m_all[...] = jnp.full(m_all.shape, NEG_INF, F32)
    acc_all[...] = jnp.zeros(acc_all.shape, F32)

    def trip(u, carry):
        @pl.when((u % trips_per_blk == 1) & (u > trips_per_blk))
        def _():
            finalize(u // trips_per_blk - 1)

        for i in range(unroll):
            run_step(unroll * u + i, i)
        return carry

    lax.fori_loop(0, total // unroll, trip, 0)
    run_step(total, total, do_qk=False)
    finalize(n_blk - 1)


def _mla_attn(qt, k, vt, n_q, tq, tk):
    b, s, _ = k.shape
    nk = s // tk
    unroll = max(u for u in range(N_SLOTS, MAX_UNROLL + 1, N_SLOTS) if nk % u == 0 and nk // u >= 2)
    assert s % (n_q * tq) == 0 and N_SLOTS == 2
    return pl.pallas_call(
        functools.partial(_mla_attn_kernel, tq=tq, tk=tk, n_q=n_q, unroll=unroll),
        grid=(b, B_HEADS),
        in_specs=[pl.BlockSpec((1, HEAD_PAD, s), lambda bi, h: (bi, h, 0)),
                  pl.BlockSpec((1, s, HEAD_PAD), lambda bi, h: (bi, 0, h)),
                  pl.BlockSpec((1, V_ROWS, s), lambda bi, h: (bi, h, 0))],
        out_specs=pl.BlockSpec((1, V_HEAD, s), lambda bi, h: (bi, h, 0)),
        out_shape=jax.ShapeDtypeStruct((b, B_HEADS * V_HEAD, s), BF16),
        scratch_shapes=[pltpu.VMEM((N_SLOTS, n_q, tk, tq), F32), pltpu.VMEM((N_SLOTS, n_q, 1, tq), F32),
                        pltpu.VMEM((2, n_q, 1, tq), F32), pltpu.VMEM((2, n_q, V_ROWS, tq), F32)],
        compiler_params=_cparams(("parallel", "parallel")),
        name="mla_attn",
    )(qt, k, vt)


def _proj_t_ln_kernel(ot_ref, x_ref, w_ref, g_ref, b_ref, out_ref):
    y = ALPHA * x_ref[0] + _dot_tn(ot_ref[0], w_ref[...])
    out_ref[0] = _layer_norm(y, g_ref[...], b_ref[...])


def _proj_t_ln(ot, x3d, w_bf16, ln_g, ln_b, tm):
    b, s, d = x3d.shape
    k = ot.shape[1]
    return pl.pallas_call(
        _proj_t_ln_kernel,
        grid=(b, s // tm),
        in_specs=[pl.BlockSpec((1, k, tm), lambda bi, si: (bi, 0, si)),
                  pl.BlockSpec((1, tm, d), lambda bi, si: (bi, si, 0)),
                  _resident((k, d)), _resident((1, d)), _resident((1, d))],
        out_specs=pl.BlockSpec((1, tm, d), lambda bi, si: (bi, si, 0)),
        out_shape=jax.ShapeDtypeStruct((b, s, d), F32),
        compiler_params=_cparams(("parallel", "parallel")),
        name="proj_t_ln",
    )(ot, x3d, w_bf16, ln_g, ln_b)


def _encoder(x, bias, w_qkv_a, w_o_a, w_dkv_b, g_q_b, g_kv_b, w_uq_b, w_ukv_b, w_o_b,
             ffn_w_in, ffn_conv_w, ffn_conv_b, ffn_w_out, ln_g, ln_b):
    b, s, d = x.shape
    tm = 512
    for i in range(DEPTH):
        j = i // N_MIXERS
        g1, b1 = ln_g[i, 0][None], ln_b[i, 0][None]
        g2, b2 = ln_g[i, 1][None], ln_b[i, 1][None]
        if i % N_MIXERS == 0:
            qkv = _qkv_proj(x, w_qkv_a[j].astype(BF16), tm=256)
            outs = [_dilated_attn(qkv[g], bias, g) for g in range(A_GROUPS)]
            x = _merge_proj_ln([o for o, _ in outs], [l for _, l in outs], x,
                               w_o_a[j].astype(BF16), g1, b1, tm)
        else:
            q, k, vt = _mla_proj(x, w_dkv_b[j], g_q_b[j], g_kv_b[j], w_uq_b[j], w_ukv_b[j], tm)
            ot = _mla_attn(q, k, vt, n_q=2, tq=512, tk=256)
            x = _proj_t_ln(ot, x, w_o_b[j].astype(BF16), g1, b1, tm)
        x = _ffn_ln(x, ffn_w_in[i], ffn_conv_w[i], ffn_conv_b[i], ffn_w_out[i], g2, b2, tm)
    return x


def kernel(x_prompt, x_sample, rel_bias, w_qkv_a, w_o_a, w_dkv_b, g_q_b, g_kv_b, w_uq_b, w_ukv_b, w_o_b,
           ffn_w_in, ffn_conv_w, ffn_conv_b, ffn_w_out, ln_g, ln_b):
    bias = _bias_tiles(rel_bias)
    args = (bias, w_qkv_a, w_o_a, w_dkv_b, g_q_b, g_kv_b, w_uq_b, w_ukv_b, w_o_b,
            ffn_w_in, ffn_conv_w, ffn_conv_b, ffn_w_out, ln_g, ln_b)
    return (_encoder(x_prompt, *args), _encoder(x_sample, *args))
```

```python
import functools
import math

import jax
import jax.numpy as jnp
import numpy as np
from jax import lax
from jax.experimental import pallas as pl
from jax.experimental.pallas import tpu as pltpu

F32 = jnp.float32
BF16 = jnp.bfloat16

D_MODEL = 1024
DEPTH = 2
N_MIXERS = 2

A_WINDOWS = (128, 512, 2048)
A_DILATIONS = (1, 4, 16)
A_GROUPS = 3
A_HEADS = 8
A_HEAD_DIM = 128
A_WIDTH = A_HEADS * A_HEAD_DIM
A_HALF = 64
NUM_BUCKETS = 32
MAX_DISTANCE = max(A_WINDOWS) // 2

B_HEADS = 16
Q_LORA = 384
KV_LORA = 256
QK_NOPE = 64
QK_ROPE = 32
V_HEAD = 64
ROPE_THETA = 10000.0
HEAD_PAD = 128
V_ROWS = V_HEAD + 16
N_SLOTS = 2
MAX_UNROLL = 16

D_FF = 2816
FF_CHUNK = 256

ALPHA = (2.0 * DEPTH) ** 0.25
LN_EPS = 1e-5
RMS_EPS = 1e-6
NEG_INF = -1e30

LANES = 128
SUBLANES = 8
VMEM_LIMIT = 56 * 1024 * 1024

QK_SCALE_LOG2E = np.float32((QK_NOPE + QK_ROPE) ** -0.5 * math.log2(math.e))


def _cparams(sem):
    return pltpu.CompilerParams(dimension_semantics=sem, vmem_limit_bytes=VMEM_LIMIT)


def _resident(shape):
    nd = len(shape)
    return pl.BlockSpec(shape, lambda *_: (0,) * nd, pipeline_mode=pl.Buffered(1))


def _layer_norm(y, g, b):
    mu = jnp.mean(y, axis=-1, keepdims=True)
    yc = y - mu
    var = jnp.mean(yc * yc, axis=-1, keepdims=True)
    return yc * lax.rsqrt(var + LN_EPS) * g + b


def _dot(a, b):
    return jnp.dot(a, b, preferred_element_type=F32)


def _dot_nt(a, b):
    return lax.dot_general(a, b, (((1,), (1,)), ((), ())), preferred_element_type=F32)


def _dot_tn(a, b):
    return lax.dot_general(a, b, (((0,), (0,)), ((), ())), preferred_element_type=F32)


def _qkv_proj_kernel(x_ref, w_ref, *refs, tm):
    outs, res_scr = refs[:A_GROUPS], refs[A_GROUPS]
    xb = x_ref[0].astype(BF16)
    slot = 0
    for g, dil in enumerate(A_DILATIONS):
        for c in range(3):
            n0 = (g * 3 + c) * A_WIDTH
            cols = slice(c * A_WIDTH, (c + 1) * A_WIDTH)
            res = _dot(xb, w_ref[:, n0:n0 + A_WIDTH])
            if dil == 1:
                outs[g][0, 0, :, cols] = res.astype(BF16)
                continue
            for h in range(A_HEADS):
                res_scr[slot, h] = res[:, h * LANES:(h + 1) * LANES]
            for r in range(dil):
                for h in range(A_HEADS):
                    lanes = slice(c * A_WIDTH + h * LANES, c * A_WIDTH + (h + 1) * LANES)
                    outs[g][0, r, :, lanes] = res_scr[slot, h, pl.ds(r, tm // dil, stride=dil), :].astype(BF16)
            slot = 1 - slot


def _qkv_proj(x3d, w_bf16, tm):
    b, s, d = x3d.shape
    width = 3 * A_WIDTH
    return pl.pallas_call(
        functools.partial(_qkv_proj_kernel, tm=tm),
        grid=(b, s // tm),
        in_specs=[pl.BlockSpec((1, tm, d), lambda bi, i: (bi, i, 0)), _resident(w_bf16.shape)],
        out_specs=[pl.BlockSpec((1, dil, tm // dil, width), lambda bi, i: (bi, 0, i, 0))
                   for dil in A_DILATIONS],
        out_shape=[jax.ShapeDtypeStruct((b, dil, s // dil, width), BF16) for dil in A_DILATIONS],
        scratch_shapes=[pltpu.VMEM((2, A_HEADS, tm, LANES), F32)],
        compiler_params=_cparams(("parallel", "parallel")),
        name="qkv_proj",
    )(x3d, w_bf16)


def _t5_bucket(rel):
    nb = NUM_BUCKETS // 2
    max_exact = nb // 2
    ret = jnp.where(rel > 0, nb, 0)
    n = jnp.abs(rel)
    large = max_exact + (jnp.log(jnp.maximum(n, 1).astype(F32) / max_exact)
                         / math.log(MAX_DISTANCE / max_exact) * (nb - max_exact)).astype(jnp.int32)
    large = jnp.minimum(large, nb - 1)
    return ret + jnp.where(n < max_exact, n, large)


def _band_bucket_index():
    qi = jnp.arange(2 * A_HALF)[:, None]
    kj = jnp.arange(4 * A_HALF)[None, :]
    rel = kj - A_HALF - qi
    tiles = [jnp.where(jnp.abs(rel) <= A_HALF, _t5_bucket(rel * dil), -1) for dil in A_DILATIONS]
    return jnp.stack(tiles, axis=0).astype(jnp.int32)


def _bias_tile_kernel(tab_ref, idx_ref, o_ref):
    col = pl.program_id(0) * A_HEADS + pl.program_id(1)
    idx = idx_ref[0]
    acc = jnp.full(idx.shape, NEG_INF, F32)
    for bucket in range(NUM_BUCKETS):
        acc = jnp.where(idx == bucket, tab_ref[bucket, col], acc)
    o_ref[0, 0] = acc


def _bias_tiles(rel_bias):
    idx = _band_bucket_index()
    tq, tk = idx.shape[1:]
    return pl.pallas_call(
        _bias_tile_kernel,
        grid=(A_GROUPS, A_HEADS),
        in_specs=[pl.BlockSpec(memory_space=pltpu.SMEM),
                  pl.BlockSpec((1, tq, tk), lambda g, h: (g, 0, 0))],
        out_specs=pl.BlockSpec((1, 1, tq, tk), lambda g, h: (g, h, 0, 0)),
        out_shape=jax.ShapeDtypeStruct((A_GROUPS, A_HEADS, tq, tk), F32),
        compiler_params=_cparams(("arbitrary", "arbitrary")),
        name="bias_tiles",
    )(rel_bias, idx)


def _dilated_attn_kernel(q_ref, kl_ref, km_ref, kr_ref, vl_ref, vm_ref, vr_ref, bias_ref,
                         o_ref, lse_ref, kbuf, vbuf, *, tq, sub_len):
    li = pl.program_id(2)
    kbuf[0:A_HALF] = kl_ref[0, 0]
    kbuf[A_HALF:A_HALF + tq] = km_ref[0, 0]
    kbuf[A_HALF + tq:] = kr_ref[0, 0]
    vbuf[0:A_HALF] = vl_ref[0, 0]
    vbuf[A_HALF:A_HALF + tq] = vm_ref[0, 0]
    vbuf[A_HALF + tq:] = vr_ref[0, 0]

    scale = A_HEAD_DIM ** -0.5
    qt, kt = 2 * A_HALF, 4 * A_HALF
    lane = lax.broadcasted_iota(jnp.int32, (qt, LANES), 1)
    for t in range(tq // qt):
        pos = li * tq + t * qt - A_HALF + lax.broadcasted_iota(jnp.int32, (1, kt), 1)
        edge = jnp.where((pos >= 0) & (pos < sub_len), 0.0, NEG_INF).astype(F32)
        lse_tile = jnp.zeros((qt, LANES), F32)
        for h in range(A_HEADS):
            cols = slice(h * A_HEAD_DIM, (h + 1) * A_HEAD_DIM)
            q = q_ref[0, 0, t * qt:(t + 1) * qt, cols]
            k = kbuf[t * qt:t * qt + kt, cols]
            v = vbuf[t * qt:t * qt + kt, cols]
            s = _dot_nt(q, k) * scale + bias_ref[0, h] + edge
            m = jnp.max(s, axis=-1, keepdims=True)
            p = jnp.exp(s - m)
            den = jnp.sum(p, axis=-1, keepdims=True)
            o = _dot(p.astype(BF16), v) / den
            o_ref[0, 0, t * qt:(t + 1) * qt, cols] = o.astype(BF16)
            lse_tile = jnp.where(lane == h, m + jnp.log(den), lse_tile)
        lse_ref[0, 0, t * qt:(t + 1) * qt, :] = lse_tile


def _dilated_attn(qkv_g, bias, g):
    batch, dil, sub_len, _ = qkv_g.shape
    tq = min(1024, sub_len)
    n_t = sub_len // tq
    hb = tq // A_HALF
    n_hb = sub_len // A_HALF

    main = lambda which: pl.BlockSpec((1, 1, tq, A_WIDTH), lambda b, r, i: (b, r, i, which))
    left = lambda which: pl.BlockSpec(
        (1, 1, A_HALF, A_WIDTH), lambda b, r, i: (b, r, jnp.maximum(i * hb - 1, 0), which))
    right = lambda which: pl.BlockSpec(
        (1, 1, A_HALF, A_WIDTH), lambda b, r, i: (b, r, jnp.minimum((i + 1) * hb, n_hb - 1), which))

    return pl.pallas_call(
        functools.partial(_dilated_attn_kernel, tq=tq, sub_len=sub_len),
        grid=(batch, dil, n_t),
        in_specs=[main(0), left(1), main(1), right(1), left(2), main(2), right(2),
                  pl.BlockSpec((1, A_HEADS, 2 * A_HALF, 4 * A_HALF), lambda b, r, i: (g, 0, 0, 0))],
        out_specs=[pl.BlockSpec((1, 1, tq, A_WIDTH), lambda b, r, i: (b, r, i, 0)),
                   pl.BlockSpec((1, 1, tq, LANES), lambda b, r, i: (b, r, i, 0))],
        out_shape=[jax.ShapeDtypeStruct((batch, dil, sub_len, A_WIDTH), BF16),
                   jax.ShapeDtypeStruct((batch, dil, sub_len, LANES), F32)],
        scratch_shapes=[pltpu.VMEM((tq + 2 * A_HALF, A_WIDTH), BF16),
                        pltpu.VMEM((tq + 2 * A_HALF, A_WIDTH), BF16)],
        compiler_params=_cparams(("parallel", "parallel", "parallel")),
        name=f"dilated_attn_g{g}",
    )(qkv_g, qkv_g, qkv_g, qkv_g, qkv_g, qkv_g, qkv_g, bias)


def _merge_proj_ln_kernel(o0_ref, o1_ref, o2_ref, l0_ref, l1_ref, l2_ref, x_ref, w_ref, g_ref, b_ref,
                          out_ref, merged, o_scr, l_scr, *, tm):
    for g, (o_ref, l_ref, dil) in enumerate(zip((o0_ref, o1_ref, o2_ref), (l0_ref, l1_ref, l2_ref),
                                                A_DILATIONS)):
        for r in range(dil):
            rows = pl.ds(r, tm // dil, stride=dil) if dil > 1 else slice(None)
            l_scr[g, rows, :] = l_ref[0, r]
            for h in range(A_HEADS):
                o_scr[g, h, rows, :] = o_ref[0, r, :, h * A_HEAD_DIM:(h + 1) * A_HEAD_DIM].astype(F32)
    l0, l1, l2 = l_scr[0], l_scr[1], l_scr[2]
    m = jnp.maximum(jnp.maximum(l0, l1), l2)
    e0, e1, e2 = jnp.exp(l0 - m), jnp.exp(l1 - m), jnp.exp(l2 - m)
    den = e0 + e1 + e2
    w0, w1, w2 = e0 / den, e1 / den, e2 / den
    for h in range(A_HEADS):
        cols = slice(h * A_HEAD_DIM, (h + 1) * A_HEAD_DIM)
        mix = (w0[:, h:h + 1] * o_scr[0, h] + w1[:, h:h + 1] * o_scr[1, h]
               + w2[:, h:h + 1] * o_scr[2, h])
        merged[:, cols] = mix.astype(BF16)
    y = ALPHA * x_ref[0] + _dot(merged[...], w_ref[...])
    out_ref[0] = _layer_norm(y, g_ref[...], b_ref[...])


def _merge_proj_ln(os, lses, x3d, w_bf16, ln_g, ln_b, tm):
    b, s, d = x3d.shape
    plane = lambda w: [pl.BlockSpec((1, dil, tm // dil, w), lambda bi, i: (bi, 0, i, 0))
                       for dil in A_DILATIONS]
    row = pl.BlockSpec((1, tm, d), lambda bi, i: (bi, i, 0))
    return pl.pallas_call(
        functools.partial(_merge_proj_ln_kernel, tm=tm),
        grid=(b, s // tm),
        in_specs=plane(A_WIDTH) + plane(LANES) + [row, _resident((A_WIDTH, d)),
                                                  _resident((1, d)), _resident((1, d))],
        out_specs=row,
        out_shape=jax.ShapeDtypeStruct((b, s, d), F32),
        scratch_shapes=[pltpu.VMEM((tm, A_WIDTH), BF16),
                        pltpu.VMEM((A_GROUPS, A_HEADS, tm, A_HEAD_DIM), F32),
                        pltpu.VMEM((A_GROUPS, tm, LANES), F32)],
        compiler_params=_cparams(("parallel", "parallel")),
        name="merge_proj_ln",
    )(*os, *lses, x3d, w_bf16, ln_g, ln_b)


def _ffn_ln_kernel(x_ref, xp_ref, xn_ref, win_ref, cw_ref, cb_ref, wout_ref, g_ref, b_ref,
                   out_ref, act, *, tm):
    si = pl.program_id(1)
    n_s = pl.num_programs(1)
    x = x_ref[0]
    xb = x.astype(BF16)
    prev_ok = (si > 0).astype(F32)
    next_ok = (si < n_s - 1).astype(F32)
    xe = jnp.concatenate([xp_ref[0] * prev_ok, xn_ref[0] * next_ok], axis=0).astype(BF16)
    row = lax.broadcasted_iota(jnp.int32, (tm, 2 * FF_CHUNK), 0)
    for c in range(D_FF // FF_CHUNK):
        cols = slice(c * 2 * FF_CHUNK, (c + 1) * 2 * FF_CHUNK)
        w = win_ref[:, cols]
        h = _dot(xb, w)
        he = _dot(xe, w)
        up = jnp.where(row == 0, he[SUBLANES - 1:SUBLANES], pltpu.roll(h, 1, 0))
        dn = jnp.where(row == tm - 1, he[SUBLANES:SUBLANES + 1], pltpu.roll(h, tm - 1, 0))
        cw = cw_ref[:, cols]
        hc = up * cw[0:1] + h * cw[1:2] + dn * cw[2:3] + cb_ref[:, cols]
        a = hc[:, :FF_CHUNK]
        gate = hc[:, FF_CHUNK:]
        gelu = 0.5 * gate * (1.0 + lax.erf(gate * np.float32(math.sqrt(0.5))))
        act[:, c * FF_CHUNK:(c + 1) * FF_CHUNK] = (a * gelu).astype(BF16)
    y = ALPHA * x + _dot(act[...], wout_ref[...])
    out_ref[0] = _layer_norm(y, g_ref[...], b_ref[...])


def _chunk_interleave(w):
    lead = w.shape[:-1]
    w = w.reshape(lead + (2, D_FF // FF_CHUNK, FF_CHUNK))
    w = jnp.swapaxes(w, -3, -2)
    return w.reshape(lead + (2 * D_FF,))


def _ffn_ln(x3d, w_in, conv_w, conv_b, w_out, ln_g, ln_b, tm):
    b, s, d = x3d.shape
    nb8 = tm // SUBLANES
    n8 = s // SUBLANES
    win = _chunk_interleave(w_in).astype(BF16)
    cw = _chunk_interleave(conv_w)
    cb = _chunk_interleave(conv_b)[None]
    return pl.pallas_call(
        functools.partial(_ffn_ln_kernel, tm=tm),
        grid=(b, s // tm),
        in_specs=[pl.BlockSpec((1, tm, d), lambda bi, si: (bi, si, 0)),
                  pl.BlockSpec((1, SUBLANES, d), lambda bi, si: (bi, jnp.maximum(si * nb8 - 1, 0), 0)),
                  pl.BlockSpec((1, SUBLANES, d), lambda bi, si: (bi, jnp.minimum((si + 1) * nb8, n8 - 1), 0)),
                  _resident((d, 2 * D_FF)), _resident((3, 2 * D_FF)), _resident((1, 2 * D_FF)),
                  _resident((D_FF, d)), _resident((1, d)), _resident((1, d))],
        out_specs=pl.BlockSpec((1, tm, d), lambda bi, si: (bi, si, 0)),
        out_shape=jax.ShapeDtypeStruct((b, s, d), F32),
        scratch_shapes=[pltpu.VMEM((tm, D_FF), BF16)],
        compiler_params=_cparams(("parallel", "parallel")),
        name="ffn_ln",
    )(x3d, x3d, x3d, win, cw, cb, w_out.astype(BF16), ln_g, ln_b)


def _rope_tables(seq):
    inv = 1.0 / (ROPE_THETA ** (jnp.arange(0, QK_ROPE, 2, dtype=F32) / QK_ROPE))
    ang = jnp.arange(seq, dtype=F32)[:, None] * inv[None, :]
    cos, sin = jnp.cos(ang), jnp.sin(ang)
    half = QK_ROPE // 2
    ones = jnp.ones((seq, QK_NOPE), F32)
    z_nope = jnp.zeros((seq, QK_NOPE), F32)
    z_half = jnp.zeros((seq, half), F32)
    z_pad = jnp.zeros((seq, HEAD_PAD - QK_NOPE - QK_ROPE), F32)
    cmul = jnp.concatenate([ones, cos, cos, z_pad], axis=1)
    s_first = jnp.concatenate([z_nope, -sin, z_half, z_pad], axis=1)
    s_second = jnp.concatenate([z_nope, z_half, sin, z_pad], axis=1)
    return cmul, s_first, s_second


def _rope(t, cmul, s_first, s_second):
    half = QK_ROPE // 2
    return (t * cmul + pltpu.roll(t, HEAD_PAD - half, 1) * s_first
            + pltpu.roll(t, half, 1) * s_second)


def _rms_norm(x, g):
    ms = jnp.mean(x * x, axis=-1, keepdims=True)
    return x * lax.rsqrt(ms + RMS_EPS) * g


def _mla_proj_kernel(x_ref, wd_ref, gq_ref, gkv_ref, wuqt_ref, wuk_ref, wuvt_ref,
                     cm_ref, s1_ref, s2_ref, cmt_ref, s1t_ref, s2t_ref, qt_ref, k_ref, vt_ref):
    xb = x_ref[0].astype(BF16)
    c = _dot(xb, wd_ref[...])
    cqn = _rms_norm(c[:, :Q_LORA], gq_ref[...]).astype(BF16)
    ckvn = _rms_norm(c[:, Q_LORA:Q_LORA + KV_LORA], gkv_ref[...]).astype(BF16)
    cm, s1, s2 = cm_ref[...], s1_ref[...], s2_ref[...]
    kr = _rope(c[:, Q_LORA + KV_LORA:], cm, s1, s2)
    k = _dot(ckvn, wuk_ref[...])
    for h in range(B_HEADS):
        cols = slice(h * HEAD_PAD, (h + 1) * HEAD_PAD)
        k_ref[0, :, cols] = (k[:, cols] + kr).astype(BF16)
    qt = _dot_nt(wuqt_ref[...], cqn)
    cmt, s1t, s2t = cmt_ref[...], s1t_ref[...], s2t_ref[...]
    half = QK_ROPE // 2
    for h in range(B_HEADS):
        rows = slice(h * HEAD_PAD, (h + 1) * HEAD_PAD)
        t = qt[rows]
        rot = t * cmt + pltpu.roll(t, HEAD_PAD - half, 0) * s1t + pltpu.roll(t, half, 0) * s2t
        qt_ref[0, rows, :] = (rot * QK_SCALE_LOG2E).astype(BF16)
    vt = _dot_nt(wuvt_ref[...], ckvn).astype(BF16)
    tm = vt.shape[1]
    row = lax.broadcasted_iota(jnp.int32, (V_ROWS - V_HEAD, tm), 0)
    ones_row = jnp.where(row == 0, 1.0, 0.0).astype(BF16)
    for h in range(B_HEADS):
        vt_ref[0, h * V_ROWS:h * V_ROWS + V_HEAD, :] = vt[h * V_HEAD:(h + 1) * V_HEAD]
        vt_ref[0, h * V_ROWS + V_HEAD:(h + 1) * V_ROWS, :] = ones_row


def _mla_weights(w_dkv, w_uq, w_ukv):
    d = w_dkv.shape[0]
    pad_l = jnp.zeros((d, QK_NOPE), F32)
    pad_r = jnp.zeros((d, HEAD_PAD - QK_NOPE - QK_ROPE), F32)
    wd = jnp.concatenate([w_dkv[:, :Q_LORA + KV_LORA], pad_l, w_dkv[:, Q_LORA + KV_LORA:], pad_r], axis=1)
    wq = w_uq.reshape(Q_LORA, B_HEADS, QK_NOPE + QK_ROPE)
    wq = jnp.pad(wq, ((0, 0), (0, 0), (0, HEAD_PAD - QK_NOPE - QK_ROPE))).reshape(Q_LORA, B_HEADS * HEAD_PAD)
    wkv = w_ukv.reshape(KV_LORA, B_HEADS, QK_NOPE + V_HEAD)
    wk = jnp.pad(wkv[:, :, :QK_NOPE], ((0, 0), (0, 0), (0, HEAD_PAD - QK_NOPE))).reshape(KV_LORA, B_HEADS * HEAD_PAD)
    wvt = wkv[:, :, QK_NOPE:].reshape(KV_LORA, B_HEADS * V_HEAD).T
    return wd.astype(BF16), wq.T.astype(BF16), wk.astype(BF16), wvt.astype(BF16)


def _mla_proj(x3d, w_dkv, g_q, g_kv, w_uq, w_ukv, tm):
    b, s, d = x3d.shape
    wd, wqt, wk, wvt = _mla_weights(w_dkv, w_uq, w_ukv)
    tabs = _rope_tables(s)
    tabs_t = [t.T for t in tabs]
    hq = B_HEADS * HEAD_PAD
    hv = B_HEADS * V_ROWS
    tab = pl.BlockSpec((tm, HEAD_PAD), lambda bi, si: (si, 0))
    tab_t = pl.BlockSpec((HEAD_PAD, tm), lambda bi, si: (0, si))
    return pl.pallas_call(
        _mla_proj_kernel,
        grid=(b, s // tm),
        in_specs=[pl.BlockSpec((1, tm, d), lambda bi, si: (bi, si, 0)),
                  _resident(wd.shape), _resident((1, Q_LORA)), _resident((1, KV_LORA)),
                  _resident(wqt.shape), _resident(wk.shape), _resident(wvt.shape),
                  tab, tab, tab, tab_t, tab_t, tab_t],
        out_specs=[pl.BlockSpec((1, hq, tm), lambda bi, si: (bi, 0, si)),
                   pl.BlockSpec((1, tm, hq), lambda bi, si: (bi, si, 0)),
                   pl.BlockSpec((1, hv, tm), lambda bi, si: (bi, 0, si))],
        out_shape=[jax.ShapeDtypeStruct((b, hq, s), BF16),
                   jax.ShapeDtypeStruct((b, s, hq), BF16),
                   jax.ShapeDtypeStruct((b, hv, s), BF16)],
        compiler_params=_cparams(("parallel", "parallel")),
        name="mla_proj",
    )(x3d, wd, g_q[None], g_kv[None], wqt, wk, wvt, *tabs, *tabs_t)


def _mla_attn_kernel(qt_ref, k_ref, vt_ref, o_ref, s_scr, cmax_scr, m_all, acc_all, *, tq, tk, n_q, unroll):
    seq = k_ref.shape[1]
    nk = seq // tk
    blk_q = n_q * tq
    n_blk = seq // blk_q
    total = n_blk * nk
    trips_per_blk = nk // unroll

    def finalize(blk):
        par = blk % 2
        for j in range(n_q):
            acc = acc_all[par, j]
            q0 = pl.multiple_of(blk * blk_q + j * tq, tq)
            o_ref[0, :, pl.ds(q0, tq)] = (acc[:V_HEAD] / acc[V_HEAD:V_HEAD + 1]).astype(BF16)

    def run_step(t, t_static, do_qk=True):
        s_qk, s_sm = t_static % N_SLOTS, (t_static - 1) % N_SLOTS
        n_sm = t - 1 + nk
        par_sm, first_sm = (n_sm // nk + 1) % 2, (n_sm % nk) == 0
        v0 = pl.multiple_of((n_sm % nk) * tk, tk)
        vt = vt_ref[0, :, pl.ds(v0, tk)]
        if do_qk:
            k0 = pl.multiple_of((t % nk) * tk, tk)
            k = k_ref[0, pl.ds(k0, tk), :]
            for j in range(n_q):
                q0 = pl.multiple_of((t // nk) * blk_q + j * tq, tq)
                sc = _dot(k, qt_ref[0, :, pl.ds(q0, tq)])
                s_scr[s_qk, j] = sc
                cmax_scr[s_qk, j] = jnp.max(sc, axis=0, keepdims=True)
        for j in range(n_q):
            m = jnp.where(first_sm, NEG_INF, m_all[par_sm, j])
            m_new = jnp.maximum(m, cmax_scr[s_sm, j])
            alpha = jnp.exp2(m - m_new)
            m_all[par_sm, j] = m_new
            p = jnp.exp2(s_scr[s_sm, j] - m_new).astype(BF16)
            acc_all[par_sm, j] = alpha * acc_all[par_sm, j] + _dot(vt, p)

    s_scr[...] = jnp.zeros(s_scr.shape, F32)
    cmax_scr[...] = jnp.zeros(cmax_scr.shape, F32)
    m_all[...] = jnp.full(m_all.shape, NEG_INF, F32)
    acc_all[...] = jnp.zeros(acc_all.shape, F32)

    def trip(u, carry):
        @pl.when((u % trips_per_blk == 1) & (u > trips_per_blk))
        def _():
            finalize(u // trips_per_blk - 1)

        for i in range(unroll):
            run_step(unroll * u + i, i)
        return carry

    lax.fori_loop(0, total // unroll, trip, 0)
    run_step(total, total, do_qk=False)
    finalize(n_blk - 1)


def _mla_attn(qt, k, vt, n_q, tq, tk):
    b, s, _ = k.shape
    nk = s // tk
    unroll = max(u for u in range(N_SLOTS, MAX_UNROLL + 1, N_SLOTS) if nk % u == 0 and nk // u >= 2)
    assert s % (n_q * tq) == 0 and N_SLOTS == 2
    return pl.pallas_call(
        functools.partial(_mla_attn_kernel, tq=tq, tk=tk, n_q=n_q, unroll=unroll),
        grid=(b, B_HEADS),
        in_specs=[pl.BlockSpec((1, HEAD_PAD, s), lambda bi, h: (bi, h, 0)),
                  pl.BlockSpec((1, s, HEAD_PAD), lambda bi, h: (bi, 0, h)),
                  pl.BlockSpec((1, V_ROWS, s), lambda bi, h: (bi, h, 0))],
        out_specs=pl.BlockSpec((1, V_HEAD, s), lambda bi, h: (bi, h, 0)),
        out_shape=jax.ShapeDtypeStruct((b, B_HEADS * V_HEAD, s), BF16),
        scratch_shapes=[pltpu.VMEM((N_SLOTS, n_q, tk, tq), F32), pltpu.VMEM((N_SLOTS, n_q, 1, tq), F32),
                        pltpu.VMEM((2, n_q, 1, tq), F32), pltpu.VMEM((2, n_q, V_ROWS, tq), F32)],
        compiler_params=_cparams(("parallel", "parallel")),
        name="mla_attn",
    )(qt, k, vt)


def _proj_t_ln_kernel(ot_ref, x_ref, w_ref, g_ref, b_ref, out_ref):
    y = ALPHA * x_ref[0] + _dot_tn(ot_ref[0], w_ref[...])
    out_ref[0] = _layer_norm(y, g_ref[...], b_ref[...])


def _proj_t_ln(ot, x3d, w_bf16, ln_g, ln_b, tm):
    b, s, d = x3d.shape
    k = ot.shape[1]
    return pl.pallas_call(
        _proj_t_ln_kernel,
        grid=(b, s // tm),
        in_specs=[pl.BlockSpec((1, k, tm), lambda bi, si: (bi, 0, si)),
                  pl.BlockSpec((1, tm, d), lambda bi, si: (bi, si, 0)),
                  _resident((k, d)), _resident((1, d)), _resident((1, d))],
        out_specs=pl.BlockSpec((1, tm, d), lambda bi, si: (bi, si, 0)),
        out_shape=jax.ShapeDtypeStruct((b, s, d), F32),
        compiler_params=_cparams(("parallel", "parallel")),
        name="proj_t_ln",
    )(ot, x3d, w_bf16, ln_g, ln_b)


def _encoder(x, bias, w_qkv_a, w_o_a, w_dkv_b, g_q_b, g_kv_b, w_uq_b, w_ukv_b, w_o_b,
             ffn_w_in, ffn_conv_w, ffn_conv_b, ffn_w_out, ln_g, ln_b):
    b, s, d = x.shape
    tm = 512
    for i in range(DEPTH):
        j = i // N_MIXERS
        g1, b1 = ln_g[i, 0][None], ln_b[i, 0][None]
        g2, b2 = ln_g[i, 1][None], ln_b[i, 1][None]
        if i % N_MIXERS == 0:
            qkv = _qkv_proj(x, w_qkv_a[j].astype(BF16), tm=256)
            outs = [_dilated_attn(qkv[g], bias, g) for g in range(A_GROUPS)]
            x = _merge_proj_ln([o for o, _ in outs], [l for _, l in outs], x,
                               w_o_a[j].astype(BF16), g1, b1, tm)
        else:
            q, k, vt = _mla_proj(x, w_dkv_b[j], g_q_b[j], g_kv_b[j], w_uq_b[j], w_ukv_b[j], tm)
            ot = _mla_attn(q, k, vt, n_q=2, tq=512, tk=256)
            x = _proj_t_ln(ot, x, w_o_b[j].astype(BF16), g1, b1, tm)
        x = _ffn_ln(x, ffn_w_in[i], ffn_conv_w[i], ffn_conv_b[i], ffn_w_out[i], g2, b2, tm)
    return x


def kernel(x_prompt, x_sample, rel_bias, w_qkv_a, w_o_a, w_dkv_b, g_q_b, g_kv_b, w_uq_b, w_ukv_b, w_o_b,
           ffn_w_in, ffn_conv_w, ffn_conv_b, ffn_w_out, ln_g, ln_b):
    bias = _bias_tiles(rel_bias)
    args = (bias, w_qkv_a, w_o_a, w_dkv_b, g_q_b, g_kv_b, w_uq_b, w_ukv_b, w_o_b,
            ffn_w_in, ffn_conv_w, ffn_conv_b, ffn_w_out, ln_g, ln_b)
    return (_encoder(x_prompt, *args), _encoder(x_sample, *args))
```

```python
import functools
import math

import jax
import jax.numpy as jnp
import numpy as np
from jax import lax
from jax.experimental import pallas as pl
from jax.experimental.pallas import tpu as pltpu

F32 = jnp.float32
BF16 = jnp.bfloat16

D_MODEL = 1024
DEPTH = 2
N_MIXERS = 2

A_WINDOWS = (128, 512, 2048)
A_DILATIONS = (1, 4, 16)
A_GROUPS = 3
A_HEADS = 8
A_HEAD_DIM = 128
A_WIDTH = A_HEADS * A_HEAD_DIM
A_HALF = 64
NUM_BUCKETS = 32
MAX_DISTANCE = max(A_WINDOWS) // 2

B_HEADS = 16
Q_LORA = 384
KV_LORA = 256
QK_NOPE = 64
QK_ROPE = 32
V_HEAD = 64
ROPE_THETA = 10000.0
HEAD_PAD = 128
V_ROWS = V_HEAD + 16
N_SLOTS = 2
MAX_UNROLL = 8

D_FF = 2816
FF_CHUNK = 256

ALPHA = (2.0 * DEPTH) ** 0.25
LN_EPS = 1e-5
RMS_EPS = 1e-6
NEG_INF = -1e30

LANES = 128
SUBLANES = 8
VMEM_LIMIT = 56 * 1024 * 1024

QK_SCALE_LOG2E = np.float32((QK_NOPE + QK_ROPE) ** -0.5 * math.log2(math.e))


def _cparams(sem):
    return pltpu.CompilerParams(dimension_semantics=sem, vmem_limit_bytes=VMEM_LIMIT)


def _resident(shape):
    nd = len(shape)
    return pl.BlockSpec(shape, lambda *_: (0,) * nd, pipeline_mode=pl.Buffered(1))


def _layer_norm(y, g, b):
    mu = jnp.mean(y, axis=-1, keepdims=True)
    yc = y - mu
    var = jnp.mean(yc * yc, axis=-1, keepdims=True)
    return yc * lax.rsqrt(var + LN_EPS) * g + b


def _dot(a, b):
    return jnp.dot(a, b, preferred_element_type=F32)


def _dot_nt(a, b):
    return lax.dot_general(a, b, (((1,), (1,)), ((), ())), preferred_element_type=F32)


def _dot_tn(a, b):
    return lax.dot_general(a, b, (((0,), (0,)), ((), ())), preferred_element_type=F32)


def _qkv_proj_kernel(x_ref, w_ref, *refs, tm):
    outs, res_scr = refs[:A_GROUPS], refs[A_GROUPS]
    xb = x_ref[0].astype(BF16)
    slot = 0
    for g, dil in enumerate(A_DILATIONS):
        for c in range(3):
            n0 = (g * 3 + c) * A_WIDTH
            cols = slice(c * A_WIDTH, (c + 1) * A_WIDTH)
            res = _dot(xb, w_ref[:, n0:n0 + A_WIDTH])
            if dil == 1:
                outs[g][0, 0, :, cols] = res.astype(BF16)
                continue
            for h in range(A_HEADS):
                res_scr[slot, h] = res[:, h * LANES:(h + 1) * LANES]
            for r in range(dil):
                for h in range(A_HEADS):
                    lanes = slice(c * A_WIDTH + h * LANES, c * A_WIDTH + (h + 1) * LANES)
                    outs[g][0, r, :, lanes] = res_scr[slot, h, pl.ds(r, tm // dil, stride=dil), :].astype(BF16)
            slot = 1 - slot


def _qkv_proj(x3d, w_bf16, tm):
    b, s, d = x3d.shape
    width = 3 * A_WIDTH
    return pl.pallas_call(
        functools.partial(_qkv_proj_kernel, tm=tm),
        grid=(b, s // tm),
        in_specs=[pl.BlockSpec((1, tm, d), lambda bi, i: (bi, i, 0)), _resident(w_bf16.shape)],
        out_specs=[pl.BlockSpec((1, dil, tm // dil, width), lambda bi, i: (bi, 0, i, 0))
                   for dil in A_DILATIONS],
        out_shape=[jax.ShapeDtypeStruct((b, dil, s // dil, width), BF16) for dil in A_DILATIONS],
        scratch_shapes=[pltpu.VMEM((2, A_HEADS, tm, LANES), F32)],
        compiler_params=_cparams(("parallel", "parallel")),
        name="qkv_proj",
    )(x3d, w_bf16)


def _t5_bucket(rel):
    nb = NUM_BUCKETS // 2
    max_exact = nb // 2
    ret = jnp.where(rel > 0, nb, 0)
    n = jnp.abs(rel)
    large = max_exact + (jnp.log(jnp.maximum(n, 1).astype(F32) / max_exact)
                         / math.log(MAX_DISTANCE / max_exact) * (nb - max_exact)).astype(jnp.int32)
    large = jnp.minimum(large, nb - 1)
    return ret + jnp.where(n < max_exact, n, large)


def _band_bucket_index():
    qi = jnp.arange(2 * A_HALF)[:, None]
    kj = jnp.arange(4 * A_HALF)[None, :]
    rel = kj - A_HALF - qi
    tiles = [jnp.where(jnp.abs(rel) <= A_HALF, _t5_bucket(rel * dil), -1) for dil in A_DILATIONS]
    return jnp.stack(tiles, axis=0).astype(jnp.int32)


def _bias_tile_kernel(tab_ref, idx_ref, o_ref):
    col = pl.program_id(0) * A_HEADS + pl.program_id(1)
    idx = idx_ref[0]
    acc = jnp.full(idx.shape, NEG_INF, F32)
    for bucket in range(NUM_BUCKETS):
        acc = jnp.where(idx == bucket, tab_ref[bucket, col], acc)
    o_ref[0, 0] = acc


def _bias_tiles(rel_bias):
    idx = _band_bucket_index()
    tq, tk = idx.shape[1:]
    return pl.pallas_call(
        _bias_tile_kernel,
        grid=(A_GROUPS, A_HEADS),
        in_specs=[pl.BlockSpec(memory_space=pltpu.SMEM),
                  pl.BlockSpec((1, tq, tk), lambda g, h: (g, 0, 0))],
        out_specs=pl.BlockSpec((1, 1, tq, tk), lambda g, h: (g, h, 0, 0)),
        out_shape=jax.ShapeDtypeStruct((A_GROUPS, A_HEADS, tq, tk), F32),
        compiler_params=_cparams(("arbitrary", "arbitrary")),
        name="bias_tiles",
    )(rel_bias, idx)


def _dilated_attn_kernel(q_ref, kl_ref, km_ref, kr_ref, vl_ref, vm_ref, vr_ref, bias_ref,
                         o_ref, lse_ref, kbuf, vbuf, *, tq, sub_len):
    li = pl.program_id(2)
    kbuf[0:A_HALF] = kl_ref[0, 0]
    kbuf[A_HALF:A_HALF + tq] = km_ref[0, 0]
    kbuf[A_HALF + tq:] = kr_ref[0, 0]
    vbuf[0:A_HALF] = vl_ref[0, 0]
    vbuf[A_HALF:A_HALF + tq] = vm_ref[0, 0]
    vbuf[A_HALF + tq:] = vr_ref[0, 0]

    scale = A_HEAD_DIM ** -0.5
    qt, kt = 2 * A_HALF, 4 * A_HALF
    lane = lax.broadcasted_iota(jnp.int32, (qt, LANES), 1)
    for t in range(tq // qt):
        pos = li * tq + t * qt - A_HALF + lax.broadcasted_iota(jnp.int32, (1, kt), 1)
        edge = jnp.where((pos >= 0) & (pos < sub_len), 0.0, NEG_INF).astype(F32)
        lse_tile = jnp.zeros((qt, LANES), F32)
        for h in range(A_HEADS):
            cols = slice(h * A_HEAD_DIM, (h + 1) * A_HEAD_DIM)
            q = q_ref[0, 0, t * qt:(t + 1) * qt, cols]
            k = kbuf[t * qt:t * qt + kt, cols]
            v = vbuf[t * qt:t * qt + kt, cols]
            s = _dot_nt(q, k) * scale + bias_ref[0, h] + edge
            m = jnp.max(s, axis=-1, keepdims=True)
            p = jnp.exp(s - m)
            den = jnp.sum(p, axis=-1, keepdims=True)
            o = _dot(p.astype(BF16), v) / den
            o_ref[0, 0, t * qt:(t + 1) * qt, cols] = o.astype(BF16)
            lse_tile = jnp.where(lane == h, m + jnp.log(den), lse_tile)
        lse_ref[0, 0, t * qt:(t + 1) * qt, :] = lse_tile


def _dilated_attn(qkv_g, bias, g):
    batch, dil, sub_len, _ = qkv_g.shape
    tq = min(1024, sub_len)
    n_t = sub_len // tq
    hb = tq // A_HALF
    n_hb = sub_len // A_HALF

    main = lambda which: pl.BlockSpec((1, 1, tq, A_WIDTH), lambda b, r, i: (b, r, i, which))
    left = lambda which: pl.BlockSpec(
        (1, 1, A_HALF, A_WIDTH), lambda b, r, i: (b, r, jnp.maximum(i * hb - 1, 0), which))
    right = lambda which: pl.BlockSpec(
        (1, 1, A_HALF, A_WIDTH), lambda b, r, i: (b, r, jnp.minimum((i + 1) * hb, n_hb - 1), which))

    return pl.pallas_call(
        functools.partial(_dilated_attn_kernel, tq=tq, sub_len=sub_len),
        grid=(batch, dil, n_t),
        in_specs=[main(0), left(1), main(1), right(1), left(2), main(2), right(2),
                  pl.BlockSpec((1, A_HEADS, 2 * A_HALF, 4 * A_HALF), lambda b, r, i: (g, 0, 0, 0))],
        out_specs=[pl.BlockSpec((1, 1, tq, A_WIDTH), lambda b, r, i: (b, r, i, 0)),
                   pl.BlockSpec((1, 1, tq, LANES), lambda b, r, i: (b, r, i, 0))],
        out_shape=[jax.ShapeDtypeStruct((batch, dil, sub_len, A_WIDTH), BF16),
                   jax.ShapeDtypeStruct((batch, dil, sub_len, LANES), F32)],
        scratch_shapes=[pltpu.VMEM((tq + 2 * A_HALF, A_WIDTH), BF16),
                        pltpu.VMEM((tq + 2 * A_HALF, A_WIDTH), BF16)],
        compiler_params=_cparams(("parallel", "parallel", "parallel")),
        name=f"dilated_attn_g{g}",
    )(qkv_g, qkv_g, qkv_g, qkv_g, qkv_g, qkv_g, qkv_g, bias)


def _merge_proj_ln_kernel(o0_ref, o1_ref, o2_ref, l0_ref, l1_ref, l2_ref, x_ref, w_ref, g_ref, b_ref,
                          out_ref, merged, o_scr, l_scr, *, tm):
    for g, (o_ref, l_ref, dil) in enumerate(zip((o0_ref, o1_ref, o2_ref), (l0_ref, l1_ref, l2_ref),
                                                A_DILATIONS)):
        for r in range(dil):
            rows = pl.ds(r, tm // dil, stride=dil) if dil > 1 else slice(None)
            l_scr[g, rows, :] = l_ref[0, r]
            for h in range(A_HEADS):
                o_scr[g, h, rows, :] = o_ref[0, r, :, h * A_HEAD_DIM:(h + 1) * A_HEAD_DIM].astype(F32)
    l0, l1, l2 = l_scr[0], l_scr[1], l_scr[2]
    m = jnp.maximum(jnp.maximum(l0, l1), l2)
    e0, e1, e2 = jnp.exp(l0 - m), jnp.exp(l1 - m), jnp.exp(l2 - m)
    den = e0 + e1 + e2
    w0, w1, w2 = e0 / den, e1 / den, e2 / den
    for h in range(A_HEADS):
        cols = slice(h * A_HEAD_DIM, (h + 1) * A_HEAD_DIM)
        mix = (w0[:, h:h + 1] * o_scr[0, h] + w1[:, h:h + 1] * o_scr[1, h]
               + w2[:, h:h + 1] * o_scr[2, h])
        merged[:, cols] = mix.astype(BF16)
    y = ALPHA * x_ref[0] + _dot(merged[...], w_ref[...])
    out_ref[0] = _layer_norm(y, g_ref[...], b_ref[...])


def _merge_proj_ln(os, lses, x3d, w_bf16, ln_g, ln_b, tm):
    b, s, d = x3d.shape
    plane = lambda w: [pl.BlockSpec((1, dil, tm // dil, w), lambda bi, i: (bi, 0, i, 0))
                       for dil in A_DILATIONS]
    row = pl.BlockSpec((1, tm, d), lambda bi, i: (bi, i, 0))
    return pl.pallas_call(
        functools.partial(_merge_proj_ln_kernel, tm=tm),
        grid=(b, s // tm),
        in_specs=plane(A_WIDTH) + plane(LANES) + [row, _resident((A_WIDTH, d)),
                                                  _resident((1, d)), _resident((1, d))],
        out_specs=row,
        out_shape=jax.ShapeDtypeStruct((b, s, d), F32),
        scratch_shapes=[pltpu.VMEM((tm, A_WIDTH), BF16),
                        pltpu.VMEM((A_GROUPS, A_HEADS, tm, A_HEAD_DIM), F32),
                        pltpu.VMEM((A_GROUPS, tm, LANES), F32)],
        compiler_params=_cparams(("parallel", "parallel")),
        name="merge_proj_ln",
    )(*os, *lses, x3d, w_bf16, ln_g, ln_b)


def _ffn_ln_kernel(x_ref, xp_ref, xn_ref, win_ref, cw_ref, cb_ref, wout_ref, g_ref, b_ref,
                   out_ref, act, *, tm):
    si = pl.program_id(1)
    n_s = pl.num_programs(1)
    x = x_ref[0]
    xb = x.astype(BF16)
    prev_ok = (si > 0).astype(F32)
    next_ok = (si < n_s - 1).astype(F32)
    xe = jnp.concatenate([xp_ref[0] * prev_ok, xn_ref[0] * next_ok], axis=0).astype(BF16)
    row = lax.broadcasted_iota(jnp.int32, (tm, 2 * FF_CHUNK), 0)
    for c in range(D_FF // FF_CHUNK):
        cols = slice(c * 2 * FF_CHUNK, (c + 1) * 2 * FF_CHUNK)
        w = win_ref[:, cols]
        h = _dot(xb, w)
        he = _dot(xe, w)
        up = jnp.where(row == 0, he[SUBLANES - 1:SUBLANES], pltpu.roll(h, 1, 0))
        dn = jnp.where(row == tm - 1, he[SUBLANES:SUBLANES + 1], pltpu.roll(h, tm - 1, 0))
        cw = cw_ref[:, cols]
        hc = up * cw[0:1] + h * cw[1:2] + dn * cw[2:3] + cb_ref[:, cols]
        a = hc[:, :FF_CHUNK]
        gate = hc[:, FF_CHUNK:]
        gelu = 0.5 * gate * (1.0 + lax.erf(gate * np.float32(math.sqrt(0.5))))
        act[:, c * FF_CHUNK:(c + 1) * FF_CHUNK] = (a * gelu).astype(BF16)
    y = ALPHA * x + _dot(act[...], wout_ref[...])
    out_ref[0] = _layer_norm(y, g_ref[...], b_ref[...])


def _chunk_interleave(w):
    lead = w.shape[:-1]
    w = w.reshape(lead + (2, D_FF // FF_CHUNK, FF_CHUNK))
    w = jnp.swapaxes(w, -3, -2)
    return w.reshape(lead + (2 * D_FF,))


def _ffn_ln(x3d, w_in, conv_w, conv_b, w_out, ln_g, ln_b, tm):
    b, s, d = x3d.shape
    nb8 = tm // SUBLANES
    n8 = s // SUBLANES
    win = _chunk_interleave(w_in).astype(BF16)
    cw = _chunk_interleave(conv_w)
    cb = _chunk_interleave(conv_b)[None]
    return pl.pallas_call(
        functools.partial(_ffn_ln_kernel, tm=tm),
        grid=(b, s // tm),
        in_specs=[pl.BlockSpec((1, tm, d), lambda bi, si: (bi, si, 0)),
                  pl.BlockSpec((1, SUBLANES, d), lambda bi, si: (bi, jnp.maximum(si * nb8 - 1, 0), 0)),
                  pl.BlockSpec((1, SUBLANES, d), lambda bi, si: (bi, jnp.minimum((si + 1) * nb8, n8 - 1), 0)),
                  _resident((d, 2 * D_FF)), _resident((3, 2 * D_FF)), _resident((1, 2 * D_FF)),
                  _resident((D_FF, d)), _resident((1, d)), _resident((1, d))],
        out_specs=pl.BlockSpec((1, tm, d), lambda bi, si: (bi, si, 0)),
        out_shape=jax.ShapeDtypeStruct((b, s, d), F32),
        scratch_shapes=[pltpu.VMEM((tm, D_FF), BF16)],
        compiler_params=_cparams(("parallel", "parallel")),
        name="ffn_ln",
    )(x3d, x3d, x3d, win, cw, cb, w_out.astype(BF16), ln_g, ln_b)


def _rope_tables(seq):
    inv = 1.0 / (ROPE_THETA ** (jnp.arange(0, QK_ROPE, 2, dtype=F32) / QK_ROPE))
    ang = jnp.arange(seq, dtype=F32)[:, None] * inv[None, :]
    cos, sin = jnp.cos(ang), jnp.sin(ang)
    half = QK_ROPE // 2
    ones = jnp.ones((seq, QK_NOPE), F32)
    z_nope = jnp.zeros((seq, QK_NOPE), F32)
    z_half = jnp.zeros((seq, half), F32)
    z_pad = jnp.zeros((seq, HEAD_PAD - QK_NOPE - QK_ROPE), F32)
    cmul = jnp.concatenate([ones, cos, cos, z_pad], axis=1)
    s_first = jnp.concatenate([z_nope, -sin, z_half, z_pad], axis=1)
    s_second = jnp.concatenate([z_nope, z_half, sin, z_pad], axis=1)
    return cmul, s_first, s_second


def _rope(t, cmul, s_first, s_second):
    half = QK_ROPE // 2
    return (t * cmul + pltpu.roll(t, HEAD_PAD - half, 1) * s_first
            + pltpu.roll(t, half, 1) * s_second)


def _rms_norm(x, g):
    ms = jnp.mean(x * x, axis=-1, keepdims=True)
    return x * lax.rsqrt(ms + RMS_EPS) * g


def _mla_proj_kernel(x_ref, wd_ref, gq_ref, gkv_ref, wuqt_ref, wuk_ref, wuvt_ref,
                     cm_ref, s1_ref, s2_ref, cmt_ref, s1t_ref, s2t_ref, qt_ref, k_ref, vt_ref):
    xb = x_ref[0].astype(BF16)
    c = _dot(xb, wd_ref[...])
    cqn = _rms_norm(c[:, :Q_LORA], gq_ref[...]).astype(BF16)
    ckvn = _rms_norm(c[:, Q_LORA:Q_LORA + KV_LORA], gkv_ref[...]).astype(BF16)
    cm, s1, s2 = cm_ref[...], s1_ref[...], s2_ref[...]
    kr = _rope(c[:, Q_LORA + KV_LORA:], cm, s1, s2)
    k = _dot(ckvn, wuk_ref[...])
    for h in range(B_HEADS):
        cols = slice(h * HEAD_PAD, (h + 1) * HEAD_PAD)
        k_ref[0, :, cols] = (k[:, cols] + kr).astype(BF16)
    qt = _dot_nt(wuqt_ref[...], cqn)
    cmt, s1t, s2t = cmt_ref[...], s1t_ref[...], s2t_ref[...]
    half = QK_ROPE // 2
    for h in range(B_HEADS):
        rows = slice(h * HEAD_PAD, (h + 1) * HEAD_PAD)
        t = qt[rows]
        rot = t * cmt + pltpu.roll(t, HEAD_PAD - half, 0) * s1t + pltpu.roll(t, half, 0) * s2t
        qt_ref[0, rows, :] = (rot * QK_SCALE_LOG2E).astype(BF16)
    vt = _dot_nt(wuvt_ref[...], ckvn).astype(BF16)
    tm = vt.shape[1]
    row = lax.broadcasted_iota(jnp.int32, (V_ROWS - V_HEAD, tm), 0)
    ones_row = jnp.where(row == 0, 1.0, 0.0).astype(BF16)
    for h in range(B_HEADS):
        vt_ref[0, h * V_ROWS:h * V_ROWS + V_HEAD, :] = vt[h * V_HEAD:(h + 1) * V_HEAD]
        vt_ref[0, h * V_ROWS + V_HEAD:(h + 1) * V_ROWS, :] = ones_row


def _mla_weights(w_dkv, w_uq, w_ukv):
    d = w_dkv.shape[0]
    pad_l = jnp.zeros((d, QK_NOPE), F32)
    pad_r = jnp.zeros((d, HEAD_PAD - QK_NOPE - QK_ROPE), F32)
    wd = jnp.concatenate([w_dkv[:, :Q_LORA + KV_LORA], pad_l, w_dkv[:, Q_LORA + KV_LORA:], pad_r], axis=1)
    wq = w_uq.reshape(Q_LORA, B_HEADS, QK_NOPE + QK_ROPE)
    wq = jnp.pad(wq, ((0, 0), (0, 0), (0, HEAD_PAD - QK_NOPE - QK_ROPE))).reshape(Q_LORA, B_HEADS * HEAD_PAD)
    wkv = w_ukv.reshape(KV_LORA, B_HEADS, QK_NOPE + V_HEAD)
    wk = jnp.pad(wkv[:, :, :QK_NOPE], ((0, 0), (0, 0), (0, HEAD_PAD - QK_NOPE))).reshape(KV_LORA, B_HEADS * HEAD_PAD)
    wvt = wkv[:, :, QK_NOPE:].reshape(KV_LORA, B_HEADS * V_HEAD).T
    return wd.astype(BF16), wq.T.astype(BF16), wk.astype(BF16), wvt.astype(BF16)


def _mla_proj(x3d, w_dkv, g_q, g_kv, w_uq, w_ukv, tm):
    b, s, d = x3d.shape
    wd, wqt, wk, wvt = _mla_weights(w_dkv, w_uq, w_ukv)
    tabs = _rope_tables(s)
    tabs_t = [t.T for t in tabs]
    hq = B_HEADS * HEAD_PAD
    hv = B_HEADS * V_ROWS
    tab = pl.BlockSpec((tm, HEAD_PAD), lambda bi, si: (si, 0))
    tab_t = pl.BlockSpec((HEAD_PAD, tm), lambda bi, si: (0, si))
    return pl.pallas_call(
        _mla_proj_kernel,
        grid=(b, s // tm),
        in_specs=[pl.BlockSpec((1, tm, d), lambda bi, si: (bi, si, 0)),
                  _resident(wd.shape), _resident((1, Q_LORA)), _resident((1, KV_LORA)),
                  _resident(wqt.shape), _resident(wk.shape), _resident(wvt.shape),
                  tab, tab, tab, tab_t, tab_t, tab_t],
        out_specs=[pl.BlockSpec((1, hq, tm), lambda bi, si: (bi, 0, si)),
                   pl.BlockSpec((1, tm, hq), lambda bi, si: (bi, si, 0)),
                   pl.BlockSpec((1, hv, tm), lambda bi, si: (bi, 0, si))],
        out_shape=[jax.ShapeDtypeStruct((b, hq, s), BF16),
                   jax.ShapeDtypeStruct((b, s, hq), BF16),
                   jax.ShapeDtypeStruct((b, hv, s), BF16)],
        compiler_params=_cparams(("parallel", "parallel")),
        name="mla_proj",
    )(x3d, wd, g_q[None], g_kv[None], wqt, wk, wvt, *tabs, *tabs_t)


def _mla_attn_kernel(qt_ref, k_ref, vt_ref, o_ref, s_scr, cmax_scr, m_all, acc_all, *, tq, tk, n_q, unroll):
    seq = k_ref.shape[1]
    nk = seq // tk
    blk_q = n_q * tq
    n_blk = seq // blk_q
    total = n_blk * nk
    trips_per_blk = nk // unroll

    def finalize(blk):
        par = blk % 2
        for j in range(n_q):
            acc = acc_all[par, j]
            q0 = pl.multiple_of(blk * blk_q + j * tq, tq)
            o_ref[0, :, pl.ds(q0, tq)] = (acc[:V_HEAD] / acc[V_HEAD:V_HEAD + 1]).astype(BF16)

    def run_step(t, t_static, do_qk=True):
        s_qk, s_sm = t_static % N_SLOTS, (t_static - 1) % N_SLOTS
        n_sm = t - 1 + nk
        par_sm, first_sm = (n_sm // nk + 1) % 2, (n_sm % nk) == 0
        v0 = pl.multiple_of((n_sm % nk) * tk, tk)
        vt = vt_ref[0, :, pl.ds(v0, tk)]
        if do_qk:
            k0 = pl.multiple_of((t % nk) * tk, tk)
            k = k_ref[0, pl.ds(k0, tk), :]
            for j in range(n_q):
                q0 = pl.multiple_of((t // nk) * blk_q + j * tq, tq)
                sc = _dot(k, qt_ref[0, :, pl.ds(q0, tq)])
                s_scr[s_qk, j] = sc
                cmax_scr[s_qk, j] = jnp.max(sc, axis=0, keepdims=True)
        for j in range(n_q):
            m = jnp.where(first_sm, NEG_INF, m_all[par_sm, j])
            m_new = jnp.maximum(m, cmax_scr[s_sm, j])
            alpha = jnp.exp2(m - m_new)
            m_all[par_sm, j] = m_new
            p = jnp.exp2(s_scr[s_sm, j] - m_new).astype(BF16)
            acc_all[par_sm, j] = alpha * acc_all[par_sm, j] + _dot(vt, p)

    s_scr[...] = jnp.zeros(s_scr.shape, F32)
    cmax_scr[...] = jnp.zeros(cmax_scr.shape, F32)
    m_all[...] = jnp.full(m_all.shape, NEG_INF, F32)
    acc_all[...] = jnp.zeros(acc_all.shape, F32)

    def trip(u, carry):
        @pl.when((u % trips_per_blk == 1) & (u > trips_per_blk))
        def _():
            finalize(u // trips_per_blk - 1)

        for i in range(unroll):
            run_step(unroll * u + i, i)
        return carry

    lax.fori_loop(0, total // unroll, trip, 0)
    run_step(total, total, do_qk=False)
    finalize(n_blk - 1)


def _mla_attn(qt, k, vt, n_q, tq, tk):
    b, s, _ = k.shape
    nk = s // tk
    unroll = max(u for u in range(N_SLOTS, MAX_UNROLL + 1, N_SLOTS) if nk % u == 0 and nk // u >= 2)
    assert s % (n_q * tq) == 0 and N_SLOTS == 2
    return pl.pallas_call(
        functools.partial(_mla_attn_kernel, tq=tq, tk=tk, n_q=n_q, unroll=unroll),
        grid=(b, B_HEADS),
        in_specs=[pl.BlockSpec((1, HEAD_PAD, s), lambda bi, h: (bi, h, 0)),
                  pl.BlockSpec((1, s, HEAD_PAD), lambda bi, h: (bi, 0, h)),
                  pl.BlockSpec((1, V_ROWS, s), lambda bi, h: (bi, h, 0))],
        out_specs=pl.BlockSpec((1, V_HEAD, s), lambda bi, h: (bi, h, 0)),
        out_shape=jax.ShapeDtypeStruct((b, B_HEADS * V_HEAD, s), BF16),
        scratch_shapes=[pltpu.VMEM((N_SLOTS, n_q, tk, tq), F32), pltpu.VMEM((N_SLOTS, n_q, 1, tq), F32),
                        pltpu.VMEM((2, n_q, 1, tq), F32), pltpu.VMEM((2, n_q, V_ROWS, tq), F32)],
        compiler_params=_cparams(("parallel", "parallel")),
        name="mla_attn",
    )(qt, k, vt)


def _proj_t_ln_kernel(ot_ref, x_ref, w_ref, g_ref, b_ref, out_ref):
    y = ALPHA * x_ref[0] + _dot_tn(ot_ref[0], w_ref[...])
    out_ref[0] = _layer_norm(y, g_ref[...], b_ref[...])


def _proj_t_ln(ot, x3d, w_bf16, ln_g, ln_b, tm):
    b, s, d = x3d.shape
    k = ot.shape[1]
    return pl.pallas_call(
        _proj_t_ln_kernel,
        grid=(b, s // tm),
        in_specs=[pl.BlockSpec((1, k, tm), lambda bi, si: (bi, 0, si)),
                  pl.BlockSpec((1, tm, d), lambda bi, si: (bi, si, 0)),
                  _resident((k, d)), _resident((1, d)), _resident((1, d))],
        out_specs=pl.BlockSpec((1, tm, d), lambda bi, si: (bi, si, 0)),
        out_shape=jax.ShapeDtypeStruct((b, s, d), F32),
        compiler_params=_cparams(("parallel", "parallel")),
        name="proj_t_ln",
    )(ot, x3d, w_bf16, ln_g, ln_b)


def _encoder(x, bias, w_qkv_a, w_o_a, w_dkv_b, g_q_b, g_kv_b, w_uq_b, w_ukv_b, w_o_b,
             ffn_w_in, ffn_conv_w, ffn_conv_b, ffn_w_out, ln_g, ln_b):
    b, s, d = x.shape
    tm = 512
    for i in range(DEPTH):
        j = i // N_MIXERS
        g1, b1 = ln_g[i, 0][None], ln_b[i, 0][None]
        g2, b2 = ln_g[i, 1][None], ln_b[i, 1][None]
        if i % N_MIXERS == 0:
            qkv = _qkv_proj(x, w_qkv_a[j].astype(BF16), tm=256)
            outs = [_dilated_attn(qkv[g], bias, g) for g in range(A_GROUPS)]
            x = _merge_proj_ln([o for o, _ in outs], [l for _, l in outs], x,
                               w_o_a[j].astype(BF16), g1, b1, tm)
        else:
            q, k, vt = _mla_proj(x, w_dkv_b[j], g_q_b[j], g_kv_b[j], w_uq_b[j], w_ukv_b[j], tm)
            ot = _mla_attn(q, k, vt, n_q=2, tq=512, tk=512)
            x = _proj_t_ln(ot, x, w_o_b[j].astype(BF16), g1, b1, tm)
        x = _ffn_ln(x, ffn_w_in[i], ffn_conv_w[i], ffn_conv_b[i], ffn_w_out[i], g2, b2, tm)
    return x


def kernel(x_prompt, x_sample, rel_bias, w_qkv_a, w_o_a, w_dkv_b, g_q_b, g_kv_b, w_uq_b, w_ukv_b, w_o_b,
           ffn_w_in, ffn_conv_w, ffn_conv_b, ffn_w_out, ln_g, ln_b):
    bias = _bias_tiles(rel_bias)
    args = (bias, w_qkv_a, w_o_a, w_dkv_b, g_q_b, g_kv_b, w_uq_b, w_ukv_b, w_o_b,
            ffn_w_in, ffn_conv_w, ffn_conv_b, ffn_w_out, ln_g, ln_b)
    return (_encoder(x_prompt, *args), _encoder(x_sample, *args))
```

```python
import functools
import math

import jax
import jax.numpy as jnp
import numpy as np
from jax import lax
from jax.experimental import pallas as pl
from jax.experimental.pallas import tpu as pltpu

F32 = jnp.float32
BF16 = jnp.bfloat16

D_MODEL = 1024
DEPTH = 2
N_MIXERS = 2

A_WINDOWS = (128, 512, 2048)
A_DILATIONS = (1, 4, 16)
A_GROUPS = 3
A_HEADS = 8
A_HEAD_DIM = 128
A_WIDTH = A_HEADS * A_HEAD_DIM
A_HALF = 64
NUM_BUCKETS = 32
MAX_DISTANCE = max(A_WINDOWS) // 2

B_HEADS = 16
Q_LORA = 384
KV_LORA = 256
QK_NOPE = 64
QK_ROPE = 32
V_HEAD = 64
ROPE_THETA = 10000.0
HEAD_PAD = 128
V_ROWS = V_HEAD + 16
N_SLOTS = 2
MAX_UNROLL = 8

D_FF = 2816
FF_CHUNK = 256

ALPHA = (2.0 * DEPTH) ** 0.25
LN_EPS = 1e-5
RMS_EPS = 1e-6
NEG_INF = -1e30

LANES = 128
SUBLANES = 8
VMEM_LIMIT = 56 * 1024 * 1024

QK_SCALE_LOG2E = np.float32((QK_NOPE + QK_ROPE) ** -0.5 * math.log2(math.e))


def _cparams(sem):
    return pltpu.CompilerParams(dimension_semantics=sem, vmem_limit_bytes=VMEM_LIMIT)


def _resident(shape):
    nd = len(shape)
    return pl.BlockSpec(shape, lambda *_: (0,) * nd, pipeline_mode=pl.Buffered(1))


def _layer_norm(y, g, b):
    mu = jnp.mean(y, axis=-1, keepdims=True)
    yc = y - mu
    var = jnp.mean(yc * yc, axis=-1, keepdims=True)
    return yc * lax.rsqrt(var + LN_EPS) * g + b


def _dot(a, b):
    return jnp.dot(a, b, preferred_element_type=F32)


def _dot_nt(a, b):
    return lax.dot_general(a, b, (((1,), (1,)), ((), ())), preferred_element_type=F32)


def _dot_tn(a, b):
    return lax.dot_general(a, b, (((0,), (0,)), ((), ())), preferred_element_type=F32)


def _qkv_proj_kernel(x_ref, w_ref, *refs, tm):
    outs, res_scr = refs[:A_GROUPS], refs[A_GROUPS]
    xb = x_ref[0].astype(BF16)
    slot = 0
    for g, dil in enumerate(A_DILATIONS):
        for c in range(3):
            n0 = (g * 3 + c) * A_WIDTH
            cols = slice(c * A_WIDTH, (c + 1) * A_WIDTH)
            res = _dot(xb, w_ref[:, n0:n0 + A_WIDTH])
            if dil == 1:
                outs[g][0, 0, :, cols] = res.astype(BF16)
                continue
            for h in range(A_HEADS):
                res_scr[slot, h] = res[:, h * LANES:(h + 1) * LANES]
            for r in range(dil):
                for h in range(A_HEADS):
                    lanes = slice(c * A_WIDTH + h * LANES, c * A_WIDTH + (h + 1) * LANES)
                    outs[g][0, r, :, lanes] = res_scr[slot, h, pl.ds(r, tm // dil, stride=dil), :].astype(BF16)
            slot = 1 - slot


def _qkv_proj(x3d, w_bf16, tm):
    b, s, d = x3d.shape
    width = 3 * A_WIDTH
    return pl.pallas_call(
        functools.partial(_qkv_proj_kernel, tm=tm),
        grid=(b, s // tm),
        in_specs=[pl.BlockSpec((1, tm, d), lambda bi, i: (bi, i, 0)), _resident(w_bf16.shape)],
        out_specs=[pl.BlockSpec((1, dil, tm // dil, width), lambda bi, i: (bi, 0, i, 0))
                   for dil in A_DILATIONS],
        out_shape=[jax.ShapeDtypeStruct((b, dil, s // dil, width), BF16) for dil in A_DILATIONS],
        scratch_shapes=[pltpu.VMEM((2, A_HEADS, tm, LANES), F32)],
        compiler_params=_cparams(("parallel", "parallel")),
        name="qkv_proj",
    )(x3d, w_bf16)


def _t5_bucket(rel):
    nb = NUM_BUCKETS // 2
    max_exact = nb // 2
    ret = jnp.where(rel > 0, nb, 0)
    n = jnp.abs(rel)
    large = max_exact + (jnp.log(jnp.maximum(n, 1).astype(F32) / max_exact)
                         / math.log(MAX_DISTANCE / max_exact) * (nb - max_exact)).astype(jnp.int32)
    large = jnp.minimum(large, nb - 1)
    return ret + jnp.where(n < max_exact, n, large)


def _band_bucket_index():
    qi = jnp.arange(2 * A_HALF)[:, None]
    kj = jnp.arange(4 * A_HALF)[None, :]
    rel = kj - A_HALF - qi
    tiles = [jnp.where(jnp.abs(rel) <= A_HALF, _t5_bucket(rel * dil), -1) for dil in A_DILATIONS]
    return jnp.stack(tiles, axis=0).astype(jnp.int32)


def _bias_tile_kernel(tab_ref, idx_ref, o_ref):
    col = pl.program_id(0) * A_HEADS + pl.program_id(1)
    idx = idx_ref[0]
    acc = jnp.full(idx.shape, NEG_INF, F32)
    for bucket in range(NUM_BUCKETS):
        acc = jnp.where(idx == bucket, tab_ref[bucket, col], acc)
    o_ref[0, 0] = acc


def _bias_tiles(rel_bias):
    idx = _band_bucket_index()
    tq, tk = idx.shape[1:]
    return pl.pallas_call(
        _bias_tile_kernel,
        grid=(A_GROUPS, A_HEADS),
        in_specs=[pl.BlockSpec(memory_space=pltpu.SMEM),
                  pl.BlockSpec((1, tq, tk), lambda g, h: (g, 0, 0))],
        out_specs=pl.BlockSpec((1, 1, tq, tk), lambda g, h: (g, h, 0, 0)),
        out_shape=jax.ShapeDtypeStruct((A_GROUPS, A_HEADS, tq, tk), F32),
        compiler_params=_cparams(("arbitrary", "arbitrary")),
        name="bias_tiles",
    )(rel_bias, idx)


def _dilated_attn_kernel(q_ref, kl_ref, km_ref, kr_ref, vl_ref, vm_ref, vr_ref, bias_ref,
                         o_ref, lse_ref, kbuf, vbuf, *, tq, sub_len):
    li = pl.program_id(2)
    kbuf[0:A_HALF] = kl_ref[0, 0]
    kbuf[A_HALF:A_HALF + tq] = km_ref[0, 0]
    kbuf[A_HALF + tq:] = kr_ref[0, 0]
    vbuf[0:A_HALF] = vl_ref[0, 0]
    vbuf[A_HALF:A_HALF + tq] = vm_ref[0, 0]
    vbuf[A_HALF + tq:] = vr_ref[0, 0]

    scale = A_HEAD_DIM ** -0.5
    qt, kt = 2 * A_HALF, 4 * A_HALF
    lane = lax.broadcasted_iota(jnp.int32, (qt, LANES), 1)
    for t in range(tq // qt):
        pos = li * tq + t * qt - A_HALF + lax.broadcasted_iota(jnp.int32, (1, kt), 1)
        edge = jnp.where((pos >= 0) & (pos < sub_len), 0.0, NEG_INF).astype(F32)
        lse_tile = jnp.zeros((qt, LANES), F32)
        for h in range(A_HEADS):
            cols = slice(h * A_HEAD_DIM, (h + 1) * A_HEAD_DIM)
            q = q_ref[0, 0, t * qt:(t + 1) * qt, cols]
            k = kbuf[t * qt:t * qt + kt, cols]
            v = vbuf[t * qt:t * qt + kt, cols]
            s = _dot_nt(q, k) * scale + bias_ref[0, h] + edge
            m = jnp.max(s, axis=-1, keepdims=True)
            p = jnp.exp(s - m)
            den = jnp.sum(p, axis=-1, keepdims=True)
            o = _dot(p.astype(BF16), v) / den
            o_ref[0, 0, t * qt:(t + 1) * qt, cols] = o.astype(BF16)
            lse_tile = jnp.where(lane == h, m + jnp.log(den), lse_tile)
        lse_ref[0, 0, t * qt:(t + 1) * qt, :] = lse_tile


def _dilated_attn(qkv_g, bias, g):
    batch, dil, sub_len, _ = qkv_g.shape
    tq = min(1024, sub_len)
    n_t = sub_len // tq
    hb = tq // A_HALF
    n_hb = sub_len // A_HALF

    main = lambda which: pl.BlockSpec((1, 1, tq, A_WIDTH), lambda b, r, i: (b, r, i, which))
    left = lambda which: pl.BlockSpec(
        (1, 1, A_HALF, A_WIDTH), lambda b, r, i: (b, r, jnp.maximum(i * hb - 1, 0), which))
    right = lambda which: pl.BlockSpec(
        (1, 1, A_HALF, A_WIDTH), lambda b, r, i: (b, r, jnp.minimum((i + 1) * hb, n_hb - 1), which))

    return pl.pallas_call(
        functools.partial(_dilated_attn_kernel, tq=tq, sub_len=sub_len),
        grid=(batch, dil, n_t),
        in_specs=[main(0), left(1), main(1), right(1), left(2), main(2), right(2),
                  pl.BlockSpec((1, A_HEADS, 2 * A_HALF, 4 * A_HALF), lambda b, r, i: (g, 0, 0, 0))],
        out_specs=[pl.BlockSpec((1, 1, tq, A_WIDTH), lambda b, r, i: (b, r, i, 0)),
                   pl.BlockSpec((1, 1, tq, LANES), lambda b, r, i: (b, r, i, 0))],
        out_shape=[jax.ShapeDtypeStruct((batch, dil, sub_len, A_WIDTH), BF16),
                   jax.ShapeDtypeStruct((batch, dil, sub_len, LANES), F32)],
        scratch_shapes=[pltpu.VMEM((tq + 2 * A_HALF, A_WIDTH), BF16),
                        pltpu.VMEM((tq + 2 * A_HALF, A_WIDTH), BF16)],
        compiler_params=_cparams(("parallel", "parallel", "parallel")),
        name=f"dilated_attn_g{g}",
    )(qkv_g, qkv_g, qkv_g, qkv_g, qkv_g, qkv_g, qkv_g, bias)


def _merge_proj_ln_kernel(o0_ref, o1_ref, o2_ref, l0_ref, l1_ref, l2_ref, x_ref, w_ref, g_ref, b_ref,
                          out_ref, merged, o_scr, l_scr, *, tm):
    for g, (o_ref, l_ref, dil) in enumerate(zip((o0_ref, o1_ref, o2_ref), (l0_ref, l1_ref, l2_ref),
                                                A_DILATIONS)):
        for r in range(dil):
            rows = pl.ds(r, tm // dil, stride=dil) if dil > 1 else slice(None)
            l_scr[g, rows, :] = l_ref[0, r]
            for h in range(A_HEADS):
                o_scr[g, h, rows, :] = o_ref[0, r, :, h * A_HEAD_DIM:(h + 1) * A_HEAD_DIM].astype(F32)
    l0, l1, l2 = l_scr[0], l_scr[1], l_scr[2]
    m = jnp.maximum(jnp.maximum(l0, l1), l2)
    e0, e1, e2 = jnp.exp(l0 - m), jnp.exp(l1 - m), jnp.exp(l2 - m)
    den = e0 + e1 + e2
    w0, w1, w2 = e0 / den, e1 / den, e2 / den
    for h in range(A_HEADS):
        cols = slice(h * A_HEAD_DIM, (h + 1) * A_HEAD_DIM)
        mix = (w0[:, h:h + 1] * o_scr[0, h] + w1[:, h:h + 1] * o_scr[1, h]
               + w2[:, h:h + 1] * o_scr[2, h])
        merged[:, cols] = mix.astype(BF16)
    y = ALPHA * x_ref[0] + _dot(merged[...], w_ref[...])
    out_ref[0] = _layer_norm(y, g_ref[...], b_ref[...])


def _merge_proj_ln(os, lses, x3d, w_bf16, ln_g, ln_b, tm):
    b, s, d = x3d.shape
    plane = lambda w: [pl.BlockSpec((1, dil, tm // dil, w), lambda bi, i: (bi, 0, i, 0))
                       for dil in A_DILATIONS]
    row = pl.BlockSpec((1, tm, d), lambda bi, i: (bi, i, 0))
    return pl.pallas_call(
        functools.partial(_merge_proj_ln_kernel, tm=tm),
        grid=(b, s // tm),
        in_specs=plane(A_WIDTH) + plane(LANES) + [row, _resident((A_WIDTH, d)),
                                                  _resident((1, d)), _resident((1, d))],
        out_specs=row,
        out_shape=jax.ShapeDtypeStruct((b, s, d), F32),
        scratch_shapes=[pltpu.VMEM((tm, A_WIDTH), BF16),
                        pltpu.VMEM((A_GROUPS, A_HEADS, tm, A_HEAD_DIM), F32),
                        pltpu.VMEM((A_GROUPS, tm, LANES), F32)],
        compiler_params=_cparams(("parallel", "parallel")),
        name="merge_proj_ln",
    )(*os, *lses, x3d, w_bf16, ln_g, ln_b)


def _ffn_ln_kernel(x_ref, xp_ref, xn_ref, win_ref, cw_ref, cb_ref, wout_ref, g_ref, b_ref,
                   out_ref, act, *, tm):
    si = pl.program_id(1)
    n_s = pl.num_programs(1)
    x = x_ref[0]
    xb = x.astype(BF16)
    prev_ok = (si > 0).astype(F32)
    next_ok = (si < n_s - 1).astype(F32)
    xe = jnp.concatenate([xp_ref[0] * prev_ok, xn_ref[0] * next_ok], axis=0).astype(BF16)
    row = lax.broadcasted_iota(jnp.int32, (tm, 2 * FF_CHUNK), 0)
    for c in range(D_FF // FF_CHUNK):
        cols = slice(c * 2 * FF_CHUNK, (c + 1) * 2 * FF_CHUNK)
        w = win_ref[:, cols]
        h = _dot(xb, w)
        he = _dot(xe, w)
        up = jnp.where(row == 0, he[SUBLANES - 1:SUBLANES], pltpu.roll(h, 1, 0))
        dn = jnp.where(row == tm - 1, he[SUBLANES:SUBLANES + 1], pltpu.roll(h, tm - 1, 0))
        cw = cw_ref[:, cols]
        hc = up * cw[0:1] + h * cw[1:2] + dn * cw[2:3] + cb_ref[:, cols]
        a = hc[:, :FF_CHUNK]
        gate = hc[:, FF_CHUNK:]
        gelu = 0.5 * gate * (1.0 + lax.erf(gate * np.float32(math.sqrt(0.5))))
        act[:, c * FF_CHUNK:(c + 1) * FF_CHUNK] = (a * gelu).astype(BF16)
    y = ALPHA * x + _dot(act[...], wout_ref[...])
    out_ref[0] = _layer_norm(y, g_ref[...], b_ref[...])


def _chunk_interleave(w):
    lead = w.shape[:-1]
    w = w.reshape(lead + (2, D_FF // FF_CHUNK, FF_CHUNK))
    w = jnp.swapaxes(w, -3, -2)
    return w.reshape(lead + (2 * D_FF,))


def _ffn_ln(x3d, w_in, conv_w, conv_b, w_out, ln_g, ln_b, tm):
    b, s, d = x3d.shape
    nb8 = tm // SUBLANES
    n8 = s // SUBLANES
    win = _chunk_interleave(w_in).astype(BF16)
    cw = _chunk_interleave(conv_w)
    cb = _chunk_interleave(conv_b)[None]
    return pl.pallas_call(
        functools.partial(_ffn_ln_kernel, tm=tm),
        grid=(b, s // tm),
        in_specs=[pl.BlockSpec((1, tm, d), lambda bi, si: (bi, si, 0)),
                  pl.BlockSpec((1, SUBLANES, d), lambda bi, si: (bi, jnp.maximum(si * nb8 - 1, 0), 0)),
                  pl.BlockSpec((1, SUBLANES, d), lambda bi, si: (bi, jnp.minimum((si + 1) * nb8, n8 - 1), 0)),
                  _resident((d, 2 * D_FF)), _resident((3, 2 * D_FF)), _resident((1, 2 * D_FF)),
                  _resident((D_FF, d)), _resident((1, d)), _resident((1, d))],
        out_specs=pl.BlockSpec((1, tm, d), lambda bi, si: (bi, si, 0)),
        out_shape=jax.ShapeDtypeStruct((b, s, d), F32),
        scratch_shapes=[pltpu.VMEM((tm, D_FF), BF16)],
        compiler_params=_cparams(("parallel", "parallel")),
        name="ffn_ln",
    )(x3d, x3d, x3d, win, cw, cb, w_out.astype(BF16), ln_g, ln_b)


def _rope_tables(seq):
    inv = 1.0 / (ROPE_THETA ** (jnp.arange(0, QK_ROPE, 2, dtype=F32) / QK_ROPE))
    ang = jnp.arange(seq, dtype=F32)[:, None] * inv[None, :]
    cos, sin = jnp.cos(ang), jnp.sin(ang)
    half = QK_ROPE // 2
    ones = jnp.ones((seq, QK_NOPE), F32)
    z_nope = jnp.zeros((seq, QK_NOPE), F32)
    z_half = jnp.zeros((seq, half), F32)
    z_pad = jnp.zeros((seq, HEAD_PAD - QK_NOPE - QK_ROPE), F32)
    cmul = jnp.concatenate([ones, cos, cos, z_pad], axis=1)
    s_first = jnp.concatenate([z_nope, -sin, z_half, z_pad], axis=1)
    s_second = jnp.concatenate([z_nope, z_half, sin, z_pad], axis=1)
    return cmul, s_first, s_second


def _rope(t, cmul, s_first, s_second):
    half = QK_ROPE // 2
    return (t * cmul + pltpu.roll(t, HEAD_PAD - half, 1) * s_first
            + pltpu.roll(t, half, 1) * s_second)


def _rms_norm(x, g):
    ms = jnp.mean(x * x, axis=-1, keepdims=True)
    return x * lax.rsqrt(ms + RMS_EPS) * g


def _mla_proj_kernel(x_ref, wd_ref, gq_ref, gkv_ref, wuqt_ref, wuk_ref, wuvt_ref,
                     cm_ref, s1_ref, s2_ref, cmt_ref, s1t_ref, s2t_ref, qt_ref, k_ref, vt_ref):
    xb = x_ref[0].astype(BF16)
    c = _dot(xb, wd_ref[...])
    cqn = _rms_norm(c[:, :Q_LORA], gq_ref[...]).astype(BF16)
    ckvn = _rms_norm(c[:, Q_LORA:Q_LORA + KV_LORA], gkv_ref[...]).astype(BF16)
    cm, s1, s2 = cm_ref[...], s1_ref[...], s2_ref[...]
    kr = _rope(c[:, Q_LORA + KV_LORA:], cm, s1, s2)
    k = _dot(ckvn, wuk_ref[...])
    for h in range(B_HEADS):
        cols = slice(h * HEAD_PAD, (h + 1) * HEAD_PAD)
        k_ref[0, :, cols] = (k[:, cols] + kr).astype(BF16)
    qt = _dot_nt(wuqt_ref[...], cqn)
    cmt, s1t, s2t = cmt_ref[...], s1t_ref[...], s2t_ref[...]
    half = QK_ROPE // 2
    for h in range(B_HEADS):
        rows = slice(h * HEAD_PAD, (h + 1) * HEAD_PAD)
        t = qt[rows]
        rot = t * cmt + pltpu.roll(t, HEAD_PAD - half, 0) * s1t + pltpu.roll(t, half, 0) * s2t
        qt_ref[0, rows, :] = (rot * QK_SCALE_LOG2E).astype(BF16)
    vt = _dot_nt(wuvt_ref[...], ckvn).astype(BF16)
    tm = vt.shape[1]
    row = lax.broadcasted_iota(jnp.int32, (V_ROWS - V_HEAD, tm), 0)
    ones_row = jnp.where(row == 0, 1.0, 0.0).astype(BF16)
    for h in range(B_HEADS):
        vt_ref[0, h * V_ROWS:h * V_ROWS + V_HEAD, :] = vt[h * V_HEAD:(h + 1) * V_HEAD]
        vt_ref[0, h * V_ROWS + V_HEAD:(h + 1) * V_ROWS, :] = ones_row


def _mla_weights(w_dkv, w_uq, w_ukv):
    d = w_dkv.shape[0]
    pad_l = jnp.zeros((d, QK_NOPE), F32)
    pad_r = jnp.zeros((d, HEAD_PAD - QK_NOPE - QK_ROPE), F32)
    wd = jnp.concatenate([w_dkv[:, :Q_LORA + KV_LORA], pad_l, w_dkv[:, Q_LORA + KV_LORA:], pad_r], axis=1)
    wq = w_uq.reshape(Q_LORA, B_HEADS, QK_NOPE + QK_ROPE)
    wq = jnp.pad(wq, ((0, 0), (0, 0), (0, HEAD_PAD - QK_NOPE - QK_ROPE))).reshape(Q_LORA, B_HEADS * HEAD_PAD)
    wkv = w_ukv.reshape(KV_LORA, B_HEADS, QK_NOPE + V_HEAD)
    wk = jnp.pad(wkv[:, :, :QK_NOPE], ((0, 0), (0, 0), (0, HEAD_PAD - QK_NOPE))).reshape(KV_LORA, B_HEADS * HEAD_PAD)
    wvt = wkv[:, :, QK_NOPE:].reshape(KV_LORA, B_HEADS * V_HEAD).T
    return wd.astype(BF16), wq.T.astype(BF16), wk.astype(BF16), wvt.astype(BF16)


def _mla_proj(x3d, w_dkv, g_q, g_kv, w_uq, w_ukv, tm):
    b, s, d = x3d.shape
    wd, wqt, wk, wvt = _mla_weights(w_dkv, w_uq, w_ukv)
    tabs = _rope_tables(s)
    tabs_t = [t.T for t in tabs]
    hq = B_HEADS * HEAD_PAD
    hv = B_HEADS * V_ROWS
    tab = pl.BlockSpec((tm, HEAD_PAD), lambda bi, si: (si, 0))
    tab_t = pl.BlockSpec((HEAD_PAD, tm), lambda bi, si: (0, si))
    return pl.pallas_call(
        _mla_proj_kernel,
        grid=(b, s // tm),
        in_specs=[pl.BlockSpec((1, tm, d), lambda bi, si: (bi, si, 0)),
                  _resident(wd.shape), _resident((1, Q_LORA)), _resident((1, KV_LORA)),
                  _resident(wqt.shape), _resident(wk.shape), _resident(wvt.shape),
                  tab, tab, tab, tab_t, tab_t, tab_t],
        out_specs=[pl.BlockSpec((1, hq, tm), lambda bi, si: (bi, 0, si)),
                   pl.BlockSpec((1, tm, hq), lambda bi, si: (bi, si, 0)),
                   pl.BlockSpec((1, hv, tm), lambda bi, si: (bi, 0, si))],
        out_shape=[jax.ShapeDtypeStruct((b, hq, s), BF16),
                   jax.ShapeDtypeStruct((b, s, hq), BF16),
                   jax.ShapeDtypeStruct((b, hv, s), BF16)],
        compiler_params=_cparams(("parallel", "parallel")),
        name="mla_proj",
    )(x3d, wd, g_q[None], g_kv[None], wqt, wk, wvt, *tabs, *tabs_t)


def _mla_attn_kernel(qt_ref, k_ref, vt_ref, o_ref, s_scr, cmax_scr, m_all, acc_all, *, tq, tk, n_q, unroll):
    seq = k_ref.shape[1]
    nk = seq // tk
    blk_q = n_q * tq
    n_blk = seq // blk_q
    total = n_blk * nk
    trips_per_blk = nk // unroll

    def finalize(blk):
        par = blk % 2
        for j in range(n_q):
            acc = acc_all[par, j]
            q0 = pl.multiple_of(blk * blk_q + j * tq, tq)
            o_ref[0, :, pl.ds(q0, tq)] = (acc[:V_HEAD] / acc[V_HEAD:V_HEAD + 1]).astype(BF16)

    def run_step(t, t_static, do_qk=True):
        s_qk, s_sm = t_static % N_SLOTS, (t_static - 1) % N_SLOTS
        n_sm = t - 1 + nk
        par_sm, first_sm = (n_sm // nk + 1) % 2, (n_sm % nk) == 0
        v0 = pl.multiple_of((n_sm % nk) * tk, tk)
        vt = vt_ref[0, :, pl.ds(v0, tk)]
        if do_qk:
            k0 = pl.multiple_of((t % nk) * tk, tk)
            k = k_ref[0, pl.ds(k0, tk), :]
            for j in range(n_q):
                q0 = pl.multiple_of((t // nk) * blk_q + j * tq, tq)
                sc = _dot(k, qt_ref[0, :, pl.ds(q0, tq)])
                s_scr[s_qk, j] = sc
                cmax_scr[s_qk, j] = jnp.max(sc, axis=0, keepdims=True)
        for j in range(n_q):
            m = jnp.where(first_sm, NEG_INF, m_all[par_sm, j])
            m_new = jnp.maximum(m, cmax_scr[s_sm, j])
            alpha = jnp.exp2(m - m_new)
            m_all[par_sm, j] = m_new
            p = jnp.exp2(s_scr[s_sm, j] - m_new).astype(BF16)
            acc_all[par_sm, j] = alpha * acc_all[par_sm, j] + _dot(vt, p)

    s_scr[...] = jnp.zeros(s_scr.shape, F32)
    cmax_scr[...] = jnp.zeros(cmax_scr.shape, F32)
    m_all[...] = jnp.full(m_all.shape, NEG_INF, F32)
    acc_all[...] = jnp.zeros(acc_all.shape, F32)

    def trip(u, carry):
        @pl.when((u % trips_per_blk == 1) & (u > trips_per_blk))
        def _():
            finalize(u // trips_per_blk - 1)

        for i in range(unroll):
            run_step(unroll * u + i, i)
        return carry

    lax.fori_loop(0, total // unroll, trip, 0)
    run_step(total, total, do_qk=False)
    finalize(n_blk - 1)


def _mla_attn(qt, k, vt, n_q, tq, tk):
    b, s, _ = k.shape
    nk = s // tk
    unroll = max(u for u in range(N_SLOTS, MAX_UNROLL + 1, N_SLOTS) if nk % u == 0 and nk // u >= 2)
    assert s % (n_q * tq) == 0 and N_SLOTS == 2
    return pl.pallas_call(
        functools.partial(_mla_attn_kernel, tq=tq, tk=tk, n_q=n_q, unroll=unroll),
        grid=(b, B_HEADS),
        in_specs=[pl.BlockSpec((1, HEAD_PAD, s), lambda bi, h: (bi, h, 0)),
                  pl.BlockSpec((1, s, HEAD_PAD), lambda bi, h: (bi, 0, h)),
                  pl.BlockSpec((1, V_ROWS, s), lambda bi, h: (bi, h, 0))],
        out_specs=pl.BlockSpec((1, V_HEAD, s), lambda bi, h: (bi, h, 0)),
        out_shape=jax.ShapeDtypeStruct((b, B_HEADS * V_HEAD, s), BF16),
        scratch_shapes=[pltpu.VMEM((N_SLOTS, n_q, tk, tq), F32), pltpu.VMEM((N_SLOTS, n_q, 1, tq), F32),
                        pltpu.VMEM((2, n_q, 1, tq), F32), pltpu.VMEM((2, n_q, V_ROWS, tq), F32)],
        compiler_params=_cparams(("parallel", "parallel")),
        name="mla_attn",
    )(qt, k, vt)


def _proj_t_ln_kernel(ot_ref, x_ref, w_ref, g_ref, b_ref, out_ref):
    y = ALPHA * x_ref[0] + _dot_tn(ot_ref[0], w_ref[...])
    out_ref[0] = _layer_norm(y, g_ref[...], b_ref[...])


def _proj_t_ln(ot, x3d, w_bf16, ln_g, ln_b, tm):
    b, s, d = x3d.shape
    k = ot.shape[1]
    return pl.pallas_call(
        _proj_t_ln_kernel,
        grid=(b, s // tm),
        in_specs=[pl.BlockSpec((1, k, tm), lambda bi, si: (bi, 0, si)),
                  pl.BlockSpec((1, tm, d), lambda bi, si: (bi, si, 0)),
                  _resident((k, d)), _resident((1, d)), _resident((1, d))],
        out_specs=pl.BlockSpec((1, tm, d), lambda bi, si: (bi, si, 0)),
        out_shape=jax.ShapeDtypeStruct((b, s, d), F32),
        compiler_params=_cparams(("parallel", "parallel")),
        name="proj_t_ln",
    )(ot, x3d, w_bf16, ln_g, ln_b)


def _encoder(x, bias, w_qkv_a, w_o_a, w_dkv_b, g_q_b, g_kv_b, w_uq_b, w_ukv_b, w_o_b,
             ffn_w_in, ffn_conv_w, ffn_conv_b, ffn_w_out, ln_g, ln_b):
    b, s, d = x.shape
    tm = 512
    for i in range(DEPTH):
        j = i // N_MIXERS
        g1, b1 = ln_g[i, 0][None], ln_b[i, 0][None]
        g2, b2 = ln_g[i, 1][None], ln_b[i, 1][None]
        if i % N_MIXERS == 0:
            qkv = _qkv_proj(x, w_qkv_a[j].astype(BF16), tm=256)
            outs = [_dilated_attn(qkv[g], bias, g) for g in range(A_GROUPS)]
            x = _merge_proj_ln([o for o, _ in outs], [l for _, l in outs], x,
                               w_o_a[j].astype(BF16), g1, b1, tm)
        else:
            q, k, vt = _mla_proj(x, w_dkv_b[j], g_q_b[j], g_kv_b[j], w_uq_b[j], w_ukv_b[j], tm)
            ot = _mla_attn(q, k, vt, n_q=2, tq=512, tk=256)
            x = _proj_t_ln(ot, x, w_o_b[j].astype(BF16), g1, b1, tm)
        x = _ffn_ln(x, ffn_w_in[i], ffn_conv_w[i], ffn_conv_b[i], ffn_w_out[i], g2, b2, tm)
    return x


def kernel(x_prompt, x_sample, rel_bias, w_qkv_a, w_o_a, w_dkv_b, g_q_b, g_kv_b, w_uq_b, w_ukv_b, w_o_b,
           ffn_w_in, ffn_conv_w, ffn_conv_b, ffn_w_out, ln_g, ln_b):
    bias = _bias_tiles(rel_bias)
    args = (bias, w_qkv_a, w_o_a, w_dkv_b, g_q_b, g_kv_b, w_uq_b, w_ukv_b, w_o_b,
            ffn_w_in, ffn_conv_w, ffn_conv_b, ffn_w_out, ln_g, ln_b)
    return (_encoder(x_prompt, *args), _encoder(x_sample, *args))
```

```python
import functools
import math

import jax
import jax.numpy as jnp
import numpy as np
from jax import lax
from jax.experimental import pallas as pl
from jax.experimental.pallas import tpu as pltpu

F32 = jnp.float32
BF16 = jnp.bfloat16

D_MODEL = 1024
DEPTH = 2
N_MIXERS = 2

A_WINDOWS = (128, 512, 2048)
A_DILATIONS = (1, 4, 16)
A_GROUPS = 3
A_HEADS = 8
A_HEAD_DIM = 128
A_WIDTH = A_HEADS * A_HEAD_DIM
A_HALF = 64
NUM_BUCKETS = 32
MAX_DISTANCE = max(A_WINDOWS) // 2

B_HEADS = 16
Q_LORA = 384
KV_LORA = 256
QK_NOPE = 64
QK_ROPE = 32
V_HEAD = 64
ROPE_THETA = 10000.0
HEAD_PAD = 128
V_ROWS = V_HEAD + 16
N_SLOTS = 2
MAX_UNROLL = 4

D_FF = 2816
FF_CHUNK = 256

ALPHA = (2.0 * DEPTH) ** 0.25
LN_EPS = 1e-5
RMS_EPS = 1e-6
NEG_INF = -1e30

LANES = 128
SUBLANES = 8
VMEM_LIMIT = 56 * 1024 * 1024

QK_SCALE_LOG2E = np.float32((QK_NOPE + QK_ROPE) ** -0.5 * math.log2(math.e))


def _cparams(sem):
    return pltpu.CompilerParams(dimension_semantics=sem, vmem_limit_bytes=VMEM_LIMIT)


def _resident(shape):
    nd = len(shape)
    return pl.BlockSpec(shape, lambda *_: (0,) * nd, pipeline_mode=pl.Buffered(1))


def _layer_norm(y, g, b):
    mu = jnp.mean(y, axis=-1, keepdims=True)
    yc = y - mu
    var = jnp.mean(yc * yc, axis=-1, keepdims=True)
    return yc * lax.rsqrt(var + LN_EPS) * g + b


def _dot(a, b):
    return jnp.dot(a, b, preferred_element_type=F32)


def _dot_nt(a, b):
    return lax.dot_general(a, b, (((1,), (1,)), ((), ())), preferred_element_type=F32)


def _dot_tn(a, b):
    return lax.dot_general(a, b, (((0,), (0,)), ((), ())), preferred_element_type=F32)


def _qkv_proj_kernel(x_ref, w_ref, *refs, tm):
    outs, res_scr = refs[:A_GROUPS], refs[A_GROUPS]
    xb = x_ref[0].astype(BF16)
    slot = 0
    for g, dil in enumerate(A_DILATIONS):
        for c in range(3):
            n0 = (g * 3 + c) * A_WIDTH
            cols = slice(c * A_WIDTH, (c + 1) * A_WIDTH)
            res = _dot(xb, w_ref[:, n0:n0 + A_WIDTH])
            if dil == 1:
                outs[g][0, 0, :, cols] = res.astype(BF16)
                continue
            for h in range(A_HEADS):
                res_scr[slot, h] = res[:, h * LANES:(h + 1) * LANES]
            for r in range(dil):
                for h in range(A_HEADS):
                    lanes = slice(c * A_WIDTH + h * LANES, c * A_WIDTH + (h + 1) * LANES)
                    outs[g][0, r, :, lanes] = res_scr[slot, h, pl.ds(r, tm // dil, stride=dil), :].astype(BF16)
            slot = 1 - slot


def _qkv_proj(x3d, w_bf16, tm):
    b, s, d = x3d.shape
    width = 3 * A_WIDTH
    return pl.pallas_call(
        functools.partial(_qkv_proj_kernel, tm=tm),
        grid=(b, s // tm),
        in_specs=[pl.BlockSpec((1, tm, d), lambda bi, i: (bi, i, 0)), _resident(w_bf16.shape)],
        out_specs=[pl.BlockSpec((1, dil, tm // dil, width), lambda bi, i: (bi, 0, i, 0))
                   for dil in A_DILATIONS],
        out_shape=[jax.ShapeDtypeStruct((b, dil, s // dil, width), BF16) for dil in A_DILATIONS],
        scratch_shapes=[pltpu.VMEM((2, A_HEADS, tm, LANES), F32)],
        compiler_params=_cparams(("parallel", "parallel")),
        name="qkv_proj",
    )(x3d, w_bf16)


def _t5_bucket(rel):
    nb = NUM_BUCKETS // 2
    max_exact = nb // 2
    ret = jnp.where(rel > 0, nb, 0)
    n = jnp.abs(rel)
    large = max_exact + (jnp.log(jnp.maximum(n, 1).astype(F32) / max_exact)
                         / math.log(MAX_DISTANCE / max_exact) * (nb - max_exact)).astype(jnp.int32)
    large = jnp.minimum(large, nb - 1)
    return ret + jnp.where(n < max_exact, n, large)


def _band_bucket_index():
    qi = jnp.arange(2 * A_HALF)[:, None]
    kj = jnp.arange(4 * A_HALF)[None, :]
    rel = kj - A_HALF - qi
    tiles = [jnp.where(jnp.abs(rel) <= A_HALF, _t5_bucket(rel * dil), -1) for dil in A_DILATIONS]
    return jnp.stack(tiles, axis=0).astype(jnp.int32)


def _bias_tile_kernel(tab_ref, idx_ref, o_ref):
    col = pl.program_id(0) * A_HEADS + pl.program_id(1)
    idx = idx_ref[0]
    acc = jnp.full(idx.shape, NEG_INF, F32)
    for bucket in range(NUM_BUCKETS):
        acc = jnp.where(idx == bucket, tab_ref[bucket, col], acc)
    o_ref[0, 0] = acc


def _bias_tiles(rel_bias):
    idx = _band_bucket_index()
    tq, tk = idx.shape[1:]
    return pl.pallas_call(
        _bias_tile_kernel,
        grid=(A_GROUPS, A_HEADS),
        in_specs=[pl.BlockSpec(memory_space=pltpu.SMEM),
                  pl.BlockSpec((1, tq, tk), lambda g, h: (g, 0, 0))],
        out_specs=pl.BlockSpec((1, 1, tq, tk), lambda g, h: (g, h, 0, 0)),
        out_shape=jax.ShapeDtypeStruct((A_GROUPS, A_HEADS, tq, tk), F32),
        compiler_params=_cparams(("arbitrary", "arbitrary")),
        name="bias_tiles",
    )(rel_bias, idx)


def _dilated_attn_kernel(q_ref, kl_ref, km_ref, kr_ref, vl_ref, vm_ref, vr_ref, bias_ref,
                         o_ref, lse_ref, kbuf, vbuf, *, tq, sub_len):
    li = pl.program_id(2)
    kbuf[0:A_HALF] = kl_ref[0, 0]
    kbuf[A_HALF:A_HALF + tq] = km_ref[0, 0]
    kbuf[A_HALF + tq:] = kr_ref[0, 0]
    vbuf[0:A_HALF] = vl_ref[0, 0]
    vbuf[A_HALF:A_HALF + tq] = vm_ref[0, 0]
    vbuf[A_HALF + tq:] = vr_ref[0, 0]

    scale = A_HEAD_DIM ** -0.5
    qt, kt = 2 * A_HALF, 4 * A_HALF
    lane = lax.broadcasted_iota(jnp.int32, (qt, LANES), 1)
    for t in range(tq // qt):
        pos = li * tq + t * qt - A_HALF + lax.broadcasted_iota(jnp.int32, (1, kt), 1)
        edge = jnp.where((pos >= 0) & (pos < sub_len), 0.0, NEG_INF).astype(F32)
        lse_tile = jnp.zeros((qt, LANES), F32)
        for h in range(A_HEADS):
            cols = slice(h * A_HEAD_DIM, (h + 1) * A_HEAD_DIM)
            q = q_ref[0, 0, t * qt:(t + 1) * qt, cols]
            k = kbuf[t * qt:t * qt + kt, cols]
            v = vbuf[t * qt:t * qt + kt, cols]
            s = _dot_nt(q, k) * scale + bias_ref[0, h] + edge
            m = jnp.max(s, axis=-1, keepdims=True)
            p = jnp.exp(s - m)
            den = jnp.sum(p, axis=-1, keepdims=True)
            o = _dot(p.astype(BF16), v) / den
            o_ref[0, 0, t * qt:(t + 1) * qt, cols] = o.astype(BF16)
            lse_tile = jnp.where(lane == h, m + jnp.log(den), lse_tile)
        lse_ref[0, 0, t * qt:(t + 1) * qt, :] = lse_tile


def _dilated_attn(qkv_g, bias, g):
    batch, dil, sub_len, _ = qkv_g.shape
    tq = min(1024, sub_len)
    n_t = sub_len // tq
    hb = tq // A_HALF
    n_hb = sub_len // A_HALF

    main = lambda which: pl.BlockSpec((1, 1, tq, A_WIDTH), lambda b, r, i: (b, r, i, which))
    left = lambda which: pl.BlockSpec(
        (1, 1, A_HALF, A_WIDTH), lambda b, r, i: (b, r, jnp.maximum(i * hb - 1, 0), which))
    right = lambda which: pl.BlockSpec(
        (1, 1, A_HALF, A_WIDTH), lambda b, r, i: (b, r, jnp.minimum((i + 1) * hb, n_hb - 1), which))

    return pl.pallas_call(
        functools.partial(_dilated_attn_kernel, tq=tq, sub_len=sub_len),
        grid=(batch, dil, n_t),
        in_specs=[main(0), left(1), main(1), right(1), left(2), main(2), right(2),
                  pl.BlockSpec((1, A_HEADS, 2 * A_HALF, 4 * A_HALF), lambda b, r, i: (g, 0, 0, 0))],
        out_specs=[pl.BlockSpec((1, 1, tq, A_WIDTH), lambda b, r, i: (b, r, i, 0)),
                   pl.BlockSpec((1, 1, tq, LANES), lambda b, r, i: (b, r, i, 0))],
        out_shape=[jax.ShapeDtypeStruct((batch, dil, sub_len, A_WIDTH), BF16),
                   jax.ShapeDtypeStruct((batch, dil, sub_len, LANES), F32)],
        scratch_shapes=[pltpu.VMEM((tq + 2 * A_HALF, A_WIDTH), BF16),
                        pltpu.VMEM((tq + 2 * A_HALF, A_WIDTH), BF16)],
        compiler_params=_cparams(("parallel", "parallel", "parallel")),
        name=f"dilated_attn_g{g}",
    )(qkv_g, qkv_g, qkv_g, qkv_g, qkv_g, qkv_g, qkv_g, bias)


def _merge_proj_ln_kernel(o0_ref, o1_ref, o2_ref, l0_ref, l1_ref, l2_ref, x_ref, w_ref, g_ref, b_ref,
                          out_ref, merged, o_scr, l_scr, *, tm):
    for g, (o_ref, l_ref, dil) in enumerate(zip((o0_ref, o1_ref, o2_ref), (l0_ref, l1_ref, l2_ref),
                                                A_DILATIONS)):
        for r in range(dil):
            rows = pl.ds(r, tm // dil, stride=dil) if dil > 1 else slice(None)
            l_scr[g, rows, :] = l_ref[0, r]
            for h in range(A_HEADS):
                o_scr[g, h, rows, :] = o_ref[0, r, :, h * A_HEAD_DIM:(h + 1) * A_HEAD_DIM].astype(F32)
    l0, l1, l2 = l_scr[0], l_scr[1], l_scr[2]
    m = jnp.maximum(jnp.maximum(l0, l1), l2)
    e0, e1, e2 = jnp.exp(l0 - m), jnp.exp(l1 - m), jnp.exp(l2 - m)
    den = e0 + e1 + e2
    w0, w1, w2 = e0 / den, e1 / den, e2 / den
    for h in range(A_HEADS):
        cols = slice(h * A_HEAD_DIM, (h + 1) * A_HEAD_DIM)
        mix = (w0[:, h:h + 1] * o_scr[0, h] + w1[:, h:h + 1] * o_scr[1, h]
               + w2[:, h:h + 1] * o_scr[2, h])
        merged[:, cols] = mix.astype(BF16)
    y = ALPHA * x_ref[0] + _dot(merged[...], w_ref[...])
    out_ref[0] = _layer_norm(y, g_ref[...], b_ref[...])


def _merge_proj_ln(os, lses, x3d, w_bf16, ln_g, ln_b, tm):
    b, s, d = x3d.shape
    plane = lambda w: [pl.BlockSpec((1, dil, tm // dil, w), lambda bi, i: (bi, 0, i, 0))
                       for dil in A_DILATIONS]
    row = pl.BlockSpec((1, tm, d), lambda bi, i: (bi, i, 0))
    return pl.pallas_call(
        functools.partial(_merge_proj_ln_kernel, tm=tm),
        grid=(b, s // tm),
        in_specs=plane(A_WIDTH) + plane(LANES) + [row, _resident((A_WIDTH, d)),
                                                  _resident((1, d)), _resident((1, d))],
        out_specs=row,
        out_shape=jax.ShapeDtypeStruct((b, s, d), F32),
        scratch_shapes=[pltpu.VMEM((tm, A_WIDTH), BF16),
                        pltpu.VMEM((A_GROUPS, A_HEADS, tm, A_HEAD_DIM), F32),
                        pltpu.VMEM((A_GROUPS, tm, LANES), F32)],
        compiler_params=_cparams(("parallel", "parallel")),
        name="merge_proj_ln",
    )(*os, *lses, x3d, w_bf16, ln_g, ln_b)


def _ffn_ln_kernel(x_ref, xp_ref, xn_ref, win_ref, cw_ref, cb_ref, wout_ref, g_ref, b_ref,
                   out_ref, act, *, tm):
    si = pl.program_id(1)
    n_s = pl.num_programs(1)
    x = x_ref[0]
    xb = x.astype(BF16)
    prev_ok = (si > 0).astype(F32)
    next_ok = (si < n_s - 1).astype(F32)
    xe = jnp.concatenate([xp_ref[0] * prev_ok, xn_ref[0] * next_ok], axis=0).astype(BF16)
    row = lax.broadcasted_iota(jnp.int32, (tm, 2 * FF_CHUNK), 0)
    for c in range(D_FF // FF_CHUNK):
        cols = slice(c * 2 * FF_CHUNK, (c + 1) * 2 * FF_CHUNK)
        w = win_ref[:, cols]
        h = _dot(xb, w)
        he = _dot(xe, w)
        up = jnp.where(row == 0, he[SUBLANES - 1:SUBLANES], pltpu.roll(h, 1, 0))
        dn = jnp.where(row == tm - 1, he[SUBLANES:SUBLANES + 1], pltpu.roll(h, tm - 1, 0))
        cw = cw_ref[:, cols]
        hc = up * cw[0:1] + h * cw[1:2] + dn * cw[2:3] + cb_ref[:, cols]
        a = hc[:, :FF_CHUNK]
        gate = hc[:, FF_CHUNK:]
        gelu = 0.5 * gate * (1.0 + lax.erf(gate * np.float32(math.sqrt(0.5))))
        act[:, c * FF_CHUNK:(c + 1) * FF_CHUNK] = (a * gelu).astype(BF16)
    y = ALPHA * x + _dot(act[...], wout_ref[...])
    out_ref[0] = _layer_norm(y, g_ref[...], b_ref[...])


def _chunk_interleave(w):
    lead = w.shape[:-1]
    w = w.reshape(lead + (2, D_FF // FF_CHUNK, FF_CHUNK))
    w = jnp.swapaxes(w, -3, -2)
    return w.reshape(lead + (2 * D_FF,))


def _ffn_ln(x3d, w_in, conv_w, conv_b, w_out, ln_g, ln_b, tm):
    b, s, d = x3d.shape
    nb8 = tm // SUBLANES
    n8 = s // SUBLANES
    win = _chunk_interleave(w_in).astype(BF16)
    cw = _chunk_interleave(conv_w)
    cb = _chunk_interleave(conv_b)[None]
    return pl.pallas_call(
        functools.partial(_ffn_ln_kernel, tm=tm),
        grid=(b, s // tm),
        in_specs=[pl.BlockSpec((1, tm, d), lambda bi, si: (bi, si, 0)),
                  pl.BlockSpec((1, SUBLANES, d), lambda bi, si: (bi, jnp.maximum(si * nb8 - 1, 0), 0)),
                  pl.BlockSpec((1, SUBLANES, d), lambda bi, si: (bi, jnp.minimum((si + 1) * nb8, n8 - 1), 0)),
                  _resident((d, 2 * D_FF)), _resident((3, 2 * D_FF)), _resident((1, 2 * D_FF)),
                  _resident((D_FF, d)), _resident((1, d)), _resident((1, d))],
        out_specs=pl.BlockSpec((1, tm, d), lambda bi, si: (bi, si, 0)),
        out_shape=jax.ShapeDtypeStruct((b, s, d), F32),
        scratch_shapes=[pltpu.VMEM((tm, D_FF), BF16)],
        compiler_params=_cparams(("parallel", "parallel")),
        name="ffn_ln",
    )(x3d, x3d, x3d, win, cw, cb, w_out.astype(BF16), ln_g, ln_b)


def _rope_tables(seq):
    inv = 1.0 / (ROPE_THETA ** (jnp.arange(0, QK_ROPE, 2, dtype=F32) / QK_ROPE))
    ang = jnp.arange(seq, dtype=F32)[:, None] * inv[None, :]
    cos, sin = jnp.cos(ang), jnp.sin(ang)
    half = QK_ROPE // 2
    ones = jnp.ones((seq, QK_NOPE), F32)
    z_nope = jnp.zeros((seq, QK_NOPE), F32)
    z_half = jnp.zeros((seq, half), F32)
    z_pad = jnp.zeros((seq, HEAD_PAD - QK_NOPE - QK_ROPE), F32)
    cmul = jnp.concatenate([ones, cos, cos, z_pad], axis=1)
    s_first = jnp.concatenate([z_nope, -sin, z_half, z_pad], axis=1)
    s_second = jnp.concatenate([z_nope, z_half, sin, z_pad], axis=1)
    return cmul, s_first, s_second


def _rope(t, cmul, s_first, s_second):
    half = QK_ROPE // 2
    return (t * cmul + pltpu.roll(t, HEAD_PAD - half, 1) * s_first
            + pltpu.roll(t, half, 1) * s_second)


def _rms_norm(x, g):
    ms = jnp.mean(x * x, axis=-1, keepdims=True)
    return x * lax.rsqrt(ms + RMS_EPS) * g


def _mla_proj_kernel(x_ref, wd_ref, gq_ref, gkv_ref, wuqt_ref, wuk_ref, wuvt_ref,
                     cm_ref, s1_ref, s2_ref, cmt_ref, s1t_ref, s2t_ref, qt_ref, k_ref, vt_ref):
    xb = x_ref[0].astype(BF16)
    c = _dot(xb, wd_ref[...])
    cqn = _rms_norm(c[:, :Q_LORA], gq_ref[...]).astype(BF16)
    ckvn = _rms_norm(c[:, Q_LORA:Q_LORA + KV_LORA], gkv_ref[...]).astype(BF16)
    cm, s1, s2 = cm_ref[...], s1_ref[...], s2_ref[...]
    kr = _rope(c[:, Q_LORA + KV_LORA:], cm, s1, s2)
    k = _dot(ckvn, wuk_ref[...])
    for h in range(B_HEADS):
        cols = slice(h * HEAD_PAD, (h + 1) * HEAD_PAD)
        k_ref[0, :, cols] = (k[:, cols] + kr).astype(BF16)
    qt = _dot_nt(wuqt_ref[...], cqn)
    cmt, s1t, s2t = cmt_ref[...], s1t_ref[...], s2t_ref[...]
    half = QK_ROPE // 2
    for h in range(B_HEADS):
        rows = slice(h * HEAD_PAD, (h + 1) * HEAD_PAD)
        t = qt[rows]
        rot = t * cmt + pltpu.roll(t, HEAD_PAD - half, 0) * s1t + pltpu.roll(t, half, 0) * s2t
        qt_ref[0, rows, :] = (rot * QK_SCALE_LOG2E).astype(BF16)
    vt = _dot_nt(wuvt_ref[...], ckvn).astype(BF16)
    tm = vt.shape[1]
    row = lax.broadcasted_iota(jnp.int32, (V_ROWS - V_HEAD, tm), 0)
    ones_row = jnp.where(row == 0, 1.0, 0.0).astype(BF16)
    for h in range(B_HEADS):
        vt_ref[0, h * V_ROWS:h * V_ROWS + V_HEAD, :] = vt[h * V_HEAD:(h + 1) * V_HEAD]
        vt_ref[0, h * V_ROWS + V_HEAD:(h + 1) * V_ROWS, :] = ones_row


def _mla_weights(w_dkv, w_uq, w_ukv):
    d = w_dkv.shape[0]
    pad_l = jnp.zeros((d, QK_NOPE), F32)
    pad_r = jnp.zeros((d, HEAD_PAD - QK_NOPE - QK_ROPE), F32)
    wd = jnp.concatenate([w_dkv[:, :Q_LORA + KV_LORA], pad_l, w_dkv[:, Q_LORA + KV_LORA:], pad_r], axis=1)
    wq = w_uq.reshape(Q_LORA, B_HEADS, QK_NOPE + QK_ROPE)
    wq = jnp.pad(wq, ((0, 0), (0, 0), (0, HEAD_PAD - QK_NOPE - QK_ROPE))).reshape(Q_LORA, B_HEADS * HEAD_PAD)
    wkv = w_ukv.reshape(KV_LORA, B_HEADS, QK_NOPE + V_HEAD)
    wk = jnp.pad(wkv[:, :, :QK_NOPE], ((0, 0), (0, 0), (0, HEAD_PAD - QK_NOPE))).reshape(KV_LORA, B_HEADS * HEAD_PAD)
    wvt = wkv[:, :, QK_NOPE:].reshape(KV_LORA, B_HEADS * V_HEAD).T
    return wd.astype(BF16), wq.T.astype(BF16), wk.astype(BF16), wvt.astype(BF16)


def _mla_proj(x3d, w_dkv, g_q, g_kv, w_uq, w_ukv, tm):
    b, s, d = x3d.shape
    wd, wqt, wk, wvt = _mla_weights(w_dkv, w_uq, w_ukv)
    tabs = _rope_tables(s)
    tabs_t = [t.T for t in tabs]
    hq = B_HEADS * HEAD_PAD
    hv = B_HEADS * V_ROWS
    tab = pl.BlockSpec((tm, HEAD_PAD), lambda bi, si: (si, 0))
    tab_t = pl.BlockSpec((HEAD_PAD, tm), lambda bi, si: (0, si))
    return pl.pallas_call(
        _mla_proj_kernel,
        grid=(b, s // tm),
        in_specs=[pl.BlockSpec((1, tm, d), lambda bi, si: (bi, si, 0)),
                  _resident(wd.shape), _resident((1, Q_LORA)), _resident((1, KV_LORA)),
                  _resident(wqt.shape), _resident(wk.shape), _resident(wvt.shape),
                  tab, tab, tab, tab_t, tab_t, tab_t],
        out_specs=[pl.BlockSpec((1, hq, tm), lambda bi, si: (bi, 0, si)),
                   pl.BlockSpec((1, tm, hq), lambda bi, si: (bi, si, 0)),
                   pl.BlockSpec((1, hv, tm), lambda bi, si: (bi, 0, si))],
        out_shape=[jax.ShapeDtypeStruct((b, hq, s), BF16),
                   jax.ShapeDtypeStruct((b, s, hq), BF16),
                   jax.ShapeDtypeStruct((b, hv, s), BF16)],
        compiler_params=_cparams(("parallel", "parallel")),
        name="mla_proj",
    )(x3d, wd, g_q[None], g_kv[None], wqt, wk, wvt, *tabs, *tabs_t)


def _mla_attn_kernel(qt_ref, k_ref, vt_ref, o_ref, s_scr, cmax_scr, m_all, acc_all, *, tq, tk, n_q, unroll):
    seq = k_ref.shape[1]
    nk = seq // tk
    blk_q = n_q * tq
    n_blk = seq // blk_q
    total = n_blk * nk
    trips_per_blk = nk // unroll

    def finalize(blk):
        par = blk % 2
        for j in range(n_q):
            acc = acc_all[par, j]
            q0 = pl.multiple_of(blk * blk_q + j * tq, tq)
            o_ref[0, :, pl.ds(q0, tq)] = (acc[:V_HEAD] / acc[V_HEAD:V_HEAD + 1]).astype(BF16)

    def run_step(t, t_static, do_qk=True):
        s_qk, s_sm = t_static % N_SLOTS, (t_static - 1) % N_SLOTS
        n_sm = t - 1 + nk
        par_sm, first_sm = (n_sm // nk + 1) % 2, (n_sm % nk) == 0
        v0 = pl.multiple_of((n_sm % nk) * tk, tk)
        vt = vt_ref[0, :, pl.ds(v0, tk)]
        if do_qk:
            k0 = pl.multiple_of((t % nk) * tk, tk)
            k = k_ref[0, pl.ds(k0, tk), :]
            for j in range(n_q):
                q0 = pl.multiple_of((t // nk) * blk_q + j * tq, tq)
                sc = _dot(k, qt_ref[0, :, pl.ds(q0, tq)])
                s_scr[s_qk, j] = sc
                cmax_scr[s_qk, j] = jnp.max(sc, axis=0, keepdims=True)
        for j in range(n_q):
            m = jnp.where(first_sm, NEG_INF, m_all[par_sm, j])
            m_new = jnp.maximum(m, cmax_scr[s_sm, j])
            alpha = jnp.exp2(m - m_new)
            m_all[par_sm, j] = m_new
            p = jnp.exp2(s_scr[s_sm, j] - m_new).astype(BF16)
            acc_all[par_sm, j] = alpha * acc_all[par_sm, j] + _dot(vt, p)

    s_scr[...] = jnp.zeros(s_scr.shape, F32)
    cmax_scr[...] = jnp.zeros(cmax_scr.shape, F32)
    m_all[...] = jnp.full(m_all.shape, NEG_INF, F32)
    acc_all[...] = jnp.zeros(acc_all.shape, F32)

    def trip(u, carry):
        @pl.when((u % trips_per_blk == 1) & (u > trips_per_blk))
        def _():
            finalize(u // trips_per_blk - 1)

        for i in range(unroll):
            run_step(unroll * u + i, i)
        return carry

    lax.fori_loop(0, total // unroll, trip, 0)
    run_step(total, total, do_qk=False)
    finalize(n_blk - 1)


def _mla_attn(qt, k, vt, n_q, tq, tk):
    b, s, _ = k.shape
    nk = s // tk
    unroll = max(u for u in range(N_SLOTS, MAX_UNROLL + 1, N_SLOTS) if nk % u == 0 and nk // u >= 2)
    assert s % (n_q * tq) == 0 and N_SLOTS == 2
    return pl.pallas_call(
        functools.partial(_mla_attn_kernel, tq=tq, tk=tk, n_q=n_q, unroll=unroll),
        grid=(b, B_HEADS),
        in_specs=[pl.BlockSpec((1, HEAD_PAD, s), lambda bi, h: (bi, h, 0)),
                  pl.BlockSpec((1, s, HEAD_PAD), lambda bi, h: (bi, 0, h)),
                  pl.BlockSpec((1, V_ROWS, s), lambda bi, h: (bi, h, 0))],
        out_specs=pl.BlockSpec((1, V_HEAD, s), lambda bi, h: (bi, h, 0)),
        out_shape=jax.ShapeDtypeStruct((b, B_HEADS * V_HEAD, s), BF16),
        scratch_shapes=[pltpu.VMEM((N_SLOTS, n_q, tk, tq), F32), pltpu.VMEM((N_SLOTS, n_q, 1, tq), F32),
                        pltpu.VMEM((2, n_q, 1, tq), F32), pltpu.VMEM((2, n_q, V_ROWS, tq), F32)],
        compiler_params=_cparams(("parallel", "parallel")),
        name="mla_attn",
    )(qt, k, vt)


def _proj_t_ln_kernel(ot_ref, x_ref, w_ref, g_ref, b_ref, out_ref):
    y = ALPHA * x_ref[0] + _dot_tn(ot_ref[0], w_ref[...])
    out_ref[0] = _layer_norm(y, g_ref[...], b_ref[...])


def _proj_t_ln(ot, x3d, w_bf16, ln_g, ln_b, tm):
    b, s, d = x3d.shape
    k = ot.shape[1]
    return pl.pallas_call(
        _proj_t_ln_kernel,
        grid=(b, s // tm),
        in_specs=[pl.BlockSpec((1, k, tm), lambda bi, si: (bi, 0, si)),
                  pl.BlockSpec((1, tm, d), lambda bi, si: (bi, si, 0)),
                  _resident((k, d)), _resident((1, d)), _resident((1, d))],
        out_specs=pl.BlockSpec((1, tm, d), lambda bi, si: (bi, si, 0)),
        out_shape=jax.ShapeDtypeStruct((b, s, d), F32),
        compiler_params=_cparams(("parallel", "parallel")),
        name="proj_t_ln",
    )(ot, x3d, w_bf16, ln_g, ln_b)


def _encoder(x, bias, w_qkv_a, w_o_a, w_dkv_b, g_q_b, g_kv_b, w_uq_b, w_ukv_b, w_o_b,
             ffn_w_in, ffn_conv_w, ffn_conv_b, ffn_w_out, ln_g, ln_b):
    b, s, d = x.shape
    tm = 512
    for i in range(DEPTH):
        j = i // N_MIXERS
        g1, b1 = ln_g[i, 0][None], ln_b[i, 0][None]
        g2, b2 = ln_g[i, 1][None], ln_b[i, 1][None]
        if i % N_MIXERS == 0:
            qkv = _qkv_proj(x, w_qkv_a[j].astype(BF16), tm=256)
            outs = [_dilated_attn(qkv[g], bias, g) for g in range(A_GROUPS)]
            x = _merge_proj_ln([o for o, _ in outs], [l for _, l in outs], x,
                               w_o_a[j].astype(BF16), g1, b1, tm)
        else:
            q, k, vt = _mla_proj(x, w_dkv_b[j], g_q_b[j], g_kv_b[j], w_uq_b[j], w_ukv_b[j], tm)
            ot = _mla_attn(q, k, vt, n_q=2, tq=512, tk=256)
            x = _proj_t_ln(ot, x, w_o_b[j].astype(BF16), g1, b1, tm)
        x = _ffn_ln(x, ffn_w_in[i], ffn_conv_w[i], ffn_conv_b[i], ffn_w_out[i], g2, b2, tm)
    return x


def kernel(x_prompt, x_sample, rel_bias, w_qkv_a, w_o_a, w_dkv_b, g_q_b, g_kv_b, w_uq_b, w_ukv_b, w_o_b,
           ffn_w_in, ffn_conv_w, ffn_conv_b, ffn_w_out, ln_g, ln_b):
    bias = _bias_tiles(rel_bias)
    args = (bias, w_qkv_a, w_o_a, w_dkv_b, g_q_b, g_kv_b, w_uq_b, w_ukv_b, w_o_b,
            ffn_w_in, ffn_conv_w, ffn_conv_b, ffn_w_out, ln_g, ln_b)
    return (_encoder(x_prompt, *args), _encoder(x_sample, *args))
```

```python
import functools
import math

import jax
import jax.numpy as jnp
import numpy as np
from jax import lax
from jax.experimental import pallas as pl
from jax.experimental.pallas import tpu as pltpu

F32 = jnp.float32
BF16 = jnp.bfloat16

D_MODEL = 1024
DEPTH = 2
N_MIXERS = 2

A_WINDOWS = (128, 512, 2048)
A_DILATIONS = (1, 4, 16)
A_GROUPS = 3
A_HEADS = 8
A_HEAD_DIM = 128
A_WIDTH = A_HEADS * A_HEAD_DIM
A_HALF = 64
NUM_BUCKETS = 32
MAX_DISTANCE = max(A_WINDOWS) // 2

B_HEADS = 16
Q_LORA = 384
KV_LORA = 256
QK_NOPE = 64
QK_ROPE = 32
V_HEAD = 64
ROPE_THETA = 10000.0
HEAD_PAD = 128
V_ROWS = V_HEAD + 16
N_SLOTS = 2
MAX_UNROLL = 8

D_FF = 2816
FF_CHUNK = 256

ALPHA = (2.0 * DEPTH) ** 0.25
LN_EPS = 1e-5
RMS_EPS = 1e-6
NEG_INF = -1e30

LANES = 128
SUBLANES = 8
VMEM_LIMIT = 56 * 1024 * 1024

ROW_TILE = 512
QKV_ROW_TILE = 256
DILATED_Q_TILE = 1024
MLA_Q_TILES = 2
MLA_Q_TILE = 512
MLA_K_TILE = 256

QK_SCALE_LOG2E = np.float32((QK_NOPE + QK_ROPE) ** -0.5 * math.log2(math.e))


def _cparams(sem):
    return pltpu.CompilerParams(dimension_semantics=sem, vmem_limit_bytes=VMEM_LIMIT)


def _resident(shape):
    nd = len(shape)
    return pl.BlockSpec(shape, lambda *_: (0,) * nd, pipeline_mode=pl.Buffered(1))


def _layer_norm(y, g, b):
    mu = jnp.mean(y, axis=-1, keepdims=True)
    yc = y - mu
    var = jnp.mean(yc * yc, axis=-1, keepdims=True)
    return yc * lax.rsqrt(var + LN_EPS) * g + b


def _dot(a, b):
    return jnp.dot(a, b, preferred_element_type=F32)


def _dot_nt(a, b):
    return lax.dot_general(a, b, (((1,), (1,)), ((), ())), preferred_element_type=F32)


def _dot_tn(a, b):
    return lax.dot_general(a, b, (((0,), (0,)), ((), ())), preferred_element_type=F32)


def _qkv_proj_kernel(x_ref, w_ref, *refs, tm):
    outs, x_planes, x_perm = refs[:A_GROUPS], refs[A_GROUPS], refs[A_GROUPS + 1]
    x = x_ref[0]
    n_planes = x.shape[1] // LANES
    for h in range(n_planes):
        x_planes[h] = x[:, h * LANES:(h + 1) * LANES]
    for g, dil in enumerate(A_DILATIONS):
        n = tm // dil
        if dil == 1:
            xb = x.astype(BF16)
        else:
            xp = x_perm.at[g - 1]
            for r in range(dil):
                for h in range(n_planes):
                    xp[r * n:(r + 1) * n, h * LANES:(h + 1) * LANES] = x_planes[h, pl.ds(r, n, stride=dil), :]
            xb = xp[...].astype(BF16)
        for c in range(3):
            n0 = (g * 3 + c) * A_WIDTH
            cols = slice(c * A_WIDTH, (c + 1) * A_WIDTH)
            res = _dot(xb, w_ref[:, n0:n0 + A_WIDTH]).astype(BF16)
            for r in range(dil):
                outs[g][0, r, :, cols] = res[r * n:(r + 1) * n]


def _qkv_proj(x3d, w_bf16, tm):
    b, s, d = x3d.shape
    width = 3 * A_WIDTH
    return pl.pallas_call(
        functools.partial(_qkv_proj_kernel, tm=tm),
        grid=(b, s // tm),
        in_specs=[pl.BlockSpec((1, tm, d), lambda bi, i: (bi, i, 0)), _resident(w_bf16.shape)],
        out_specs=[pl.BlockSpec((1, dil, tm // dil, width), lambda bi, i: (bi, 0, i, 0))
                   for dil in A_DILATIONS],
        out_shape=[jax.ShapeDtypeStruct((b, dil, s // dil, width), BF16) for dil in A_DILATIONS],
        scratch_shapes=[pltpu.VMEM((d // LANES, tm, LANES), F32),
                        pltpu.VMEM((sum(dil > 1 for dil in A_DILATIONS), tm, d), F32)],
        compiler_params=_cparams(("parallel", "parallel")),
        name="qkv_proj",
    )(x3d, w_bf16)


def _t5_bucket(rel):
    nb = NUM_BUCKETS // 2
    max_exact = nb // 2
    ret = jnp.where(rel > 0, nb, 0)
    n = jnp.abs(rel)
    large = max_exact + (jnp.log(jnp.maximum(n, 1).astype(F32) / max_exact)
                         / math.log(MAX_DISTANCE / max_exact) * (nb - max_exact)).astype(jnp.int32)
    large = jnp.minimum(large, nb - 1)
    return ret + jnp.where(n < max_exact, n, large)


def _band_bucket_index():
    qi = jnp.arange(2 * A_HALF)[:, None]
    kj = jnp.arange(4 * A_HALF)[None, :]
    rel = kj - A_HALF - qi
    tiles = [jnp.where(jnp.abs(rel) <= A_HALF, _t5_bucket(rel * dil), -1) for dil in A_DILATIONS]
    return jnp.stack(tiles, axis=0).astype(jnp.int32)


def _bias_tile_kernel(tab_ref, idx_ref, o_ref):
    col = pl.program_id(0) * A_HEADS + pl.program_id(1)
    idx = idx_ref[0]
    acc = jnp.full(idx.shape, NEG_INF, F32)
    for bucket in range(NUM_BUCKETS):
        acc = jnp.where(idx == bucket, tab_ref[bucket, col], acc)
    o_ref[0, 0] = acc


def _bias_tiles(rel_bias):
    idx = _band_bucket_index()
    tq, tk = idx.shape[1:]
    return pl.pallas_call(
        _bias_tile_kernel,
        grid=(A_GROUPS, A_HEADS),
        in_specs=[pl.BlockSpec(memory_space=pltpu.SMEM),
                  pl.BlockSpec((1, tq, tk), lambda g, h: (g, 0, 0))],
        out_specs=pl.BlockSpec((1, 1, tq, tk), lambda g, h: (g, h, 0, 0)),
        out_shape=jax.ShapeDtypeStruct((A_GROUPS, A_HEADS, tq, tk), F32),
        compiler_params=_cparams(("arbitrary", "arbitrary")),
        name="bias_tiles",
    )(rel_bias, idx)


def _dilated_attn_kernel(q_ref, kl_ref, km_ref, kr_ref, vl_ref, vm_ref, vr_ref, bias_ref,
                         o_ref, lse_ref, kbuf, vbuf, *, tq, sub_len):
    li = pl.program_id(2)
    kbuf[0:A_HALF] = kl_ref[0, 0]
    kbuf[A_HALF:A_HALF + tq] = km_ref[0, 0]
    kbuf[A_HALF + tq:] = kr_ref[0, 0]
    vbuf[0:A_HALF] = vl_ref[0, 0]
    vbuf[A_HALF:A_HALF + tq] = vm_ref[0, 0]
    vbuf[A_HALF + tq:] = vr_ref[0, 0]

    scale = A_HEAD_DIM ** -0.5
    qt, kt = 2 * A_HALF, 4 * A_HALF
    lane = lax.broadcasted_iota(jnp.int32, (qt, LANES), 1)
    for t in range(tq // qt):
        pos = li * tq + t * qt - A_HALF + lax.broadcasted_iota(jnp.int32, (1, kt), 1)
        edge = jnp.where((pos >= 0) & (pos < sub_len), 0.0, NEG_INF).astype(F32)
        lse_tile = jnp.zeros((qt, LANES), F32)
        for h in range(A_HEADS):
            cols = slice(h * A_HEAD_DIM, (h + 1) * A_HEAD_DIM)
            q = q_ref[0, 0, t * qt:(t + 1) * qt, cols]
            k = kbuf[t * qt:t * qt + kt, cols]
            v = vbuf[t * qt:t * qt + kt, cols]
            s = _dot_nt(q, k) * scale + bias_ref[0, h] + edge
            m = jnp.max(s, axis=-1, keepdims=True)
            p = jnp.exp(s - m)
            den = jnp.sum(p, axis=-1, keepdims=True)
            o = _dot(p.astype(BF16), v) / den
            o_ref[0, 0, t * qt:(t + 1) * qt, cols] = o.astype(BF16)
            lse_tile = jnp.where(lane == h, m + jnp.log(den), lse_tile)
        lse_ref[0, 0, t * qt:(t + 1) * qt, :] = lse_tile


def _dilated_attn(qkv_g, bias, g):
    batch, dil, sub_len, _ = qkv_g.shape
    tq = min(DILATED_Q_TILE, sub_len)
    n_t = sub_len // tq
    hb = tq // A_HALF
    n_hb = sub_len // A_HALF

    main = lambda which: pl.BlockSpec((1, 1, tq, A_WIDTH), lambda b, r, i: (b, r, i, which))
    left = lambda which: pl.BlockSpec(
        (1, 1, A_HALF, A_WIDTH), lambda b, r, i: (b, r, jnp.maximum(i * hb - 1, 0), which))
    right = lambda which: pl.BlockSpec(
        (1, 1, A_HALF, A_WIDTH), lambda b, r, i: (b, r, jnp.minimum((i + 1) * hb, n_hb - 1), which))

    return pl.pallas_call(
        functools.partial(_dilated_attn_kernel, tq=tq, sub_len=sub_len),
        grid=(batch, dil, n_t),
        in_specs=[main(0), left(1), main(1), right(1), left(2), main(2), right(2),
                  pl.BlockSpec((1, A_HEADS, 2 * A_HALF, 4 * A_HALF), lambda b, r, i: (g, 0, 0, 0))],
        out_specs=[pl.BlockSpec((1, 1, tq, A_WIDTH), lambda b, r, i: (b, r, i, 0)),
                   pl.BlockSpec((1, 1, tq, LANES), lambda b, r, i: (b, r, i, 0))],
        out_shape=[jax.ShapeDtypeStruct((batch, dil, sub_len, A_WIDTH), BF16),
                   jax.ShapeDtypeStruct((batch, dil, sub_len, LANES), F32)],
        scratch_shapes=[pltpu.VMEM((tq + 2 * A_HALF, A_WIDTH), BF16),
                        pltpu.VMEM((tq + 2 * A_HALF, A_WIDTH), BF16)],
        compiler_params=_cparams(("parallel", "parallel", "parallel")),
        name=f"dilated_attn_g{g}",
    )(qkv_g, qkv_g, qkv_g, qkv_g, qkv_g, qkv_g, qkv_g, bias)


def _merge_proj_ln_kernel(o0_ref, o1_ref, o2_ref, l0_ref, l1_ref, l2_ref, x_ref, w_ref, g_ref, b_ref,
                          out_ref, merged, o_scr, l_scr, *, tm):
    for g, (o_ref, l_ref, dil) in enumerate(zip((o0_ref, o1_ref, o2_ref), (l0_ref, l1_ref, l2_ref),
                                                A_DILATIONS)):
        for r in range(dil):
            rows = pl.ds(r, tm // dil, stride=dil) if dil > 1 else slice(None)
            l_scr[g, rows, :] = l_ref[0, r]
            for h in range(A_HEADS):
                o_scr[g, h, rows, :] = o_ref[0, r, :, h * A_HEAD_DIM:(h + 1) * A_HEAD_DIM].astype(F32)
    l0, l1, l2 = l_scr[0], l_scr[1], l_scr[2]
    m = jnp.maximum(jnp.maximum(l0, l1), l2)
    e0, e1, e2 = jnp.exp(l0 - m), jnp.exp(l1 - m), jnp.exp(l2 - m)
    den = e0 + e1 + e2
    w0, w1, w2 = e0 / den, e1 / den, e2 / den
    for h in range(A_HEADS):
        cols = slice(h * A_HEAD_DIM, (h + 1) * A_HEAD_DIM)
        mix = (w0[:, h:h + 1] * o_scr[0, h] + w1[:, h:h + 1] * o_scr[1, h]
               + w2[:, h:h + 1] * o_scr[2, h])
        merged[:, cols] = mix.astype(BF16)
    y = ALPHA * x_ref[0] + _dot(merged[...], w_ref[...])
    out_ref[0] = _layer_norm(y, g_ref[...], b_ref[...])


def _merge_proj_ln(os, lses, x3d, w_bf16, ln_g, ln_b, tm):
    b, s, d = x3d.shape
    plane = lambda w: [pl.BlockSpec((1, dil, tm // dil, w), lambda bi, i: (bi, 0, i, 0))
                       for dil in A_DILATIONS]
    row = pl.BlockSpec((1, tm, d), lambda bi, i: (bi, i, 0))
    return pl.pallas_call(
        functools.partial(_merge_proj_ln_kernel, tm=tm),
        grid=(b, s // tm),
        in_specs=plane(A_WIDTH) + plane(LANES) + [row, _resident((A_WIDTH, d)),
                                                  _resident((1, d)), _resident((1, d))],
        out_specs=row,
        out_shape=jax.ShapeDtypeStruct((b, s, d), F32),
        scratch_shapes=[pltpu.VMEM((tm, A_WIDTH), BF16),
                        pltpu.VMEM((A_GROUPS, A_HEADS, tm, A_HEAD_DIM), F32),
                        pltpu.VMEM((A_GROUPS, tm, LANES), F32)],
        compiler_params=_cparams(("parallel", "parallel")),
        name="merge_proj_ln",
    )(*os, *lses, x3d, w_bf16, ln_g, ln_b)


def _ffn_ln_kernel(x_ref, xp_ref, xn_ref, win_ref, cw_ref, cb_ref, wout_ref, g_ref, b_ref,
                   out_ref, act, *, tm):
    si = pl.program_id(1)
    n_s = pl.num_programs(1)
    x = x_ref[0]
    xb = x.astype(BF16)
    prev_ok = (si > 0).astype(F32)
    next_ok = (si < n_s - 1).astype(F32)
    xe = jnp.concatenate([xp_ref[0] * prev_ok, xn_ref[0] * next_ok], axis=0).astype(BF16)
    row = lax.broadcasted_iota(jnp.int32, (tm, 2 * FF_CHUNK), 0)
    for c in range(D_FF // FF_CHUNK):
        cols = slice(c * 2 * FF_CHUNK, (c + 1) * 2 * FF_CHUNK)
        w = win_ref[:, cols]
        h = _dot(xb, w)
        he = _dot(xe, w)
        up = jnp.where(row == 0, he[SUBLANES - 1:SUBLANES], pltpu.roll(h, 1, 0))
        dn = jnp.where(row == tm - 1, he[SUBLANES:SUBLANES + 1], pltpu.roll(h, tm - 1, 0))
        cw = cw_ref[:, cols]
        hc = up * cw[0:1] + h * cw[1:2] + dn * cw[2:3] + cb_ref[:, cols]
        a = hc[:, :FF_CHUNK]
        gate = hc[:, FF_CHUNK:]
        gelu = 0.5 * gate * (1.0 + lax.erf(gate * np.float32(math.sqrt(0.5))))
        act[:, c * FF_CHUNK:(c + 1) * FF_CHUNK] = (a * gelu).astype(BF16)
    y = ALPHA * x + _dot(act[...], wout_ref[...])
    out_ref[0] = _layer_norm(y, g_ref[...], b_ref[...])


def _chunk_interleave(w):
    lead = w.shape[:-1]
    w = w.reshape(lead + (2, D_FF // FF_CHUNK, FF_CHUNK))
    w = jnp.swapaxes(w, -3, -2)
    return w.reshape(lead + (2 * D_FF,))


def _ffn_ln(x3d, w_in, conv_w, conv_b, w_out, ln_g, ln_b, tm):
    b, s, d = x3d.shape
    nb8 = tm // SUBLANES
    n8 = s // SUBLANES
    win = _chunk_interleave(w_in).astype(BF16)
    cw = _chunk_interleave(conv_w)
    cb = _chunk_interleave(conv_b)[None]
    return pl.pallas_call(
        functools.partial(_ffn_ln_kernel, tm=tm),
        grid=(b, s // tm),
        in_specs=[pl.BlockSpec((1, tm, d), lambda bi, si: (bi, si, 0)),
                  pl.BlockSpec((1, SUBLANES, d), lambda bi, si: (bi, jnp.maximum(si * nb8 - 1, 0), 0)),
                  pl.BlockSpec((1, SUBLANES, d), lambda bi, si: (bi, jnp.minimum((si + 1) * nb8, n8 - 1), 0)),
                  _resident((d, 2 * D_FF)), _resident((3, 2 * D_FF)), _resident((1, 2 * D_FF)),
                  _resident((D_FF, d)), _resident((1, d)), _resident((1, d))],
        out_specs=pl.BlockSpec((1, tm, d), lambda bi, si: (bi, si, 0)),
        out_shape=jax.ShapeDtypeStruct((b, s, d), F32),
        scratch_shapes=[pltpu.VMEM((tm, D_FF), BF16)],
        compiler_params=_cparams(("parallel", "parallel")),
        name="ffn_ln",
    )(x3d, x3d, x3d, win, cw, cb, w_out.astype(BF16), ln_g, ln_b)


def _rope_tables(seq):
    inv = 1.0 / (ROPE_THETA ** (jnp.arange(0, QK_ROPE, 2, dtype=F32) / QK_ROPE))
    ang = jnp.arange(seq, dtype=F32)[:, None] * inv[None, :]
    cos, sin = jnp.cos(ang), jnp.sin(ang)
    half = QK_ROPE // 2
    ones = jnp.ones((seq, QK_NOPE), F32)
    z_nope = jnp.zeros((seq, QK_NOPE), F32)
    z_half = jnp.zeros((seq, half), F32)
    z_pad = jnp.zeros((seq, HEAD_PAD - QK_NOPE - QK_ROPE), F32)
    cmul = jnp.concatenate([ones, cos, cos, z_pad], axis=1)
    s_first = jnp.concatenate([z_nope, -sin, z_half, z_pad], axis=1)
    s_second = jnp.concatenate([z_nope, z_half, sin, z_pad], axis=1)
    return cmul, s_first, s_second


def _rope(t, cmul, s_first, s_second):
    half = QK_ROPE // 2
    return (t * cmul + pltpu.roll(t, HEAD_PAD - half, 1) * s_first
            + pltpu.roll(t, half, 1) * s_second)


def _rms_norm(x, g):
    ms = jnp.mean(x * x, axis=-1, keepdims=True)
    return x * lax.rsqrt(ms + RMS_EPS) * g


def _mla_proj_kernel(x_ref, wd_ref, gq_ref, gkv_ref, wuqt_ref, wuk_ref, wuvt_ref,
                     cm_ref, s1_ref, s2_ref, cmt_ref, s1t_ref, s2t_ref, qt_ref, k_ref, vt_ref):
    xb = x_ref[0].astype(BF16)
    c = _dot(xb, wd_ref[...])
    cqn = _rms_norm(c[:, :Q_LORA], gq_ref[...]).astype(BF16)
    ckvn = _rms_norm(c[:, Q_LORA:Q_LORA + KV_LORA], gkv_ref[...]).astype(BF16)
    cm, s1, s2 = cm_ref[...], s1_ref[...], s2_ref[...]
    kr = _rope(c[:, Q_LORA + KV_LORA:], cm, s1, s2)
    k = _dot(ckvn, wuk_ref[...])
    for h in range(B_HEADS):
        cols = slice(h * HEAD_PAD, (h + 1) * HEAD_PAD)
        k_ref[0, :, cols] = (k[:, cols] + kr).astype(BF16)
    qt = _dot_nt(wuqt_ref[...], cqn)
    cmt, s1t, s2t = cmt_ref[...], s1t_ref[...], s2t_ref[...]
    half = QK_ROPE // 2
    for h in range(B_HEADS):
        rows = slice(h * HEAD_PAD, (h + 1) * HEAD_PAD)
        t = qt[rows]
        rot = t * cmt + pltpu.roll(t, HEAD_PAD - half, 0) * s1t + pltpu.roll(t, half, 0) * s2t
        qt_ref[0, rows, :] = (rot * QK_SCALE_LOG2E).astype(BF16)
    vt = _dot_nt(wuvt_ref[...], ckvn).astype(BF16)
    tm = vt.shape[1]
    row = lax.broadcasted_iota(jnp.int32, (V_ROWS - V_HEAD, tm), 0)
    ones_row = jnp.where(row == 0, 1.0, 0.0).astype(BF16)
    for h in range(B_HEADS):
        vt_ref[0, h * V_ROWS:h * V_ROWS + V_HEAD, :] = vt[h * V_HEAD:(h + 1) * V_HEAD]
        vt_ref[0, h * V_ROWS + V_HEAD:(h + 1) * V_ROWS, :] = ones_row


def _mla_weights(w_dkv, w_uq, w_ukv):
    d = w_dkv.shape[0]
    pad_l = jnp.zeros((d, QK_NOPE), F32)
    pad_r = jnp.zeros((d, HEAD_PAD - QK_NOPE - QK_ROPE), F32)
    wd = jnp.concatenate([w_dkv[:, :Q_LORA + KV_LORA], pad_l, w_dkv[:, Q_LORA + KV_LORA:], pad_r], axis=1)
    wq = w_uq.reshape(Q_LORA, B_HEADS, QK_NOPE + QK_ROPE)
    wq = jnp.pad(wq, ((0, 0), (0, 0), (0, HEAD_PAD - QK_NOPE - QK_ROPE))).reshape(Q_LORA, B_HEADS * HEAD_PAD)
    wkv = w_ukv.reshape(KV_LORA, B_HEADS, QK_NOPE + V_HEAD)
    wk = jnp.pad(wkv[:, :, :QK_NOPE], ((0, 0), (0, 0), (0, HEAD_PAD - QK_NOPE))).reshape(KV_LORA, B_HEADS * HEAD_PAD)
    wvt = wkv[:, :, QK_NOPE:].reshape(KV_LORA, B_HEADS * V_HEAD).T
    return wd.astype(BF16), wq.T.astype(BF16), wk.astype(BF16), wvt.astype(BF16)


def _mla_proj(x3d, w_dkv, g_q, g_kv, w_uq, w_ukv, tm):
    b, s, d = x3d.shape
    wd, wqt, wk, wvt = _mla_weights(w_dkv, w_uq, w_ukv)
    tabs = _rope_tables(s)
    tabs_t = [t.T for t in tabs]
    hq = B_HEADS * HEAD_PAD
    hv = B_HEADS * V_ROWS
    tab = pl.BlockSpec((tm, HEAD_PAD), lambda bi, si: (si, 0))
    tab_t = pl.BlockSpec((HEAD_PAD, tm), lambda bi, si: (0, si))
    return pl.pallas_call(
        _mla_proj_kernel,
        grid=(b, s // tm),
        in_specs=[pl.BlockSpec((1, tm, d), lambda bi, si: (bi, si, 0)),
                  _resident(wd.shape), _resident((1, Q_LORA)), _resident((1, KV_LORA)),
                  _resident(wqt.shape), _resident(wk.shape), _resident(wvt.shape),
                  tab, tab, tab, tab_t, tab_t, tab_t],
        out_specs=[pl.BlockSpec((1, hq, tm), lambda bi, si: (bi, 0, si)),
                   pl.BlockSpec((1, tm, hq), lambda bi, si: (bi, si, 0)),
                   pl.BlockSpec((1, hv, tm), lambda bi, si: (bi, 0, si))],
        out_shape=[jax.ShapeDtypeStruct((b, hq, s), BF16),
                   jax.ShapeDtypeStruct((b, s, hq), BF16),
                   jax.ShapeDtypeStruct((b, hv, s), BF16)],
        compiler_params=_cparams(("parallel", "parallel")),
        name="mla_proj",
    )(x3d, wd, g_q[None], g_kv[None], wqt, wk, wvt, *tabs, *tabs_t)


def _mla_attn_kernel(qt_ref, k_ref, vt_ref, o_ref, s_scr, cmax_scr, m_all, acc_all, *, tq, tk, n_q, unroll):
    seq = k_ref.shape[1]
    nk = seq // tk
    blk_q = n_q * tq
    n_blk = seq // blk_q
    total = n_blk * nk
    trips_per_blk = nk // unroll

    def finalize(blk):
        par = blk % 2
        for j in range(n_q):
            acc = acc_all[par, j]
            q0 = pl.multiple_of(blk * blk_q + j * tq, tq)
            o_ref[0, :, pl.ds(q0, tq)] = (acc[:V_HEAD] / acc[V_HEAD:V_HEAD + 1]).astype(BF16)

    def run_step(t, t_static, do_qk=True):
        s_qk, s_sm = t_static % N_SLOTS, (t_static - 1) % N_SLOTS
        n_sm = t - 1 + nk
        par_sm, first_sm = (n_sm // nk + 1) % 2, (n_sm % nk) == 0
        v0 = pl.multiple_of((n_sm % nk) * tk, tk)
        vt = vt_ref[0, :, pl.ds(v0, tk)]
        if do_qk:
            k0 = pl.multiple_of((t % nk) * tk, tk)
            k = k_ref[0, pl.ds(k0, tk), :]
            for j in range(n_q):
                q0 = pl.multiple_of((t // nk) * blk_q + j * tq, tq)
                sc = _dot(k, qt_ref[0, :, pl.ds(q0, tq)])
                s_scr[s_qk, j] = sc
                cmax_scr[s_qk, j] = jnp.max(sc, axis=0, keepdims=True)
        for j in range(n_q):
            m = jnp.where(first_sm, NEG_INF, m_all[par_sm, j])
            m_new = jnp.maximum(m, cmax_scr[s_sm, j])
            alpha = jnp.exp2(m - m_new)
            m_all[par_sm, j] = m_new
            p = jnp.exp2(s_scr[s_sm, j] - m_new).astype(BF16)
            acc_all[par_sm, j] = alpha * acc_all[par_sm, j] + _dot(vt, p)

    s_scr[...] = jnp.zeros(s_scr.shape, F32)
    cmax_scr[...] = jnp.zeros(cmax_scr.shape, F32)
    m_all[...] = jnp.full(m_all.shape, NEG_INF, F32)
    acc_all[...] = jnp.zeros(acc_all.shape, F32)

    def trip(u, carry):
        @pl.when((u % trips_per_blk == 1) & (u > trips_per_blk))
        def _():
            finalize(u // trips_per_blk - 1)

        for i in range(unroll):
            run_step(unroll * u + i, i)
        return carry

    lax.fori_loop(0, total // unroll, trip, 0)
    run_step(total, total, do_qk=False)
    finalize(n_blk - 1)


def _mla_attn(qt, k, vt, n_q, tq, tk):
    b, s, _ = k.shape
    nk = s // tk
    unroll = max(u for u in range(N_SLOTS, MAX_UNROLL + 1, N_SLOTS) if nk % u == 0 and nk // u >= 2)
    assert s % (n_q * tq) == 0 and N_SLOTS == 2
    return pl.pallas_call(
        functools.partial(_mla_attn_kernel, tq=tq, tk=tk, n_q=n_q, unroll=unroll),
        grid=(b, B_HEADS),
        in_specs=[pl.BlockSpec((1, HEAD_PAD, s), lambda bi, h: (bi, h, 0)),
                  pl.BlockSpec((1, s, HEAD_PAD), lambda bi, h: (bi, 0, h)),
                  pl.BlockSpec((1, V_ROWS, s), lambda bi, h: (bi, h, 0))],
        out_specs=pl.BlockSpec((1, V_HEAD, s), lambda bi, h: (bi, h, 0)),
        out_shape=jax.ShapeDtypeStruct((b, B_HEADS * V_HEAD, s), BF16),
        scratch_shapes=[pltpu.VMEM((N_SLOTS, n_q, tk, tq), F32), pltpu.VMEM((N_SLOTS, n_q, 1, tq), F32),
                        pltpu.VMEM((2, n_q, 1, tq), F32), pltpu.VMEM((2, n_q, V_ROWS, tq), F32)],
        compiler_params=_cparams(("parallel", "parallel")),
        name="mla_attn",
    )(qt, k, vt)


def _proj_t_ln_kernel(ot_ref, x_ref, w_ref, g_ref, b_ref, out_ref):
    y = ALPHA * x_ref[0] + _dot_tn(ot_ref[0], w_ref[...])
    out_ref[0] = _layer_norm(y, g_ref[...], b_ref[...])


def _proj_t_ln(ot, x3d, w_bf16, ln_g, ln_b, tm):
    b, s, d = x3d.shape
    k = ot.shape[1]
    return pl.pallas_call(
        _proj_t_ln_kernel,
        grid=(b, s // tm),
        in_specs=[pl.BlockSpec((1, k, tm), lambda bi, si: (bi, 0, si)),
                  pl.BlockSpec((1, tm, d), lambda bi, si: (bi, si, 0)),
                  _resident((k, d)), _resident((1, d)), _resident((1, d))],
        out_specs=pl.BlockSpec((1, tm, d), lambda bi, si: (bi, si, 0)),
        out_shape=jax.ShapeDtypeStruct((b, s, d), F32),
        compiler_params=_cparams(("parallel", "parallel")),
        name="proj_t_ln",
    )(ot, x3d, w_bf16, ln_g, ln_b)


def _encoder(x, bias, w_qkv_a, w_o_a, w_dkv_b, g_q_b, g_kv_b, w_uq_b, w_ukv_b, w_o_b,
             ffn_w_in, ffn_conv_w, ffn_conv_b, ffn_w_out, ln_g, ln_b):
    b, s, d = x.shape
    tm = ROW_TILE
    for i in range(DEPTH):
        j = i // N_MIXERS
        g1, b1 = ln_g[i, 0][None], ln_b[i, 0][None]
        g2, b2 = ln_g[i, 1][None], ln_b[i, 1][None]
        if i % N_MIXERS == 0:
            qkv = _qkv_proj(x, w_qkv_a[j].astype(BF16), QKV_ROW_TILE)
            outs = [_dilated_attn(qkv[g], bias, g) for g in range(A_GROUPS)]
            x = _merge_proj_ln([o for o, _ in outs], [l for _, l in outs], x,
                               w_o_a[j].astype(BF16), g1, b1, tm)
        else:
            q, k, vt = _mla_proj(x, w_dkv_b[j], g_q_b[j], g_kv_b[j], w_uq_b[j], w_ukv_b[j], tm)
            ot = _mla_attn(q, k, vt, MLA_Q_TILES, MLA_Q_TILE, MLA_K_TILE)
            x = _proj_t_ln(ot, x, w_o_b[j].astype(BF16), g1, b1, tm)
        x = _ffn_ln(x, ffn_w_in[i], ffn_conv_w[i], ffn_conv_b[i], ffn_w_out[i], g2, b2, tm)
    return x


def kernel(x_prompt, x_sample, rel_bias, w_qkv_a, w_o_a, w_dkv_b, g_q_b, g_kv_b, w_uq_b, w_ukv_b, w_o_b,
           ffn_w_in, ffn_conv_w, ffn_conv_b, ffn_w_out, ln_g, ln_b):
    bias = _bias_tiles(rel_bias)
    args = (bias, w_qkv_a, w_o_a, w_dkv_b, g_q_b, g_kv_b, w_uq_b, w_ukv_b, w_o_b,
            ffn_w_in, ffn_conv_w, ffn_conv_b, ffn_w_out, ln_g, ln_b)
    return (_encoder(x_prompt, *args), _encoder(x_sample, *args))
```

```python
import functools
import math

import jax
import jax.numpy as jnp
import numpy as np
from jax import lax
from jax.experimental import pallas as pl
from jax.experimental.pallas import tpu as pltpu

F32 = jnp.float32
BF16 = jnp.bfloat16

D_MODEL = 1024
DEPTH = 2
N_MIXERS = 2

A_WINDOWS = (128, 512, 2048)
A_DILATIONS = (1, 4, 16)
A_GROUPS = 3
A_HEADS = 8
A_HEAD_DIM = 128
A_WIDTH = A_HEADS * A_HEAD_DIM
A_HALF = 64
NUM_BUCKETS = 32
MAX_DISTANCE = max(A_WINDOWS) // 2

B_HEADS = 16
Q_LORA = 384
KV_LORA = 256
QK_NOPE = 64
QK_ROPE = 32
V_HEAD = 64
ROPE_THETA = 10000.0
HEAD_PAD = 128
V_ROWS = V_HEAD + 16
N_SLOTS = 2
MAX_UNROLL = 8

D_FF = 2816
FF_CHUNK = 256

ALPHA = (2.0 * DEPTH) ** 0.25
LN_EPS = 1e-5
RMS_EPS = 1e-6
NEG_INF = -1e30

LANES = 128
SUBLANES = 8
VMEM_LIMIT = 56 * 1024 * 1024

ROW_TILE = 512
QKV_ROW_TILE = 256
DILATED_Q_TILE = 1024
MLA_Q_TILES = 2
MLA_Q_TILE = 512
MLA_K_TILE = 256

QK_SCALE_LOG2E = np.float32((QK_NOPE + QK_ROPE) ** -0.5 * math.log2(math.e))


def _cparams(sem):
    return pltpu.CompilerParams(dimension_semantics=sem, vmem_limit_bytes=VMEM_LIMIT)


def _resident(shape):
    nd = len(shape)
    return pl.BlockSpec(shape, lambda *_: (0,) * nd, pipeline_mode=pl.Buffered(1))


def _layer_norm(y, g, b):
    mu = jnp.mean(y, axis=-1, keepdims=True)
    yc = y - mu
    var = jnp.mean(yc * yc, axis=-1, keepdims=True)
    return yc * lax.rsqrt(var + LN_EPS) * g + b


def _dot(a, b):
    return jnp.dot(a, b, preferred_element_type=F32)


def _dot_nt(a, b):
    return lax.dot_general(a, b, (((1,), (1,)), ((), ())), preferred_element_type=F32)


def _dot_tn(a, b):
    return lax.dot_general(a, b, (((0,), (0,)), ((), ())), preferred_element_type=F32)


def _qkv_proj_kernel(x_ref, w_ref, *refs, tm):
    outs, x_planes, x_perm = refs[:A_GROUPS], refs[A_GROUPS], refs[A_GROUPS + 1]
    x = x_ref[0]
    n_planes = x.shape[1] // LANES
    for h in range(n_planes):
        x_planes[h] = x[:, h * LANES:(h + 1) * LANES]
    for g, dil in enumerate(A_DILATIONS):
        n = tm // dil
        if dil == 1:
            xb = x.astype(BF16)
        else:
            xp = x_perm.at[g - 1]
            for r in range(dil):
                for h in range(n_planes):
                    xp[r * n:(r + 1) * n, h * LANES:(h + 1) * LANES] = x_planes[h, pl.ds(r, n, stride=dil), :]
            xb = xp[...].astype(BF16)
        for c in range(3):
            n0 = (g * 3 + c) * A_WIDTH
            cols = slice(c * A_WIDTH, (c + 1) * A_WIDTH)
            res = _dot(xb, w_ref[:, n0:n0 + A_WIDTH]).astype(BF16)
            for r in range(dil):
                outs[g][0, r, :, cols] = res[r * n:(r + 1) * n]


def _qkv_proj(x3d, w_bf16, tm):
    b, s, d = x3d.shape
    width = 3 * A_WIDTH
    return pl.pallas_call(
        functools.partial(_qkv_proj_kernel, tm=tm),
        grid=(b, s // tm),
        in_specs=[pl.BlockSpec((1, tm, d), lambda bi, i: (bi, i, 0)), _resident(w_bf16.shape)],
        out_specs=[pl.BlockSpec((1, dil, tm // dil, width), lambda bi, i: (bi, 0, i, 0))
                   for dil in A_DILATIONS],
        out_shape=[jax.ShapeDtypeStruct((b, dil, s // dil, width), BF16) for dil in A_DILATIONS],
        scratch_shapes=[pltpu.VMEM((d // LANES, tm, LANES), F32),
                        pltpu.VMEM((sum(dil > 1 for dil in A_DILATIONS), tm, d), F32)],
        compiler_params=_cparams(("parallel", "parallel")),
        name="qkv_proj",
    )(x3d, w_bf16)


def _t5_bucket(rel):
    nb = NUM_BUCKETS // 2
    max_exact = nb // 2
    ret = jnp.where(rel > 0, nb, 0)
    n = jnp.abs(rel)
    large = max_exact + (jnp.log(jnp.maximum(n, 1).astype(F32) / max_exact)
                         / math.log(MAX_DISTANCE / max_exact) * (nb - max_exact)).astype(jnp.int32)
    large = jnp.minimum(large, nb - 1)
    return ret + jnp.where(n < max_exact, n, large)


def _band_bucket_index():
    qi = jnp.arange(2 * A_HALF)[:, None]
    kj = jnp.arange(4 * A_HALF)[None, :]
    rel = kj - A_HALF - qi
    tiles = [jnp.where(jnp.abs(rel) <= A_HALF, _t5_bucket(rel * dil), -1) for dil in A_DILATIONS]
    return jnp.stack(tiles, axis=0).astype(jnp.int32)


def _bias_tile_kernel(tab_ref, idx_ref, o_ref):
    col = pl.program_id(0) * A_HEADS + pl.program_id(1)
    idx = idx_ref[0]
    acc = jnp.full(idx.shape, NEG_INF, F32)
    for bucket in range(NUM_BUCKETS):
        acc = jnp.where(idx == bucket, tab_ref[bucket, col], acc)
    o_ref[0, 0] = acc


def _bias_tiles(rel_bias):
    idx = _band_bucket_index()
    tq, tk = idx.shape[1:]
    return pl.pallas_call(
        _bias_tile_kernel,
        grid=(A_GROUPS, A_HEADS),
        in_specs=[pl.BlockSpec(memory_space=pltpu.SMEM),
                  pl.BlockSpec((1, tq, tk), lambda g, h: (g, 0, 0))],
        out_specs=pl.BlockSpec((1, 1, tq, tk), lambda g, h: (g, h, 0, 0)),
        out_shape=jax.ShapeDtypeStruct((A_GROUPS, A_HEADS, tq, tk), F32),
        compiler_params=_cparams(("arbitrary", "arbitrary")),
        name="bias_tiles",
    )(rel_bias, idx)


def _dilated_attn_kernel(q_ref, kl_ref, km_ref, kr_ref, vl_ref, vm_ref, vr_ref, bias_ref,
                         o_ref, lse_ref, kbuf, vbuf, *, tq, sub_len):
    li = pl.program_id(2)
    kbuf[0:A_HALF] = kl_ref[0, 0]
    kbuf[A_HALF:A_HALF + tq] = km_ref[0, 0]
    kbuf[A_HALF + tq:] = kr_ref[0, 0]
    vbuf[0:A_HALF] = vl_ref[0, 0]
    vbuf[A_HALF:A_HALF + tq] = vm_ref[0, 0]
    vbuf[A_HALF + tq:] = vr_ref[0, 0]

    scale = A_HEAD_DIM ** -0.5
    qt, kt = 2 * A_HALF, 4 * A_HALF
    lane = lax.broadcasted_iota(jnp.int32, (qt, LANES), 1)
    for t in range(tq // qt):
        pos = li * tq + t * qt - A_HALF + lax.broadcasted_iota(jnp.int32, (1, kt), 1)
        edge = jnp.where((pos >= 0) & (pos < sub_len), 0.0, NEG_INF).astype(F32)
        lse_tile = jnp.zeros((qt, LANES), F32)
        for h in range(A_HEADS):
            cols = slice(h * A_HEAD_DIM, (h + 1) * A_HEAD_DIM)
            q = q_ref[0, 0, t * qt:(t + 1) * qt, cols]
            k = kbuf[t * qt:t * qt + kt, cols]
            v = vbuf[t * qt:t * qt + kt, cols]
            s = _dot_nt(q, k) * scale + bias_ref[0, h] + edge
            m = jnp.max(s, axis=-1, keepdims=True)
            p = jnp.exp(s - m)
            den = jnp.sum(p, axis=-1, keepdims=True)
            o = _dot(p.astype(BF16), v) / den
            o_ref[0, 0, t * qt:(t + 1) * qt, cols] = o.astype(BF16)
            lse_tile = jnp.where(lane == h, m + jnp.log(den), lse_tile)
        lse_ref[0, 0, t * qt:(t + 1) * qt, :] = lse_tile


def _dilated_attn(qkv_g, bias, g):
    batch, dil, sub_len, _ = qkv_g.shape
    tq = min(DILATED_Q_TILE, sub_len)
    n_t = sub_len // tq
    hb = tq // A_HALF
    n_hb = sub_len // A_HALF

    main = lambda which: pl.BlockSpec((1, 1, tq, A_WIDTH), lambda b, r, i: (b, r, i, which))
    left = lambda which: pl.BlockSpec(
        (1, 1, A_HALF, A_WIDTH), lambda b, r, i: (b, r, jnp.maximum(i * hb - 1, 0), which))
    right = lambda which: pl.BlockSpec(
        (1, 1, A_HALF, A_WIDTH), lambda b, r, i: (b, r, jnp.minimum((i + 1) * hb, n_hb - 1), which))

    return pl.pallas_call(
        functools.partial(_dilated_attn_kernel, tq=tq, sub_len=sub_len),
        grid=(batch, dil, n_t),
        in_specs=[main(0), left(1), main(1), right(1), left(2), main(2), right(2),
                  pl.BlockSpec((1, A_HEADS, 2 * A_HALF, 4 * A_HALF), lambda b, r, i: (g, 0, 0, 0))],
        out_specs=[pl.BlockSpec((1, 1, tq, A_WIDTH), lambda b, r, i: (b, r, i, 0)),
                   pl.BlockSpec((1, 1, tq, LANES), lambda b, r, i: (b, r, i, 0))],
        out_shape=[jax.ShapeDtypeStruct((batch, dil, sub_len, A_WIDTH), BF16),
                   jax.ShapeDtypeStruct((batch, dil, sub_len, LANES), F32)],
        scratch_shapes=[pltpu.VMEM((tq + 2 * A_HALF, A_WIDTH), BF16),
                        pltpu.VMEM((tq + 2 * A_HALF, A_WIDTH), BF16)],
        compiler_params=_cparams(("parallel", "parallel", "parallel")),
        name=f"dilated_attn_g{g}",
    )(qkv_g, qkv_g, qkv_g, qkv_g, qkv_g, qkv_g, qkv_g, bias)


def _merge_proj_ln_kernel(o0_ref, o1_ref, o2_ref, l0_ref, l1_ref, l2_ref, x_ref, w_ref, g_ref, b_ref,
                          out_ref, merged, o_scr, l_scr, *, tm):
    for g, (o_ref, l_ref, dil) in enumerate(zip((o0_ref, o1_ref, o2_ref), (l0_ref, l1_ref, l2_ref),
                                                A_DILATIONS)):
        for r in range(dil):
            rows = pl.ds(r, tm // dil, stride=dil) if dil > 1 else slice(None)
            l_scr[g, rows, :] = l_ref[0, r]
            for h in range(A_HEADS):
                o_scr[g, h, rows, :] = o_ref[0, r, :, h * A_HEAD_DIM:(h + 1) * A_HEAD_DIM].astype(F32)
    l0, l1, l2 = l_scr[0], l_scr[1], l_scr[2]
    m = jnp.maximum(jnp.maximum(l0, l1), l2)
    e0, e1, e2 = jnp.exp(l0 - m), jnp.exp(l1 - m), jnp.exp(l2 - m)
    den = e0 + e1 + e2
    w0, w1, w2 = e0 / den, e1 / den, e2 / den
    for h in range(A_HEADS):
        cols = slice(h * A_HEAD_DIM, (h + 1) * A_HEAD_DIM)
        mix = (w0[:, h:h + 1] * o_scr[0, h] + w1[:, h:h + 1] * o_scr[1, h]
               + w2[:, h:h + 1] * o_scr[2, h])
        merged[:, cols] = mix.astype(BF16)
    y = ALPHA * x_ref[0] + _dot(merged[...], w_ref[...])
    out_ref[0] = _layer_norm(y, g_ref[...], b_ref[...])


def _merge_proj_ln(os, lses, x3d, w_bf16, ln_g, ln_b, tm):
    b, s, d = x3d.shape
    plane = lambda w: [pl.BlockSpec((1, dil, tm // dil, w), lambda bi, i: (bi, 0, i, 0))
                       for dil in A_DILATIONS]
    row = pl.BlockSpec((1, tm, d), lambda bi, i: (bi, i, 0))
    return pl.pallas_call(
        functools.partial(_merge_proj_ln_kernel, tm=tm),
        grid=(b, s // tm),
        in_specs=plane(A_WIDTH) + plane(LANES) + [row, _resident((A_WIDTH, d)),
                                                  _resident((1, d)), _resident((1, d))],
        out_specs=row,
        out_shape=jax.ShapeDtypeStruct((b, s, d), F32),
        scratch_shapes=[pltpu.VMEM((tm, A_WIDTH), BF16),
                        pltpu.VMEM((A_GROUPS, A_HEADS, tm, A_HEAD_DIM), F32),
                        pltpu.VMEM((A_GROUPS, tm, LANES), F32)],
        compiler_params=_cparams(("parallel", "parallel")),
        name="merge_proj_ln",
    )(*os, *lses, x3d, w_bf16, ln_g, ln_b)


def _ffn_ln_kernel(x_ref, xp_ref, xn_ref, win_ref, cw_ref, cb_ref, wout_ref, g_ref, b_ref,
                   out_ref, act, *, tm):
    si = pl.program_id(1)
    n_s = pl.num_programs(1)
    x = x_ref[0]
    xb = x.astype(BF16)
    prev_ok = (si > 0).astype(F32)
    next_ok = (si < n_s - 1).astype(F32)
    xe = jnp.concatenate([xp_ref[0] * prev_ok, xn_ref[0] * next_ok], axis=0).astype(BF16)
    xall = jnp.concatenate([xb, xe], axis=0)
    row = lax.broadcasted_iota(jnp.int32, (tm, 2 * FF_CHUNK), 0)
    for c in range(D_FF // FF_CHUNK):
        cols = slice(c * 2 * FF_CHUNK, (c + 1) * 2 * FF_CHUNK)
        hall = _dot(xall, win_ref[:, cols])
        h = hall[:tm]
        he = hall[tm:]
        up = jnp.where(row == 0, he[SUBLANES - 1:SUBLANES], pltpu.roll(h, 1, 0))
        dn = jnp.where(row == tm - 1, he[SUBLANES:SUBLANES + 1], pltpu.roll(h, tm - 1, 0))
        cw = cw_ref[:, cols]
        hc = up * cw[0:1] + h * cw[1:2] + dn * cw[2:3] + cb_ref[:, cols]
        a = hc[:, :FF_CHUNK]
        gate = hc[:, FF_CHUNK:]
        gelu = 0.5 * gate * (1.0 + lax.erf(gate * np.float32(math.sqrt(0.5))))
        act[:, c * FF_CHUNK:(c + 1) * FF_CHUNK] = (a * gelu).astype(BF16)
    y = ALPHA * x + _dot(act[...], wout_ref[...])
    out_ref[0] = _layer_norm(y, g_ref[...], b_ref[...])


def _chunk_interleave(w):
    lead = w.shape[:-1]
    w = w.reshape(lead + (2, D_FF // FF_CHUNK, FF_CHUNK))
    w = jnp.swapaxes(w, -3, -2)
    return w.reshape(lead + (2 * D_FF,))


def _ffn_ln(x3d, w_in, conv_w, conv_b, w_out, ln_g, ln_b, tm):
    b, s, d = x3d.shape
    nb8 = tm // SUBLANES
    n8 = s // SUBLANES
    win = _chunk_interleave(w_in).astype(BF16)
    cw = _chunk_interleave(conv_w)
    cb = _chunk_interleave(conv_b)[None]
    return pl.pallas_call(
        functools.partial(_ffn_ln_kernel, tm=tm),
        grid=(b, s // tm),
        in_specs=[pl.BlockSpec((1, tm, d), lambda bi, si: (bi, si, 0)),
                  pl.BlockSpec((1, SUBLANES, d), lambda bi, si: (bi, jnp.maximum(si * nb8 - 1, 0), 0)),
                  pl.BlockSpec((1, SUBLANES, d), lambda bi, si: (bi, jnp.minimum((si + 1) * nb8, n8 - 1), 0)),
                  _resident((d, 2 * D_FF)), _resident((3, 2 * D_FF)), _resident((1, 2 * D_FF)),
                  _resident((D_FF, d)), _resident((1, d)), _resident((1, d))],
        out_specs=pl.BlockSpec((1, tm, d), lambda bi, si: (bi, si, 0)),
        out_shape=jax.ShapeDtypeStruct((b, s, d), F32),
        scratch_shapes=[pltpu.VMEM((tm, D_FF), BF16)],
        compiler_params=_cparams(("parallel", "parallel")),
        name="ffn_ln",
    )(x3d, x3d, x3d, win, cw, cb, w_out.astype(BF16), ln_g, ln_b)


def _rope_tables(seq):
    inv = 1.0 / (ROPE_THETA ** (jnp.arange(0, QK_ROPE, 2, dtype=F32) / QK_ROPE))
    ang = jnp.arange(seq, dtype=F32)[:, None] * inv[None, :]
    cos, sin = jnp.cos(ang), jnp.sin(ang)
    half = QK_ROPE // 2
    ones = jnp.ones((seq, QK_NOPE), F32)
    z_nope = jnp.zeros((seq, QK_NOPE), F32)
    z_half = jnp.zeros((seq, half), F32)
    z_pad = jnp.zeros((seq, HEAD_PAD - QK_NOPE - QK_ROPE), F32)
    cmul = jnp.concatenate([ones, cos, cos, z_pad], axis=1)
    s_first = jnp.concatenate([z_nope, -sin, z_half, z_pad], axis=1)
    s_second = jnp.concatenate([z_nope, z_half, sin, z_pad], axis=1)
    return cmul, s_first, s_second


def _rope(t, cmul, s_first, s_second):
    half = QK_ROPE // 2
    return (t * cmul + pltpu.roll(t, HEAD_PAD - half, 1) * s_first
            + pltpu.roll(t, half, 1) * s_second)


def _rms_norm(x, g):
    ms = jnp.mean(x * x, axis=-1, keepdims=True)
    return x * lax.rsqrt(ms + RMS_EPS) * g


def _mla_proj_kernel(x_ref, wd_ref, gq_ref, gkv_ref, wuqt_ref, wuk_ref, wuvt_ref,
                     cm_ref, s1_ref, s2_ref, cmt_ref, s1t_ref, s2t_ref, qt_ref, k_ref, vt_ref):
    xb = x_ref[0].astype(BF16)
    c = _dot(xb, wd_ref[...])
    cqn = _rms_norm(c[:, :Q_LORA], gq_ref[...]).astype(BF16)
    ckvn = _rms_norm(c[:, Q_LORA:Q_LORA + KV_LORA], gkv_ref[...]).astype(BF16)
    cm, s1, s2 = cm_ref[...], s1_ref[...], s2_ref[...]
    kr = _rope(c[:, Q_LORA + KV_LORA:], cm, s1, s2)
    k = _dot(ckvn, wuk_ref[...])
    for h in range(B_HEADS):
        cols = slice(h * HEAD_PAD, (h + 1) * HEAD_PAD)
        k_ref[0, :, cols] = (k[:, cols] + kr).astype(BF16)
    qt = _dot_nt(wuqt_ref[...], cqn)
    cmt, s1t, s2t = cmt_ref[...], s1t_ref[...], s2t_ref[...]
    half = QK_ROPE // 2
    for h in range(B_HEADS):
        rows = slice(h * HEAD_PAD, (h + 1) * HEAD_PAD)
        t = qt[rows]
        rot = t * cmt + pltpu.roll(t, HEAD_PAD - half, 0) * s1t + pltpu.roll(t, half, 0) * s2t
        qt_ref[0, rows, :] = (rot * QK_SCALE_LOG2E).astype(BF16)
    vt = _dot_nt(wuvt_ref[...], ckvn).astype(BF16)
    tm = vt.shape[1]
    row = lax.broadcasted_iota(jnp.int32, (V_ROWS - V_HEAD, tm), 0)
    ones_row = jnp.where(row == 0, 1.0, 0.0).astype(BF16)
    for h in range(B_HEADS):
        vt_ref[0, h * V_ROWS:h * V_ROWS + V_HEAD, :] = vt[h * V_HEAD:(h + 1) * V_HEAD]
        vt_ref[0, h * V_ROWS + V_HEAD:(h + 1) * V_ROWS, :] = ones_row


def _mla_weights(w_dkv, w_uq, w_ukv):
    d = w_dkv.shape[0]
    pad_l = jnp.zeros((d, QK_NOPE), F32)
    pad_r = jnp.zeros((d, HEAD_PAD - QK_NOPE - QK_ROPE), F32)
    wd = jnp.concatenate([w_dkv[:, :Q_LORA + KV_LORA], pad_l, w_dkv[:, Q_LORA + KV_LORA:], pad_r], axis=1)
    wq = w_uq.reshape(Q_LORA, B_HEADS, QK_NOPE + QK_ROPE)
    wq = jnp.pad(wq, ((0, 0), (0, 0), (0, HEAD_PAD - QK_NOPE - QK_ROPE))).reshape(Q_LORA, B_HEADS * HEAD_PAD)
    wkv = w_ukv.reshape(KV_LORA, B_HEADS, QK_NOPE + V_HEAD)
    wk = jnp.pad(wkv[:, :, :QK_NOPE], ((0, 0), (0, 0), (0, HEAD_PAD - QK_NOPE))).reshape(KV_LORA, B_HEADS * HEAD_PAD)
    wvt = wkv[:, :, QK_NOPE:].reshape(KV_LORA, B_HEADS * V_HEAD).T
    return wd.astype(BF16), wq.T.astype(BF16), wk.astype(BF16), wvt.astype(BF16)


def _mla_proj(x3d, w_dkv, g_q, g_kv, w_uq, w_ukv, tm):
    b, s, d = x3d.shape
    wd, wqt, wk, wvt = _mla_weights(w_dkv, w_uq, w_ukv)
    tabs = _rope_tables(s)
    tabs_t = [t.T for t in tabs]
    hq = B_HEADS * HEAD_PAD
    hv = B_HEADS * V_ROWS
    tab = pl.BlockSpec((tm, HEAD_PAD), lambda bi, si: (si, 0))
    tab_t = pl.BlockSpec((HEAD_PAD, tm), lambda bi, si: (0, si))
    return pl.pallas_call(
        _mla_proj_kernel,
        grid=(b, s // tm),
        in_specs=[pl.BlockSpec((1, tm, d), lambda bi, si: (bi, si, 0)),
                  _resident(wd.shape), _resident((1, Q_LORA)), _resident((1, KV_LORA)),
                  _resident(wqt.shape), _resident(wk.shape), _resident(wvt.shape),
                  tab, tab, tab, tab_t, tab_t, tab_t],
        out_specs=[pl.BlockSpec((1, hq, tm), lambda bi, si: (bi, 0, si)),
                   pl.BlockSpec((1, tm, hq), lambda bi, si: (bi, si, 0)),
                   pl.BlockSpec((1, hv, tm), lambda bi, si: (bi, 0, si))],
        out_shape=[jax.ShapeDtypeStruct((b, hq, s), BF16),
                   jax.ShapeDtypeStruct((b, s, hq), BF16),
                   jax.ShapeDtypeStruct((b, hv, s), BF16)],
        compiler_params=_cparams(("parallel", "parallel")),
        name="mla_proj",
    )(x3d, wd, g_q[None], g_kv[None], wqt, wk, wvt, *tabs, *tabs_t)


def _mla_attn_kernel(qt_ref, k_ref, vt_ref, o_ref, s_scr, cmax_scr, m_all, acc_all, *, tq, tk, n_q, unroll):
    seq = k_ref.shape[1]
    nk = seq // tk
    blk_q = n_q * tq
    n_blk = seq // blk_q
    total = n_blk * nk
    trips_per_blk = nk // unroll

    def finalize(blk):
        par = blk % 2
        for j in range(n_q):
            acc = acc_all[par, j]
            q0 = pl.multiple_of(blk * blk_q + j * tq, tq)
            o_ref[0, :, pl.ds(q0, tq)] = (acc[:V_HEAD] / acc[V_HEAD:V_HEAD + 1]).astype(BF16)

    def run_step(t, t_static, do_qk=True):
        s_qk, s_sm = t_static % N_SLOTS, (t_static - 1) % N_SLOTS
        n_sm = t - 1 + nk
        par_sm, first_sm = (n_sm // nk + 1) % 2, (n_sm % nk) == 0
        v0 = pl.multiple_of((n_sm % nk) * tk, tk)
        vt = vt_ref[0, :, pl.ds(v0, tk)]
        if do_qk:
            k0 = pl.multiple_of((t % nk) * tk, tk)
            k = k_ref[0, pl.ds(k0, tk), :]
            for j in range(n_q):
                q0 = pl.multiple_of((t // nk) * blk_q + j * tq, tq)
                sc = _dot(k, qt_ref[0, :, pl.ds(q0, tq)])
                s_scr[s_qk, j] = sc
                cmax_scr[s_qk, j] = jnp.max(sc, axis=0, keepdims=True)
        for j in range(n_q):
            m = jnp.where(first_sm, NEG_INF, m_all[par_sm, j])
            m_new = jnp.maximum(m, cmax_scr[s_sm, j])
            alpha = jnp.exp2(m - m_new)
            m_all[par_sm, j] = m_new
            p = jnp.exp2(s_scr[s_sm, j] - m_new).astype(BF16)
            acc_all[par_sm, j] = alpha * acc_all[par_sm, j] + _dot(vt, p)

    s_scr[...] = jnp.zeros(s_scr.shape, F32)
    cmax_scr[...] = jnp.zeros(cmax_scr.shape, F32)
    m_all[...] = jnp.full(m_all.shape, NEG_INF, F32)
    acc_all[...] = jnp.zeros(acc_all.shape, F32)

    def trip(u, carry):
        @pl.when((u % trips_per_blk == 1) & (u > trips_per_blk))
        def _():
            finalize(u // trips_per_blk - 1)

        for i in range(unroll):
            run_step(unroll * u + i, i)
        return carry

    lax.fori_loop(0, total // unroll, trip, 0)
    run_step(total, total, do_qk=False)
    finalize(n_blk - 1)


def _mla_attn(qt, k, vt, n_q, tq, tk):
    b, s, _ = k.shape
    nk = s // tk
    unroll = max(u for u in range(N_SLOTS, MAX_UNROLL + 1, N_SLOTS) if nk % u == 0 and nk // u >= 2)
    assert s % (n_q * tq) == 0 and N_SLOTS == 2
    return pl.pallas_call(
        functools.partial(_mla_attn_kernel, tq=tq, tk=tk, n_q=n_q, unroll=unroll),
        grid=(b, B_HEADS),
        in_specs=[pl.BlockSpec((1, HEAD_PAD, s), lambda bi, h: (bi, h, 0)),
                  pl.BlockSpec((1, s, HEAD_PAD), lambda bi, h: (bi, 0, h)),
                  pl.BlockSpec((1, V_ROWS, s), lambda bi, h: (bi, h, 0))],
        out_specs=pl.BlockSpec((1, V_HEAD, s), lambda bi, h: (bi, h, 0)),
        out_shape=jax.ShapeDtypeStruct((b, B_HEADS * V_HEAD, s), BF16),
        scratch_shapes=[pltpu.VMEM((N_SLOTS, n_q, tk, tq), F32), pltpu.VMEM((N_SLOTS, n_q, 1, tq), F32),
                        pltpu.VMEM((2, n_q, 1, tq), F32), pltpu.VMEM((2, n_q, V_ROWS, tq), F32)],
        compiler_params=_cparams(("parallel", "parallel")),
        name="mla_attn",
    )(qt, k, vt)


def _proj_t_ln_kernel(ot_ref, x_ref, w_ref, g_ref, b_ref, out_ref):
    y = ALPHA * x_ref[0] + _dot_tn(ot_ref[0], w_ref[...])
    out_ref[0] = _layer_norm(y, g_ref[...], b_ref[...])


def _proj_t_ln(ot, x3d, w_bf16, ln_g, ln_b, tm):
    b, s, d = x3d.shape
    k = ot.shape[1]
    return pl.pallas_call(
        _proj_t_ln_kernel,
        grid=(b, s // tm),
        in_specs=[pl.BlockSpec((1, k, tm), lambda bi, si: (bi, 0, si)),
                  pl.BlockSpec((1, tm, d), lambda bi, si: (bi, si, 0)),
                  _resident((k, d)), _resident((1, d)), _resident((1, d))],
        out_specs=pl.BlockSpec((1, tm, d), lambda bi, si: (bi, si, 0)),
        out_shape=jax.ShapeDtypeStruct((b, s, d), F32),
        compiler_params=_cparams(("parallel", "parallel")),
        name="proj_t_ln",
    )(ot, x3d, w_bf16, ln_g, ln_b)


def _encoder(x, bias, w_qkv_a, w_o_a, w_dkv_b, g_q_b, g_kv_b, w_uq_b, w_ukv_b, w_o_b,
             ffn_w_in, ffn_conv_w, ffn_conv_b, ffn_w_out, ln_g, ln_b):
    b, s, d = x.shape
    tm = ROW_TILE
    for i in range(DEPTH):
        j = i // N_MIXERS
        g1, b1 = ln_g[i, 0][None], ln_b[i, 0][None]
        g2, b2 = ln_g[i, 1][None], ln_b[i, 1][None]
        if i % N_MIXERS == 0:
            qkv = _qkv_proj(x, w_qkv_a[j].astype(BF16), QKV_ROW_TILE)
            outs = [_dilated_attn(qkv[g], bias, g) for g in range(A_GROUPS)]
            x = _merge_proj_ln([o for o, _ in outs], [l for _, l in outs], x,
                               w_o_a[j].astype(BF16), g1, b1, tm)
        else:
            q, k, vt = _mla_proj(x, w_dkv_b[j], g_q_b[j], g_kv_b[j], w_uq_b[j], w_ukv_b[j], tm)
            ot = _mla_attn(q, k, vt, MLA_Q_TILES, MLA_Q_TILE, MLA_K_TILE)
            x = _proj_t_ln(ot, x, w_o_b[j].astype(BF16), g1, b1, tm)
        x = _ffn_ln(x, ffn_w_in[i], ffn_conv_w[i], ffn_conv_b[i], ffn_w_out[i], g2, b2, tm)
    return x


def kernel(x_prompt, x_sample, rel_bias, w_qkv_a, w_o_a, w_dkv_b, g_q_b, g_kv_b, w_uq_b, w_ukv_b, w_o_b,
           ffn_w_in, ffn_conv_w, ffn_conv_b, ffn_w_out, ln_g, ln_b):
    bias = _bias_tiles(rel_bias)
    args = (bias, w_qkv_a, w_o_a, w_dkv_b, g_q_b, g_kv_b, w_uq_b, w_ukv_b, w_o_b,
            ffn_w_in, ffn_conv_w, ffn_conv_b, ffn_w_out, ln_g, ln_b)
    return (_encoder(x_prompt, *args), _encoder(x_sample, *args))
```

```python
import functools
import math

import jax
import jax.numpy as jnp
import numpy as np
from jax import lax
from jax.experimental import pallas as pl
from jax.experimental.pallas import tpu as pltpu

F32 = jnp.float32
BF16 = jnp.bfloat16

D_MODEL = 1024
DEPTH = 2
N_MIXERS = 2

A_WINDOWS = (128, 512, 2048)
A_DILATIONS = (1, 4, 16)
A_GROUPS = 3
A_HEADS = 8
A_HEAD_DIM = 128
A_WIDTH = A_HEADS * A_HEAD_DIM
A_HALF = 64
NUM_BUCKETS = 32
MAX_DISTANCE = max(A_WINDOWS) // 2

B_HEADS = 16
Q_LORA = 384
KV_LORA = 256
QK_NOPE = 64
QK_ROPE = 32
V_HEAD = 64
ROPE_THETA = 10000.0
HEAD_PAD = 128
V_ROWS = V_HEAD + 16
N_SLOTS = 2
MAX_UNROLL = 8

D_FF = 2816
FF_CHUNK = 256

ALPHA = (2.0 * DEPTH) ** 0.25
LN_EPS = 1e-5
RMS_EPS = 1e-6
NEG_INF = -1e30

LANES = 128
SUBLANES = 8
VMEM_LIMIT = 56 * 1024 * 1024

ROW_TILE = 512
QKV_ROW_TILE = 512
DILATED_Q_TILE = 1024
MLA_Q_TILES = 2
MLA_Q_TILE = 512
MLA_K_TILE = 256

QK_SCALE_LOG2E = np.float32((QK_NOPE + QK_ROPE) ** -0.5 * math.log2(math.e))


def _cparams(sem):
    return pltpu.CompilerParams(dimension_semantics=sem, vmem_limit_bytes=VMEM_LIMIT)


def _resident(shape):
    nd = len(shape)
    return pl.BlockSpec(shape, lambda *_: (0,) * nd, pipeline_mode=pl.Buffered(1))


def _layer_norm(y, g, b):
    mu = jnp.mean(y, axis=-1, keepdims=True)
    yc = y - mu
    var = jnp.mean(yc * yc, axis=-1, keepdims=True)
    return yc * lax.rsqrt(var + LN_EPS) * g + b


def _dot(a, b):
    return jnp.dot(a, b, preferred_element_type=F32)


def _dot_nt(a, b):
    return lax.dot_general(a, b, (((1,), (1,)), ((), ())), preferred_element_type=F32)


def _dot_tn(a, b):
    return lax.dot_general(a, b, (((0,), (0,)), ((), ())), preferred_element_type=F32)


def _qkv_proj_kernel(x_ref, w_ref, *refs, tm):
    outs, x_planes, x_perm = refs[:A_GROUPS], refs[A_GROUPS], refs[A_GROUPS + 1]
    x = x_ref[0]
    n_planes = x.shape[1] // LANES
    for h in range(n_planes):
        x_planes[h] = x[:, h * LANES:(h + 1) * LANES]
    for g, dil in enumerate(A_DILATIONS):
        n = tm // dil
        if dil == 1:
            xb = x.astype(BF16)
        else:
            xp = x_perm.at[g - 1]
            for r in range(dil):
                for h in range(n_planes):
                    xp[r * n:(r + 1) * n, h * LANES:(h + 1) * LANES] = x_planes[h, pl.ds(r, n, stride=dil), :]
            xb = xp[...].astype(BF16)
        for c in range(3):
            n0 = (g * 3 + c) * A_WIDTH
            cols = slice(c * A_WIDTH, (c + 1) * A_WIDTH)
            res = _dot(xb, w_ref[:, n0:n0 + A_WIDTH]).astype(BF16)
            for r in range(dil):
                outs[g][0, r, :, cols] = res[r * n:(r + 1) * n]


def _qkv_proj(x3d, w_bf16, tm):
    b, s, d = x3d.shape
    width = 3 * A_WIDTH
    return pl.pallas_call(
        functools.partial(_qkv_proj_kernel, tm=tm),
        grid=(b, s // tm),
        in_specs=[pl.BlockSpec((1, tm, d), lambda bi, i: (bi, i, 0)), _resident(w_bf16.shape)],
        out_specs=[pl.BlockSpec((1, dil, tm // dil, width), lambda bi, i: (bi, 0, i, 0))
                   for dil in A_DILATIONS],
        out_shape=[jax.ShapeDtypeStruct((b, dil, s // dil, width), BF16) for dil in A_DILATIONS],
        scratch_shapes=[pltpu.VMEM((d // LANES, tm, LANES), F32),
                        pltpu.VMEM((sum(dil > 1 for dil in A_DILATIONS), tm, d), F32)],
        compiler_params=_cparams(("parallel", "parallel")),
        name="qkv_proj",
    )(x3d, w_bf16)


def _t5_bucket(rel):
    nb = NUM_BUCKETS // 2
    max_exact = nb // 2
    ret = jnp.where(rel > 0, nb, 0)
    n = jnp.abs(rel)
    large = max_exact + (jnp.log(jnp.maximum(n, 1).astype(F32) / max_exact)
                         / math.log(MAX_DISTANCE / max_exact) * (nb - max_exact)).astype(jnp.int32)
    large = jnp.minimum(large, nb - 1)
    return ret + jnp.where(n < max_exact, n, large)


def _band_bucket_index():
    qi = jnp.arange(2 * A_HALF)[:, None]
    kj = jnp.arange(4 * A_HALF)[None, :]
    rel = kj - A_HALF - qi
    tiles = [jnp.where(jnp.abs(rel) <= A_HALF, _t5_bucket(rel * dil), -1) for dil in A_DILATIONS]
    return jnp.stack(tiles, axis=0).astype(jnp.int32)


def _bias_tile_kernel(tab_ref, idx_ref, o_ref):
    col = pl.program_id(0) * A_HEADS + pl.program_id(1)
    idx = idx_ref[0]
    acc = jnp.full(idx.shape, NEG_INF, F32)
    for bucket in range(NUM_BUCKETS):
        acc = jnp.where(idx == bucket, tab_ref[bucket, col], acc)
    o_ref[0, 0] = acc


def _bias_tiles(rel_bias):
    idx = _band_bucket_index()
    tq, tk = idx.shape[1:]
    return pl.pallas_call(
        _bias_tile_kernel,
        grid=(A_GROUPS, A_HEADS),
        in_specs=[pl.BlockSpec(memory_space=pltpu.SMEM),
                  pl.BlockSpec((1, tq, tk), lambda g, h: (g, 0, 0))],
        out_specs=pl.BlockSpec((1, 1, tq, tk), lambda g, h: (g, h, 0, 0)),
        out_shape=jax.ShapeDtypeStruct((A_GROUPS, A_HEADS, tq, tk), F32),
        compiler_params=_cparams(("arbitrary", "arbitrary")),
        name="bias_tiles",
    )(rel_bias, idx)


def _dilated_attn_kernel(q_ref, kl_ref, km_ref, kr_ref, vl_ref, vm_ref, vr_ref, bias_ref,
                         o_ref, lse_ref, kbuf, vbuf, *, tq, sub_len):
    li = pl.program_id(2)
    kbuf[0:A_HALF] = kl_ref[0, 0]
    kbuf[A_HALF:A_HALF + tq] = km_ref[0, 0]
    kbuf[A_HALF + tq:] = kr_ref[0, 0]
    vbuf[0:A_HALF] = vl_ref[0, 0]
    vbuf[A_HALF:A_HALF + tq] = vm_ref[0, 0]
    vbuf[A_HALF + tq:] = vr_ref[0, 0]

    scale = A_HEAD_DIM ** -0.5
    qt, kt = 2 * A_HALF, 4 * A_HALF
    lane = lax.broadcasted_iota(jnp.int32, (qt, LANES), 1)
    for t in range(tq // qt):
        pos = li * tq + t * qt - A_HALF + lax.broadcasted_iota(jnp.int32, (1, kt), 1)
        edge = jnp.where((pos >= 0) & (pos < sub_len), 0.0, NEG_INF).astype(F32)
        lse_tile = jnp.zeros((qt, LANES), F32)
        for h in range(A_HEADS):
            cols = slice(h * A_HEAD_DIM, (h + 1) * A_HEAD_DIM)
            q = q_ref[0, 0, t * qt:(t + 1) * qt, cols]
            k = kbuf[t * qt:t * qt + kt, cols]
            v = vbuf[t * qt:t * qt + kt, cols]
            s = _dot_nt(q, k) * scale + bias_ref[0, h] + edge
            m = jnp.max(s, axis=-1, keepdims=True)
            p = jnp.exp(s - m)
            den = jnp.sum(p, axis=-1, keepdims=True)
            o = _dot(p.astype(BF16), v) / den
            o_ref[0, 0, t * qt:(t + 1) * qt, cols] = o.astype(BF16)
            lse_tile = jnp.where(lane == h, m + jnp.log(den), lse_tile)
        lse_ref[0, 0, t * qt:(t + 1) * qt, :] = lse_tile


def _dilated_attn(qkv_g, bias, g):
    batch, dil, sub_len, _ = qkv_g.shape
    tq = min(DILATED_Q_TILE, sub_len)
    n_t = sub_len // tq
    hb = tq // A_HALF
    n_hb = sub_len // A_HALF

    main = lambda which: pl.BlockSpec((1, 1, tq, A_WIDTH), lambda b, r, i: (b, r, i, which))
    left = lambda which: pl.BlockSpec(
        (1, 1, A_HALF, A_WIDTH), lambda b, r, i: (b, r, jnp.maximum(i * hb - 1, 0), which))
    right = lambda which: pl.BlockSpec(
        (1, 1, A_HALF, A_WIDTH), lambda b, r, i: (b, r, jnp.minimum((i + 1) * hb, n_hb - 1), which))

    return pl.pallas_call(
        functools.partial(_dilated_attn_kernel, tq=tq, sub_len=sub_len),
        grid=(batch, dil, n_t),
        in_specs=[main(0), left(1), main(1), right(1), left(2), main(2), right(2),
                  pl.BlockSpec((1, A_HEADS, 2 * A_HALF, 4 * A_HALF), lambda b, r, i: (g, 0, 0, 0))],
        out_specs=[pl.BlockSpec((1, 1, tq, A_WIDTH), lambda b, r, i: (b, r, i, 0)),
                   pl.BlockSpec((1, 1, tq, LANES), lambda b, r, i: (b, r, i, 0))],
        out_shape=[jax.ShapeDtypeStruct((batch, dil, sub_len, A_WIDTH), BF16),
                   jax.ShapeDtypeStruct((batch, dil, sub_len, LANES), F32)],
        scratch_shapes=[pltpu.VMEM((tq + 2 * A_HALF, A_WIDTH), BF16),
                        pltpu.VMEM((tq + 2 * A_HALF, A_WIDTH), BF16)],
        compiler_params=_cparams(("parallel", "parallel", "parallel")),
        name=f"dilated_attn_g{g}",
    )(qkv_g, qkv_g, qkv_g, qkv_g, qkv_g, qkv_g, qkv_g, bias)


def _merge_proj_ln_kernel(o0_ref, o1_ref, o2_ref, l0_ref, l1_ref, l2_ref, x_ref, w_ref, g_ref, b_ref,
                          out_ref, merged, o_scr, l_scr, *, tm):
    for g, (o_ref, l_ref, dil) in enumerate(zip((o0_ref, o1_ref, o2_ref), (l0_ref, l1_ref, l2_ref),
                                                A_DILATIONS)):
        for r in range(dil):
            rows = pl.ds(r, tm // dil, stride=dil) if dil > 1 else slice(None)
            l_scr[g, rows, :] = l_ref[0, r]
            for h in range(A_HEADS):
                o_scr[g, h, rows, :] = o_ref[0, r, :, h * A_HEAD_DIM:(h + 1) * A_HEAD_DIM].astype(F32)
    l0, l1, l2 = l_scr[0], l_scr[1], l_scr[2]
    m = jnp.maximum(jnp.maximum(l0, l1), l2)
    e0, e1, e2 = jnp.exp(l0 - m), jnp.exp(l1 - m), jnp.exp(l2 - m)
    den = e0 + e1 + e2
    w0, w1, w2 = e0 / den, e1 / den, e2 / den
    for h in range(A_HEADS):
        cols = slice(h * A_HEAD_DIM, (h + 1) * A_HEAD_DIM)
        mix = (w0[:, h:h + 1] * o_scr[0, h] + w1[:, h:h + 1] * o_scr[1, h]
               + w2[:, h:h + 1] * o_scr[2, h])
        merged[:, cols] = mix.astype(BF16)
    y = ALPHA * x_ref[0] + _dot(merged[...], w_ref[...])
    out_ref[0] = _layer_norm(y, g_ref[...], b_ref[...])


def _merge_proj_ln(os, lses, x3d, w_bf16, ln_g, ln_b, tm):
    b, s, d = x3d.shape
    plane = lambda w: [pl.BlockSpec((1, dil, tm // dil, w), lambda bi, i: (bi, 0, i, 0))
                       for dil in A_DILATIONS]
    row = pl.BlockSpec((1, tm, d), lambda bi, i: (bi, i, 0))
    return pl.pallas_call(
        functools.partial(_merge_proj_ln_kernel, tm=tm),
        grid=(b, s // tm),
        in_specs=plane(A_WIDTH) + plane(LANES) + [row, _resident((A_WIDTH, d)),
                                                  _resident((1, d)), _resident((1, d))],
        out_specs=row,
        out_shape=jax.ShapeDtypeStruct((b, s, d), F32),
        scratch_shapes=[pltpu.VMEM((tm, A_WIDTH), BF16),
                        pltpu.VMEM((A_GROUPS, A_HEADS, tm, A_HEAD_DIM), F32),
                        pltpu.VMEM((A_GROUPS, tm, LANES), F32)],
        compiler_params=_cparams(("parallel", "parallel")),
        name="merge_proj_ln",
    )(*os, *lses, x3d, w_bf16, ln_g, ln_b)


def _ffn_ln_kernel(x_ref, xp_ref, xn_ref, win_ref, cw_ref, cb_ref, wout_ref, g_ref, b_ref,
                   out_ref, act, *, tm):
    si = pl.program_id(1)
    n_s = pl.num_programs(1)
    x = x_ref[0]
    xb = x.astype(BF16)
    prev_ok = (si > 0).astype(F32)
    next_ok = (si < n_s - 1).astype(F32)
    xe = jnp.concatenate([xp_ref[0] * prev_ok, xn_ref[0] * next_ok], axis=0).astype(BF16)
    xall = jnp.concatenate([xb, xe], axis=0)
    row = lax.broadcasted_iota(jnp.int32, (tm, 2 * FF_CHUNK), 0)
    for c in range(D_FF // FF_CHUNK):
        cols = slice(c * 2 * FF_CHUNK, (c + 1) * 2 * FF_CHUNK)
        hall = _dot(xall, win_ref[:, cols])
        h = hall[:tm]
        he = hall[tm:]
        up = jnp.where(row == 0, he[SUBLANES - 1:SUBLANES], pltpu.roll(h, 1, 0))
        dn = jnp.where(row == tm - 1, he[SUBLANES:SUBLANES + 1], pltpu.roll(h, tm - 1, 0))
        cw = cw_ref[:, cols]
        hc = up * cw[0:1] + h * cw[1:2] + dn * cw[2:3] + cb_ref[:, cols]
        a = hc[:, :FF_CHUNK]
        gate = hc[:, FF_CHUNK:]
        gelu = 0.5 * gate * (1.0 + lax.erf(gate * np.float32(math.sqrt(0.5))))
        act[:, c * FF_CHUNK:(c + 1) * FF_CHUNK] = (a * gelu).astype(BF16)
    y = ALPHA * x + _dot(act[...], wout_ref[...])
    out_ref[0] = _layer_norm(y, g_ref[...], b_ref[...])


def _chunk_interleave(w):
    lead = w.shape[:-1]
    w = w.reshape(lead + (2, D_FF // FF_CHUNK, FF_CHUNK))
    w = jnp.swapaxes(w, -3, -2)
    return w.reshape(lead + (2 * D_FF,))


def _ffn_ln(x3d, w_in, conv_w, conv_b, w_out, ln_g, ln_b, tm):
    b, s, d = x3d.shape
    nb8 = tm // SUBLANES
    n8 = s // SUBLANES
    win = _chunk_interleave(w_in).astype(BF16)
    cw = _chunk_interleave(conv_w)
    cb = _chunk_interleave(conv_b)[None]
    return pl.pallas_call(
        functools.partial(_ffn_ln_kernel, tm=tm),
        grid=(b, s // tm),
        in_specs=[pl.BlockSpec((1, tm, d), lambda bi, si: (bi, si, 0)),
                  pl.BlockSpec((1, SUBLANES, d), lambda bi, si: (bi, jnp.maximum(si * nb8 - 1, 0), 0)),
                  pl.BlockSpec((1, SUBLANES, d), lambda bi, si: (bi, jnp.minimum((si + 1) * nb8, n8 - 1), 0)),
                  _resident((d, 2 * D_FF)), _resident((3, 2 * D_FF)), _resident((1, 2 * D_FF)),
                  _resident((D_FF, d)), _resident((1, d)), _resident((1, d))],
        out_specs=pl.BlockSpec((1, tm, d), lambda bi, si: (bi, si, 0)),
        out_shape=jax.ShapeDtypeStruct((b, s, d), F32),
        scratch_shapes=[pltpu.VMEM((tm, D_FF), BF16)],
        compiler_params=_cparams(("parallel", "parallel")),
        name="ffn_ln",
    )(x3d, x3d, x3d, win, cw, cb, w_out.astype(BF16), ln_g, ln_b)


def _rope_tables(seq):
    inv = 1.0 / (ROPE_THETA ** (jnp.arange(0, QK_ROPE, 2, dtype=F32) / QK_ROPE))
    ang = jnp.arange(seq, dtype=F32)[:, None] * inv[None, :]
    cos, sin = jnp.cos(ang), jnp.sin(ang)
    half = QK_ROPE // 2
    ones = jnp.ones((seq, QK_NOPE), F32)
    z_nope = jnp.zeros((seq, QK_NOPE), F32)
    z_half = jnp.zeros((seq, half), F32)
    z_pad = jnp.zeros((seq, HEAD_PAD - QK_NOPE - QK_ROPE), F32)
    cmul = jnp.concatenate([ones, cos, cos, z_pad], axis=1)
    s_first = jnp.concatenate([z_nope, -sin, z_half, z_pad], axis=1)
    s_second = jnp.concatenate([z_nope, z_half, sin, z_pad], axis=1)
    return cmul, s_first, s_second


def _rope(t, cmul, s_first, s_second):
    half = QK_ROPE // 2
    return (t * cmul + pltpu.roll(t, HEAD_PAD - half, 1) * s_first
            + pltpu.roll(t, half, 1) * s_second)


def _rms_norm(x, g):
    ms = jnp.mean(x * x, axis=-1, keepdims=True)
    return x * lax.rsqrt(ms + RMS_EPS) * g


def _mla_proj_kernel(x_ref, wd_ref, gq_ref, gkv_ref, wuqt_ref, wuk_ref, wuvt_ref,
                     cm_ref, s1_ref, s2_ref, cmt_ref, s1t_ref, s2t_ref, qt_ref, k_ref, vt_ref):
    xb = x_ref[0].astype(BF16)
    c = _dot(xb, wd_ref[...])
    cqn = _rms_norm(c[:, :Q_LORA], gq_ref[...]).astype(BF16)
    ckvn = _rms_norm(c[:, Q_LORA:Q_LORA + KV_LORA], gkv_ref[...]).astype(BF16)
    cm, s1, s2 = cm_ref[...], s1_ref[...], s2_ref[...]
    kr = _rope(c[:, Q_LORA + KV_LORA:], cm, s1, s2)
    k = _dot(ckvn, wuk_ref[...])
    for h in range(B_HEADS):
        cols = slice(h * HEAD_PAD, (h + 1) * HEAD_PAD)
        k_ref[0, :, cols] = (k[:, cols] + kr).astype(BF16)
    qt = _dot_nt(wuqt_ref[...], cqn)
    cmt, s1t, s2t = cmt_ref[...], s1t_ref[...], s2t_ref[...]
    half = QK_ROPE // 2
    for h in range(B_HEADS):
        rows = slice(h * HEAD_PAD, (h + 1) * HEAD_PAD)
        t = qt[rows]
        rot = t * cmt + pltpu.roll(t, HEAD_PAD - half, 0) * s1t + pltpu.roll(t, half, 0) * s2t
        qt_ref[0, rows, :] = (rot * QK_SCALE_LOG2E).astype(BF16)
    vt = _dot_nt(wuvt_ref[...], ckvn).astype(BF16)
    tm = vt.shape[1]
    row = lax.broadcasted_iota(jnp.int32, (V_ROWS - V_HEAD, tm), 0)
    ones_row = jnp.where(row == 0, 1.0, 0.0).astype(BF16)
    for h in range(B_HEADS):
        vt_ref[0, h * V_ROWS:h * V_ROWS + V_HEAD, :] = vt[h * V_HEAD:(h + 1) * V_HEAD]
        vt_ref[0, h * V_ROWS + V_HEAD:(h + 1) * V_ROWS, :] = ones_row


def _mla_weights(w_dkv, w_uq, w_ukv):
    d = w_dkv.shape[0]
    pad_l = jnp.zeros((d, QK_NOPE), F32)
    pad_r = jnp.zeros((d, HEAD_PAD - QK_NOPE - QK_ROPE), F32)
    wd = jnp.concatenate([w_dkv[:, :Q_LORA + KV_LORA], pad_l, w_dkv[:, Q_LORA + KV_LORA:], pad_r], axis=1)
    wq = w_uq.reshape(Q_LORA, B_HEADS, QK_NOPE + QK_ROPE)
    wq = jnp.pad(wq, ((0, 0), (0, 0), (0, HEAD_PAD - QK_NOPE - QK_ROPE))).reshape(Q_LORA, B_HEADS * HEAD_PAD)
    wkv = w_ukv.reshape(KV_LORA, B_HEADS, QK_NOPE + V_HEAD)
    wk = jnp.pad(wkv[:, :, :QK_NOPE], ((0, 0), (0, 0), (0, HEAD_PAD - QK_NOPE))).reshape(KV_LORA, B_HEADS * HEAD_PAD)
    wvt = wkv[:, :, QK_NOPE:].reshape(KV_LORA, B_HEADS * V_HEAD).T
    return wd.astype(BF16), wq.T.astype(BF16), wk.astype(BF16), wvt.astype(BF16)


def _mla_proj(x3d, w_dkv, g_q, g_kv, w_uq, w_ukv, tm):
    b, s, d = x3d.shape
    wd, wqt, wk, wvt = _mla_weights(w_dkv, w_uq, w_ukv)
    tabs = _rope_tables(s)
    tabs_t = [t.T for t in tabs]
    hq = B_HEADS * HEAD_PAD
    hv = B_HEADS * V_ROWS
    tab = pl.BlockSpec((tm, HEAD_PAD), lambda bi, si: (si, 0))
    tab_t = pl.BlockSpec((HEAD_PAD, tm), lambda bi, si: (0, si))
    return pl.pallas_call(
        _mla_proj_kernel,
        grid=(b, s // tm),
        in_specs=[pl.BlockSpec((1, tm, d), lambda bi, si: (bi, si, 0)),
                  _resident(wd.shape), _resident((1, Q_LORA)), _resident((1, KV_LORA)),
                  _resident(wqt.shape), _resident(wk.shape), _resident(wvt.shape),
                  tab, tab, tab, tab_t, tab_t, tab_t],
        out_specs=[pl.BlockSpec((1, hq, tm), lambda bi, si: (bi, 0, si)),
                   pl.BlockSpec((1, tm, hq), lambda bi, si: (bi, si, 0)),
                   pl.BlockSpec((1, hv, tm), lambda bi, si: (bi, 0, si))],
        out_shape=[jax.ShapeDtypeStruct((b, hq, s), BF16),
                   jax.ShapeDtypeStruct((b, s, hq), BF16),
                   jax.ShapeDtypeStruct((b, hv, s), BF16)],
        compiler_params=_cparams(("parallel", "parallel")),
        name="mla_proj",
    )(x3d, wd, g_q[None], g_kv[None], wqt, wk, wvt, *tabs, *tabs_t)


def _mla_attn_kernel(qt_ref, k_ref, vt_ref, o_ref, s_scr, cmax_scr, m_all, acc_all, *, tq, tk, n_q, unroll):
    seq = k_ref.shape[1]
    nk = seq // tk
    blk_q = n_q * tq
    n_blk = seq // blk_q
    total = n_blk * nk
    trips_per_blk = nk // unroll

    def finalize(blk):
        par = blk % 2
        for j in range(n_q):
            acc = acc_all[par, j]
            q0 = pl.multiple_of(blk * blk_q + j * tq, tq)
            o_ref[0, :, pl.ds(q0, tq)] = (acc[:V_HEAD] / acc[V_HEAD:V_HEAD + 1]).astype(BF16)

    def run_step(t, t_static, do_qk=True):
        s_qk, s_sm = t_static % N_SLOTS, (t_static - 1) % N_SLOTS
        n_sm = t - 1 + nk
        par_sm, first_sm = (n_sm // nk + 1) % 2, (n_sm % nk) == 0
        v0 = pl.multiple_of((n_sm % nk) * tk, tk)
        vt = vt_ref[0, :, pl.ds(v0, tk)]
        if do_qk:
            k0 = pl.multiple_of((t % nk) * tk, tk)
            k = k_ref[0, pl.ds(k0, tk), :]
            for j in range(n_q):
                q0 = pl.multiple_of((t // nk) * blk_q + j * tq, tq)
                sc = _dot(k, qt_ref[0, :, pl.ds(q0, tq)])
                s_scr[s_qk, j] = sc
                cmax_scr[s_qk, j] = jnp.max(sc, axis=0, keepdims=True)
        for j in range(n_q):
            m = jnp.where(first_sm, NEG_INF, m_all[par_sm, j])
            m_new = jnp.maximum(m, cmax_scr[s_sm, j])
            alpha = jnp.exp2(m - m_new)
            m_all[par_sm, j] = m_new
            p = jnp.exp2(s_scr[s_sm, j] - m_new).astype(BF16)
            acc_all[par_sm, j] = alpha * acc_all[par_sm, j] + _dot(vt, p)

    s_scr[...] = jnp.zeros(s_scr.shape, F32)
    cmax_scr[...] = jnp.zeros(cmax_scr.shape, F32)
    m_all[...] = jnp.full(m_all.shape, NEG_INF, F32)
    acc_all[...] = jnp.zeros(acc_all.shape, F32)

    def trip(u, carry):
        @pl.when((u % trips_per_blk == 1) & (u > trips_per_blk))
        def _():
            finalize(u // trips_per_blk - 1)

        for i in range(unroll):
            run_step(unroll * u + i, i)
        return carry

    lax.fori_loop(0, total // unroll, trip, 0)
    run_step(total, total, do_qk=False)
    finalize(n_blk - 1)


def _mla_attn(qt, k, vt, n_q, tq, tk):
    b, s, _ = k.shape
    nk = s // tk
    unroll = max(u for u in range(N_SLOTS, MAX_UNROLL + 1, N_SLOTS) if nk % u == 0 and nk // u >= 2)
    assert s % (n_q * tq) == 0 and N_SLOTS == 2
    return pl.pallas_call(
        functools.partial(_mla_attn_kernel, tq=tq, tk=tk, n_q=n_q, unroll=unroll),
        grid=(b, B_HEADS),
        in_specs=[pl.BlockSpec((1, HEAD_PAD, s), lambda bi, h: (bi, h, 0)),
                  pl.BlockSpec((1, s, HEAD_PAD), lambda bi, h: (bi, 0, h)),
                  pl.BlockSpec((1, V_ROWS, s), lambda bi, h: (bi, h, 0))],
        out_specs=pl.BlockSpec((1, V_HEAD, s), lambda bi, h: (bi, h, 0)),
        out_shape=jax.ShapeDtypeStruct((b, B_HEADS * V_HEAD, s), BF16),
        scratch_shapes=[pltpu.VMEM((N_SLOTS, n_q, tk, tq), F32), pltpu.VMEM((N_SLOTS, n_q, 1, tq), F32),
                        pltpu.VMEM((2, n_q, 1, tq), F32), pltpu.VMEM((2, n_q, V_ROWS, tq), F32)],
        compiler_params=_cparams(("parallel", "parallel")),
        name="mla_attn",
    )(qt, k, vt)


def _proj_t_ln_kernel(ot_ref, x_ref, w_ref, g_ref, b_ref, out_ref):
    y = ALPHA * x_ref[0] + _dot_tn(ot_ref[0], w_ref[...])
    out_ref[0] = _layer_norm(y, g_ref[...], b_ref[...])


def _proj_t_ln(ot, x3d, w_bf16, ln_g, ln_b, tm):
    b, s, d = x3d.shape
    k = ot.shape[1]
    return pl.pallas_call(
        _proj_t_ln_kernel,
        grid=(b, s // tm),
        in_specs=[pl.BlockSpec((1, k, tm), lambda bi, si: (bi, 0, si)),
                  pl.BlockSpec((1, tm, d), lambda bi, si: (bi, si, 0)),
                  _resident((k, d)), _resident((1, d)), _resident((1, d))],
        out_specs=pl.BlockSpec((1, tm, d), lambda bi, si: (bi, si, 0)),
        out_shape=jax.ShapeDtypeStruct((b, s, d), F32),
        compiler_params=_cparams(("parallel", "parallel")),
        name="proj_t_ln",
    )(ot, x3d, w_bf16, ln_g, ln_b)


def _encoder(x, bias, w_qkv_a, w_o_a, w_dkv_b, g_q_b, g_kv_b, w_uq_b, w_ukv_b, w_o_b,
             ffn_w_in, ffn_conv_w, ffn_conv_b, ffn_w_out, ln_g, ln_b):
    b, s, d = x.shape
    tm = ROW_TILE
    for i in range(DEPTH):
        j = i // N_MIXERS
        g1, b1 = ln_g[i, 0][None], ln_b[i, 0][None]
        g2, b2 = ln_g[i, 1][None], ln_b[i, 1][None]
        if i % N_MIXERS == 0:
            qkv = _qkv_proj(x, w_qkv_a[j].astype(BF16), QKV_ROW_TILE)
            outs = [_dilated_attn(qkv[g], bias, g) for g in range(A_GROUPS)]
            x = _merge_proj_ln([o for o, _ in outs], [l for _, l in outs], x,
                               w_o_a[j].astype(BF16), g1, b1, tm)
        else:
            q, k, vt = _mla_proj(x, w_dkv_b[j], g_q_b[j], g_kv_b[j], w_uq_b[j], w_ukv_b[j], tm)
            ot = _mla_attn(q, k, vt, MLA_Q_TILES, MLA_Q_TILE, MLA_K_TILE)
            x = _proj_t_ln(ot, x, w_o_b[j].astype(BF16), g1, b1, tm)
        x = _ffn_ln(x, ffn_w_in[i], ffn_conv_w[i], ffn_conv_b[i], ffn_w_out[i], g2, b2, tm)
    return x


def kernel(x_prompt, x_sample, rel_bias, w_qkv_a, w_o_a, w_dkv_b, g_q_b, g_kv_b, w_uq_b, w_ukv_b, w_o_b,
           ffn_w_in, ffn_conv_w, ffn_conv_b, ffn_w_out, ln_g, ln_b):
    bias = _bias_tiles(rel_bias)
    args = (bias, w_qkv_a, w_o_a, w_dkv_b, g_q_b, g_kv_b, w_uq_b, w_ukv_b, w_o_b,
            ffn_w_in, ffn_conv_w, ffn_conv_b, ffn_w_out, ln_g, ln_b)
    return (_encoder(x_prompt, *args), _encoder(x_sample, *args))
```

```python
import functools
import math

import jax
import jax.numpy as jnp
import numpy as np
from jax import lax
from jax.experimental import pallas as pl
from jax.experimental.pallas import tpu as pltpu

F32 = jnp.float32
BF16 = jnp.bfloat16

D_MODEL = 1024
DEPTH = 2
N_MIXERS = 2

A_WINDOWS = (128, 512, 2048)
A_DILATIONS = (1, 4, 16)
A_GROUPS = 3
A_HEADS = 8
A_HEAD_DIM = 128
A_WIDTH = A_HEADS * A_HEAD_DIM
A_HALF = 64
NUM_BUCKETS = 32
MAX_DISTANCE = max(A_WINDOWS) // 2

B_HEADS = 16
Q_LORA = 384
KV_LORA = 256
QK_NOPE = 64
QK_ROPE = 32
V_HEAD = 64
ROPE_THETA = 10000.0
HEAD_PAD = 128
V_ROWS = V_HEAD + 16
N_SLOTS = 2
MAX_UNROLL = 8

D_FF = 2816
FF_CHUNK = 256

ALPHA = (2.0 * DEPTH) ** 0.25
LN_EPS = 1e-5
RMS_EPS = 1e-6
NEG_INF = -1e30

LANES = 128
SUBLANES = 8
VMEM_LIMIT = 56 * 1024 * 1024

ROW_TILE = 512
QKV_ROW_TILE = 512
DILATED_Q_TILE = 1024
MLA_Q_TILES = 1
MLA_Q_TILE = 1024
MLA_K_TILE = 256

QK_SCALE_LOG2E = np.float32((QK_NOPE + QK_ROPE) ** -0.5 * math.log2(math.e))


def _cparams(sem):
    return pltpu.CompilerParams(dimension_semantics=sem, vmem_limit_bytes=VMEM_LIMIT)


def _resident(shape):
    nd = len(shape)
    return pl.BlockSpec(shape, lambda *_: (0,) * nd, pipeline_mode=pl.Buffered(1))


def _layer_norm(y, g, b):
    mu = jnp.mean(y, axis=-1, keepdims=True)
    yc = y - mu
    var = jnp.mean(yc * yc, axis=-1, keepdims=True)
    return yc * lax.rsqrt(var + LN_EPS) * g + b


def _dot(a, b):
    return jnp.dot(a, b, preferred_element_type=F32)


def _dot_nt(a, b):
    return lax.dot_general(a, b, (((1,), (1,)), ((), ())), preferred_element_type=F32)


def _dot_tn(a, b):
    return lax.dot_general(a, b, (((0,), (0,)), ((), ())), preferred_element_type=F32)


def _qkv_proj_kernel(x_ref, w_ref, *refs, tm):
    outs, x_planes, x_perm = refs[:A_GROUPS], refs[A_GROUPS], refs[A_GROUPS + 1]
    x = x_ref[0]
    n_planes = x.shape[1] // LANES
    for h in range(n_planes):
        x_planes[h] = x[:, h * LANES:(h + 1) * LANES]
    for g, dil in enumerate(A_DILATIONS):
        n = tm // dil
        if dil == 1:
            xb = x.astype(BF16)
        else:
            xp = x_perm.at[g - 1]
            for r in range(dil):
                for h in range(n_planes):
                    xp[r * n:(r + 1) * n, h * LANES:(h + 1) * LANES] = x_planes[h, pl.ds(r, n, stride=dil), :]
            xb = xp[...].astype(BF16)
        for c in range(3):
            n0 = (g * 3 + c) * A_WIDTH
            cols = slice(c * A_WIDTH, (c + 1) * A_WIDTH)
            res = _dot(xb, w_ref[:, n0:n0 + A_WIDTH]).astype(BF16)
            for r in range(dil):
                outs[g][0, r, :, cols] = res[r * n:(r + 1) * n]


def _qkv_proj(x3d, w_bf16, tm):
    b, s, d = x3d.shape
    width = 3 * A_WIDTH
    return pl.pallas_call(
        functools.partial(_qkv_proj_kernel, tm=tm),
        grid=(b, s // tm),
        in_specs=[pl.BlockSpec((1, tm, d), lambda bi, i: (bi, i, 0)), _resident(w_bf16.shape)],
        out_specs=[pl.BlockSpec((1, dil, tm // dil, width), lambda bi, i: (bi, 0, i, 0))
                   for dil in A_DILATIONS],
        out_shape=[jax.ShapeDtypeStruct((b, dil, s // dil, width), BF16) for dil in A_DILATIONS],
        scratch_shapes=[pltpu.VMEM((d // LANES, tm, LANES), F32),
                        pltpu.VMEM((sum(dil > 1 for dil in A_DILATIONS), tm, d), F32)],
        compiler_params=_cparams(("parallel", "parallel")),
        name="qkv_proj",
    )(x3d, w_bf16)


def _t5_bucket(rel):
    nb = NUM_BUCKETS // 2
    max_exact = nb // 2
    ret = jnp.where(rel > 0, nb, 0)
    n = jnp.abs(rel)
    large = max_exact + (jnp.log(jnp.maximum(n, 1).astype(F32) / max_exact)
                         / math.log(MAX_DISTANCE / max_exact) * (nb - max_exact)).astype(jnp.int32)
    large = jnp.minimum(large, nb - 1)
    return ret + jnp.where(n < max_exact, n, large)


def _band_bucket_index():
    qi = jnp.arange(2 * A_HALF)[:, None]
    kj = jnp.arange(4 * A_HALF)[None, :]
    rel = kj - A_HALF - qi
    tiles = [jnp.where(jnp.abs(rel) <= A_HALF, _t5_bucket(rel * dil), -1) for dil in A_DILATIONS]
    return jnp.stack(tiles, axis=0).astype(jnp.int32)


def _bias_tile_kernel(tab_ref, idx_ref, o_ref):
    col = pl.program_id(0) * A_HEADS + pl.program_id(1)
    idx = idx_ref[0]
    acc = jnp.full(idx.shape, NEG_INF, F32)
    for bucket in range(NUM_BUCKETS):
        acc = jnp.where(idx == bucket, tab_ref[bucket, col], acc)
    o_ref[0, 0] = acc


def _bias_tiles(rel_bias):
    idx = _band_bucket_index()
    tq, tk = idx.shape[1:]
    return pl.pallas_call(
        _bias_tile_kernel,
        grid=(A_GROUPS, A_HEADS),
        in_specs=[pl.BlockSpec(memory_space=pltpu.SMEM),
                  pl.BlockSpec((1, tq, tk), lambda g, h: (g, 0, 0))],
        out_specs=pl.BlockSpec((1, 1, tq, tk), lambda g, h: (g, h, 0, 0)),
        out_shape=jax.ShapeDtypeStruct((A_GROUPS, A_HEADS, tq, tk), F32),
        compiler_params=_cparams(("arbitrary", "arbitrary")),
        name="bias_tiles",
    )(rel_bias, idx)


def _dilated_attn_kernel(q_ref, kl_ref, km_ref, kr_ref, vl_ref, vm_ref, vr_ref, bias_ref,
                         o_ref, lse_ref, kbuf, vbuf, *, tq, sub_len):
    li = pl.program_id(2)
    kbuf[0:A_HALF] = kl_ref[0, 0]
    kbuf[A_HALF:A_HALF + tq] = km_ref[0, 0]
    kbuf[A_HALF + tq:] = kr_ref[0, 0]
    vbuf[0:A_HALF] = vl_ref[0, 0]
    vbuf[A_HALF:A_HALF + tq] = vm_ref[0, 0]
    vbuf[A_HALF + tq:] = vr_ref[0, 0]

    scale = A_HEAD_DIM ** -0.5
    qt, kt = 2 * A_HALF, 4 * A_HALF
    lane = lax.broadcasted_iota(jnp.int32, (qt, LANES), 1)
    for t in range(tq // qt):
        pos = li * tq + t * qt - A_HALF + lax.broadcasted_iota(jnp.int32, (1, kt), 1)
        edge = jnp.where((pos >= 0) & (pos < sub_len), 0.0, NEG_INF).astype(F32)
        lse_tile = jnp.zeros((qt, LANES), F32)
        for h in range(A_HEADS):
            cols = slice(h * A_HEAD_DIM, (h + 1) * A_HEAD_DIM)
            q = q_ref[0, 0, t * qt:(t + 1) * qt, cols]
            k = kbuf[t * qt:t * qt + kt, cols]
            v = vbuf[t * qt:t * qt + kt, cols]
            s = _dot_nt(q, k) * scale + bias_ref[0, h] + edge
            m = jnp.max(s, axis=-1, keepdims=True)
            p = jnp.exp(s - m)
            den = jnp.sum(p, axis=-1, keepdims=True)
            o = _dot(p.astype(BF16), v) / den
            o_ref[0, 0, t * qt:(t + 1) * qt, cols] = o.astype(BF16)
            lse_tile = jnp.where(lane == h, m + jnp.log(den), lse_tile)
        lse_ref[0, 0, t * qt:(t + 1) * qt, :] = lse_tile


def _dilated_attn(qkv_g, bias, g):
    batch, dil, sub_len, _ = qkv_g.shape
    tq = min(DILATED_Q_TILE, sub_len)
    n_t = sub_len // tq
    hb = tq // A_HALF
    n_hb = sub_len // A_HALF

    main = lambda which: pl.BlockSpec((1, 1, tq, A_WIDTH), lambda b, r, i: (b, r, i, which))
    left = lambda which: pl.BlockSpec(
        (1, 1, A_HALF, A_WIDTH), lambda b, r, i: (b, r, jnp.maximum(i * hb - 1, 0), which))
    right = lambda which: pl.BlockSpec(
        (1, 1, A_HALF, A_WIDTH), lambda b, r, i: (b, r, jnp.minimum((i + 1) * hb, n_hb - 1), which))

    return pl.pallas_call(
        functools.partial(_dilated_attn_kernel, tq=tq, sub_len=sub_len),
        grid=(batch, dil, n_t),
        in_specs=[main(0), left(1), main(1), right(1), left(2), main(2), right(2),
                  pl.BlockSpec((1, A_HEADS, 2 * A_HALF, 4 * A_HALF), lambda b, r, i: (g, 0, 0, 0))],
        out_specs=[pl.BlockSpec((1, 1, tq, A_WIDTH), lambda b, r, i: (b, r, i, 0)),
                   pl.BlockSpec((1, 1, tq, LANES), lambda b, r, i: (b, r, i, 0))],
        out_shape=[jax.ShapeDtypeStruct((batch, dil, sub_len, A_WIDTH), BF16),
                   jax.ShapeDtypeStruct((batch, dil, sub_len, LANES), F32)],
        scratch_shapes=[pltpu.VMEM((tq + 2 * A_HALF, A_WIDTH), BF16),
                        pltpu.VMEM((tq + 2 * A_HALF, A_WIDTH), BF16)],
        compiler_params=_cparams(("parallel", "parallel", "parallel")),
        name=f"dilated_attn_g{g}",
    )(qkv_g, qkv_g, qkv_g, qkv_g, qkv_g, qkv_g, qkv_g, bias)


def _merge_proj_ln_kernel(o0_ref, o1_ref, o2_ref, l0_ref, l1_ref, l2_ref, x_ref, w_ref, g_ref, b_ref,
                          out_ref, merged, o_scr, l_scr, *, tm):
    for g, (o_ref, l_ref, dil) in enumerate(zip((o0_ref, o1_ref, o2_ref), (l0_ref, l1_ref, l2_ref),
                                                A_DILATIONS)):
        for r in range(dil):
            rows = pl.ds(r, tm // dil, stride=dil) if dil > 1 else slice(None)
            l_scr[g, rows, :] = l_ref[0, r]
            for h in range(A_HEADS):
                o_scr[g, h, rows, :] = o_ref[0, r, :, h * A_HEAD_DIM:(h + 1) * A_HEAD_DIM].astype(F32)
    l0, l1, l2 = l_scr[0], l_scr[1], l_scr[2]
    m = jnp.maximum(jnp.maximum(l0, l1), l2)
    e0, e1, e2 = jnp.exp(l0 - m), jnp.exp(l1 - m), jnp.exp(l2 - m)
    den = e0 + e1 + e2
    w0, w1, w2 = e0 / den, e1 / den, e2 / den
    for h in range(A_HEADS):
        cols = slice(h * A_HEAD_DIM, (h + 1) * A_HEAD_DIM)
        mix = (w0[:, h:h + 1] * o_scr[0, h] + w1[:, h:h + 1] * o_scr[1, h]
               + w2[:, h:h + 1] * o_scr[2, h])
        merged[:, cols] = mix.astype(BF16)
    y = ALPHA * x_ref[0] + _dot(merged[...], w_ref[...])
    out_ref[0] = _layer_norm(y, g_ref[...], b_ref[...])


def _merge_proj_ln(os, lses, x3d, w_bf16, ln_g, ln_b, tm):
    b, s, d = x3d.shape
    plane = lambda w: [pl.BlockSpec((1, dil, tm // dil, w), lambda bi, i: (bi, 0, i, 0))
                       for dil in A_DILATIONS]
    row = pl.BlockSpec((1, tm, d), lambda bi, i: (bi, i, 0))
    return pl.pallas_call(
        functools.partial(_merge_proj_ln_kernel, tm=tm),
        grid=(b, s // tm),
        in_specs=plane(A_WIDTH) + plane(LANES) + [row, _resident((A_WIDTH, d)),
                                                  _resident((1, d)), _resident((1, d))],
        out_specs=row,
        out_shape=jax.ShapeDtypeStruct((b, s, d), F32),
        scratch_shapes=[pltpu.VMEM((tm, A_WIDTH), BF16),
                        pltpu.VMEM((A_GROUPS, A_HEADS, tm, A_HEAD_DIM), F32),
                        pltpu.VMEM((A_GROUPS, tm, LANES), F32)],
        compiler_params=_cparams(("parallel", "parallel")),
        name="merge_proj_ln",
    )(*os, *lses, x3d, w_bf16, ln_g, ln_b)


def _ffn_ln_kernel(x_ref, xp_ref, xn_ref, win_ref, cw_ref, cb_ref, wout_ref, g_ref, b_ref,
                   out_ref, act, *, tm):
    si = pl.program_id(1)
    n_s = pl.num_programs(1)
    x = x_ref[0]
    xb = x.astype(BF16)
    prev_ok = (si > 0).astype(F32)
    next_ok = (si < n_s - 1).astype(F32)
    xe = jnp.concatenate([xp_ref[0] * prev_ok, xn_ref[0] * next_ok], axis=0).astype(BF16)
    xall = jnp.concatenate([xb, xe], axis=0)
    row = lax.broadcasted_iota(jnp.int32, (tm, 2 * FF_CHUNK), 0)
    for c in range(D_FF // FF_CHUNK):
        cols = slice(c * 2 * FF_CHUNK, (c + 1) * 2 * FF_CHUNK)
        hall = _dot(xall, win_ref[:, cols])
        h = hall[:tm]
        he = hall[tm:]
        up = jnp.where(row == 0, he[SUBLANES - 1:SUBLANES], pltpu.roll(h, 1, 0))
        dn = jnp.where(row == tm - 1, he[SUBLANES:SUBLANES + 1], pltpu.roll(h, tm - 1, 0))
        cw = cw_ref[:, cols]
        hc = up * cw[0:1] + h * cw[1:2] + dn * cw[2:3] + cb_ref[:, cols]
        a = hc[:, :FF_CHUNK]
        gate = hc[:, FF_CHUNK:]
        gelu = 0.5 * gate * (1.0 + lax.erf(gate * np.float32(math.sqrt(0.5))))
        act[:, c * FF_CHUNK:(c + 1) * FF_CHUNK] = (a * gelu).astype(BF16)
    y = ALPHA * x + _dot(act[...], wout_ref[...])
    out_ref[0] = _layer_norm(y, g_ref[...], b_ref[...])


def _chunk_interleave(w):
    lead = w.shape[:-1]
    w = w.reshape(lead + (2, D_FF // FF_CHUNK, FF_CHUNK))
    w = jnp.swapaxes(w, -3, -2)
    return w.reshape(lead + (2 * D_FF,))


def _ffn_ln(x3d, w_in, conv_w, conv_b, w_out, ln_g, ln_b, tm):
    b, s, d = x3d.shape
    nb8 = tm // SUBLANES
    n8 = s // SUBLANES
    win = _chunk_interleave(w_in).astype(BF16)
    cw = _chunk_interleave(conv_w)
    cb = _chunk_interleave(conv_b)[None]
    return pl.pallas_call(
        functools.partial(_ffn_ln_kernel, tm=tm),
        grid=(b, s // tm),
        in_specs=[pl.BlockSpec((1, tm, d), lambda bi, si: (bi, si, 0)),
                  pl.BlockSpec((1, SUBLANES, d), lambda bi, si: (bi, jnp.maximum(si * nb8 - 1, 0), 0)),
                  pl.BlockSpec((1, SUBLANES, d), lambda bi, si: (bi, jnp.minimum((si + 1) * nb8, n8 - 1), 0)),
                  _resident((d, 2 * D_FF)), _resident((3, 2 * D_FF)), _resident((1, 2 * D_FF)),
                  _resident((D_FF, d)), _resident((1, d)), _resident((1, d))],
        out_specs=pl.BlockSpec((1, tm, d), lambda bi, si: (bi, si, 0)),
        out_shape=jax.ShapeDtypeStruct((b, s, d), F32),
        scratch_shapes=[pltpu.VMEM((tm, D_FF), BF16)],
        compiler_params=_cparams(("parallel", "parallel")),
        name="ffn_ln",
    )(x3d, x3d, x3d, win, cw, cb, w_out.astype(BF16), ln_g, ln_b)


def _rope_tables(seq):
    inv = 1.0 / (ROPE_THETA ** (jnp.arange(0, QK_ROPE, 2, dtype=F32) / QK_ROPE))
    ang = jnp.arange(seq, dtype=F32)[:, None] * inv[None, :]
    cos, sin = jnp.cos(ang), jnp.sin(ang)
    half = QK_ROPE // 2
    ones = jnp.ones((seq, QK_NOPE), F32)
    z_nope = jnp.zeros((seq, QK_NOPE), F32)
    z_half = jnp.zeros((seq, half), F32)
    z_pad = jnp.zeros((seq, HEAD_PAD - QK_NOPE - QK_ROPE), F32)
    cmul = jnp.concatenate([ones, cos, cos, z_pad], axis=1)
    s_first = jnp.concatenate([z_nope, -sin, z_half, z_pad], axis=1)
    s_second = jnp.concatenate([z_nope, z_half, sin, z_pad], axis=1)
    return cmul, s_first, s_second


def _rope(t, cmul, s_first, s_second):
    half = QK_ROPE // 2
    return (t * cmul + pltpu.roll(t, HEAD_PAD - half, 1) * s_first
            + pltpu.roll(t, half, 1) * s_second)


def _rms_norm(x, g):
    ms = jnp.mean(x * x, axis=-1, keepdims=True)
    return x * lax.rsqrt(ms + RMS_EPS) * g


def _mla_proj_kernel(x_ref, wd_ref, gq_ref, gkv_ref, wuqt_ref, wuk_ref, wuvt_ref,
                     cm_ref, s1_ref, s2_ref, cmt_ref, s1t_ref, s2t_ref, qt_ref, k_ref, vt_ref):
    xb = x_ref[0].astype(BF16)
    c = _dot(xb, wd_ref[...])
    cqn = _rms_norm(c[:, :Q_LORA], gq_ref[...]).astype(BF16)
    ckvn = _rms_norm(c[:, Q_LORA:Q_LORA + KV_LORA], gkv_ref[...]).astype(BF16)
    cm, s1, s2 = cm_ref[...], s1_ref[...], s2_ref[...]
    kr = _rope(c[:, Q_LORA + KV_LORA:], cm, s1, s2)
    k = _dot(ckvn, wuk_ref[...])
    for h in range(B_HEADS):
        cols = slice(h * HEAD_PAD, (h + 1) * HEAD_PAD)
        k_ref[0, :, cols] = (k[:, cols] + kr).astype(BF16)
    qt = _dot_nt(wuqt_ref[...], cqn)
    cmt, s1t, s2t = cmt_ref[...], s1t_ref[...], s2t_ref[...]
    half = QK_ROPE // 2
    for h in range(B_HEADS):
        rows = slice(h * HEAD_PAD, (h + 1) * HEAD_PAD)
        t = qt[rows]
        rot = t * cmt + pltpu.roll(t, HEAD_PAD - half, 0) * s1t + pltpu.roll(t, half, 0) * s2t
        qt_ref[0, rows, :] = (rot * QK_SCALE_LOG2E).astype(BF16)
    vt = _dot_nt(wuvt_ref[...], ckvn).astype(BF16)
    tm = vt.shape[1]
    row = lax.broadcasted_iota(jnp.int32, (V_ROWS - V_HEAD, tm), 0)
    ones_row = jnp.where(row == 0, 1.0, 0.0).astype(BF16)
    for h in range(B_HEADS):
        vt_ref[0, h * V_ROWS:h * V_ROWS + V_HEAD, :] = vt[h * V_HEAD:(h + 1) * V_HEAD]
        vt_ref[0, h * V_ROWS + V_HEAD:(h + 1) * V_ROWS, :] = ones_row


def _mla_weights(w_dkv, w_uq, w_ukv):
    d = w_dkv.shape[0]
    pad_l = jnp.zeros((d, QK_NOPE), F32)
    pad_r = jnp.zeros((d, HEAD_PAD - QK_NOPE - QK_ROPE), F32)
    wd = jnp.concatenate([w_dkv[:, :Q_LORA + KV_LORA], pad_l, w_dkv[:, Q_LORA + KV_LORA:], pad_r], axis=1)
    wq = w_uq.reshape(Q_LORA, B_HEADS, QK_NOPE + QK_ROPE)
    wq = jnp.pad(wq, ((0, 0), (0, 0), (0, HEAD_PAD - QK_NOPE - QK_ROPE))).reshape(Q_LORA, B_HEADS * HEAD_PAD)
    wkv = w_ukv.reshape(KV_LORA, B_HEADS, QK_NOPE + V_HEAD)
    wk = jnp.pad(wkv[:, :, :QK_NOPE], ((0, 0), (0, 0), (0, HEAD_PAD - QK_NOPE))).reshape(KV_LORA, B_HEADS * HEAD_PAD)
    wvt = wkv[:, :, QK_NOPE:].reshape(KV_LORA, B_HEADS * V_HEAD).T
    return wd.astype(BF16), wq.T.astype(BF16), wk.astype(BF16), wvt.astype(BF16)


def _mla_proj(x3d, w_dkv, g_q, g_kv, w_uq, w_ukv, tm):
    b, s, d = x3d.shape
    wd, wqt, wk, wvt = _mla_weights(w_dkv, w_uq, w_ukv)
    tabs = _rope_tables(s)
    tabs_t = [t.T for t in tabs]
    hq = B_HEADS * HEAD_PAD
    hv = B_HEADS * V_ROWS
    tab = pl.BlockSpec((tm, HEAD_PAD), lambda bi, si: (si, 0))
    tab_t = pl.BlockSpec((HEAD_PAD, tm), lambda bi, si: (0, si))
    return pl.pallas_call(
        _mla_proj_kernel,
        grid=(b, s // tm),
        in_specs=[pl.BlockSpec((1, tm, d), lambda bi, si: (bi, si, 0)),
                  _resident(wd.shape), _resident((1, Q_LORA)), _resident((1, KV_LORA)),
                  _resident(wqt.shape), _resident(wk.shape), _resident(wvt.shape),
                  tab, tab, tab, tab_t, tab_t, tab_t],
        out_specs=[pl.BlockSpec((1, hq, tm), lambda bi, si: (bi, 0, si)),
                   pl.BlockSpec((1, tm, hq), lambda bi, si: (bi, si, 0)),
                   pl.BlockSpec((1, hv, tm), lambda bi, si: (bi, 0, si))],
        out_shape=[jax.ShapeDtypeStruct((b, hq, s), BF16),
                   jax.ShapeDtypeStruct((b, s, hq), BF16),
                   jax.ShapeDtypeStruct((b, hv, s), BF16)],
        compiler_params=_cparams(("parallel", "parallel")),
        name="mla_proj",
    )(x3d, wd, g_q[None], g_kv[None], wqt, wk, wvt, *tabs, *tabs_t)


def _mla_attn_kernel(qt_ref, k_ref, vt_ref, o_ref, s_scr, cmax_scr, m_all, acc_all, *, tq, tk, n_q, unroll):
    seq = k_ref.shape[1]
    nk = seq // tk
    blk_q = n_q * tq
    n_blk = seq // blk_q
    total = n_blk * nk
    trips_per_blk = nk // unroll

    def finalize(blk):
        par = blk % 2
        for j in range(n_q):
            acc = acc_all[par, j]
            q0 = pl.multiple_of(blk * blk_q + j * tq, tq)
            o_ref[0, :, pl.ds(q0, tq)] = (acc[:V_HEAD] / acc[V_HEAD:V_HEAD + 1]).astype(BF16)

    def run_step(t, t_static, do_qk=True):
        s_qk, s_sm = t_static % N_SLOTS, (t_static - 1) % N_SLOTS
        n_sm = t - 1 + nk
        par_sm, first_sm = (n_sm // nk + 1) % 2, (n_sm % nk) == 0
        v0 = pl.multiple_of((n_sm % nk) * tk, tk)
        vt = vt_ref[0, :, pl.ds(v0, tk)]
        if do_qk:
            k0 = pl.multiple_of((t % nk) * tk, tk)
            k = k_ref[0, pl.ds(k0, tk), :]
            for j in range(n_q):
                q0 = pl.multiple_of((t // nk) * blk_q + j * tq, tq)
                sc = _dot(k, qt_ref[0, :, pl.ds(q0, tq)])
                s_scr[s_qk, j] = sc
                cmax_scr[s_qk, j] = jnp.max(sc, axis=0, keepdims=True)
        for j in range(n_q):
            m = jnp.where(first_sm, NEG_INF, m_all[par_sm, j])
            m_new = jnp.maximum(m, cmax_scr[s_sm, j])
            alpha = jnp.exp2(m - m_new)
            m_all[par_sm, j] = m_new
            p = jnp.exp2(s_scr[s_sm, j] - m_new).astype(BF16)
            acc_all[par_sm, j] = alpha * acc_all[par_sm, j] + _dot(vt, p)

    s_scr[...] = jnp.zeros(s_scr.shape, F32)
    cmax_scr[...] = jnp.zeros(cmax_scr.shape, F32)
    m_all[...] = jnp.full(m_all.shape, NEG_INF, F32)
    acc_all[...] = jnp.zeros(acc_all.shape, F32)

    def trip(u, carry):
        @pl.when((u % trips_per_blk == 1) & (u > trips_per_blk))
        def _():
            finalize(u // trips_per_blk - 1)

        for i in range(unroll):
            run_step(unroll * u + i, i)
        return carry

    lax.fori_loop(0, total // unroll, trip, 0)
    run_step(total, total, do_qk=False)
    finalize(n_blk - 1)


def _mla_attn(qt, k, vt, n_q, tq, tk):
    b, s, _ = k.shape
    nk = s // tk
    unroll = max(u for u in range(N_SLOTS, MAX_UNROLL + 1, N_SLOTS) if nk % u == 0 and nk // u >= 2)
    assert s % (n_q * tq) == 0 and N_SLOTS == 2
    return pl.pallas_call(
        functools.partial(_mla_attn_kernel, tq=tq, tk=tk, n_q=n_q, unroll=unroll),
        grid=(b, B_HEADS),
        in_specs=[pl.BlockSpec((1, HEAD_PAD, s), lambda bi, h: (bi, h, 0)),
                  pl.BlockSpec((1, s, HEAD_PAD), lambda bi, h: (bi, 0, h)),
                  pl.BlockSpec((1, V_ROWS, s), lambda bi, h: (bi, h, 0))],
        out_specs=pl.BlockSpec((1, V_HEAD, s), lambda bi, h: (bi, h, 0)),
        out_shape=jax.ShapeDtypeStruct((b, B_HEADS * V_HEAD, s), BF16),
        scratch_shapes=[pltpu.VMEM((N_SLOTS, n_q, tk, tq), F32), pltpu.VMEM((N_SLOTS, n_q, 1, tq), F32),
                        pltpu.VMEM((2, n_q, 1, tq), F32), pltpu.VMEM((2, n_q, V_ROWS, tq), F32)],
        compiler_params=_cparams(("parallel", "parallel")),
        name="mla_attn",
    )(qt, k, vt)


def _proj_t_ln_kernel(ot_ref, x_ref, w_ref, g_ref, b_ref, out_ref):
    y = ALPHA * x_ref[0] + _dot_tn(ot_ref[0], w_ref[...])
    out_ref[0] = _layer_norm(y, g_ref[...], b_ref[...])


def _proj_t_ln(ot, x3d, w_bf16, ln_g, ln_b, tm):
    b, s, d = x3d.shape
    k = ot.shape[1]
    return pl.pallas_call(
        _proj_t_ln_kernel,
        grid=(b, s // tm),
        in_specs=[pl.BlockSpec((1, k, tm), lambda bi, si: (bi, 0, si)),
                  pl.BlockSpec((1, tm, d), lambda bi, si: (bi, si, 0)),
                  _resident((k, d)), _resident((1, d)), _resident((1, d))],
        out_specs=pl.BlockSpec((1, tm, d), lambda bi, si: (bi, si, 0)),
        out_shape=jax.ShapeDtypeStruct((b, s, d), F32),
        compiler_params=_cparams(("parallel", "parallel")),
        name="proj_t_ln",
    )(ot, x3d, w_bf16, ln_g, ln_b)


def _encoder(x, bias, w_qkv_a, w_o_a, w_dkv_b, g_q_b, g_kv_b, w_uq_b, w_ukv_b, w_o_b,
             ffn_w_in, ffn_conv_w, ffn_conv_b, ffn_w_out, ln_g, ln_b):
    b, s, d = x.shape
    tm = ROW_TILE
    for i in range(DEPTH):
        j = i // N_MIXERS
        g1, b1 = ln_g[i, 0][None], ln_b[i, 0][None]
        g2, b2 = ln_g[i, 1][None], ln_b[i, 1][None]
        if i % N_MIXERS == 0:
            qkv = _qkv_proj(x, w_qkv_a[j].astype(BF16), QKV_ROW_TILE)
            outs = [_dilated_attn(qkv[g], bias, g) for g in range(A_GROUPS)]
            x = _merge_proj_ln([o for o, _ in outs], [l for _, l in outs], x,
                               w_o_a[j].astype(BF16), g1, b1, tm)
        else:
            q, k, vt = _mla_proj(x, w_dkv_b[j], g_q_b[j], g_kv_b[j], w_uq_b[j], w_ukv_b[j], tm)
            ot = _mla_attn(q, k, vt, MLA_Q_TILES, MLA_Q_TILE, MLA_K_TILE)
            x = _proj_t_ln(ot, x, w_o_b[j].astype(BF16), g1, b1, tm)
        x = _ffn_ln(x, ffn_w_in[i], ffn_conv_w[i], ffn_conv_b[i], ffn_w_out[i], g2, b2, tm)
    return x


def kernel(x_prompt, x_sample, rel_bias, w_qkv_a, w_o_a, w_dkv_b, g_q_b, g_kv_b, w_uq_b, w_ukv_b, w_o_b,
           ffn_w_in, ffn_conv_w, ffn_conv_b, ffn_w_out, ln_g, ln_b):
    bias = _bias_tiles(rel_bias)
    args = (bias, w_qkv_a, w_o_a, w_dkv_b, g_q_b, g_kv_b, w_uq_b, w_ukv_b, w_o_b,
            ffn_w_in, ffn_conv_w, ffn_conv_b, ffn_w_out, ln_g, ln_b)
    return (_encoder(x_prompt, *args), _encoder(x_sample, *args))
```

```python
import functools
import math

import jax
import jax.numpy as jnp
import numpy as np
from jax import lax
from jax.experimental import pallas as pl
from jax.experimental.pallas import tpu as pltpu

F32 = jnp.float32
BF16 = jnp.bfloat16

D_MODEL = 1024
DEPTH = 2
N_MIXERS = 2

A_WINDOWS = (128, 512, 2048)
A_DILATIONS = (1, 4, 16)
A_GROUPS = 3
A_HEADS = 8
A_HEAD_DIM = 128
A_WIDTH = A_HEADS * A_HEAD_DIM
A_HALF = 64
NUM_BUCKETS = 32
MAX_DISTANCE = max(A_WINDOWS) // 2

B_HEADS = 16
Q_LORA = 384
KV_LORA = 256
QK_NOPE = 64
QK_ROPE = 32
V_HEAD = 64
ROPE_THETA = 10000.0
HEAD_PAD = 128
V_ROWS = V_HEAD + 16
N_SLOTS = 2
MAX_UNROLL = 8

D_FF = 2816
FF_CHUNK = 256

ALPHA = (2.0 * DEPTH) ** 0.25
LN_EPS = 1e-5
RMS_EPS = 1e-6
NEG_INF = -1e30

LANES = 128
SUBLANES = 8
VMEM_LIMIT = 56 * 1024 * 1024

ROW_TILE = 512
QKV_ROW_TILE = 256
DILATED_Q_TILE = 1024
MLA_Q_TILES = 4
MLA_Q_TILE = 256
MLA_K_TILE = 256

QK_SCALE_LOG2E = np.float32((QK_NOPE + QK_ROPE) ** -0.5 * math.log2(math.e))


def _cparams(sem):
    return pltpu.CompilerParams(dimension_semantics=sem, vmem_limit_bytes=VMEM_LIMIT)


def _resident(shape):
    nd = len(shape)
    return pl.BlockSpec(shape, lambda *_: (0,) * nd, pipeline_mode=pl.Buffered(1))


def _layer_norm(y, g, b):
    mu = jnp.mean(y, axis=-1, keepdims=True)
    yc = y - mu
    var = jnp.mean(yc * yc, axis=-1, keepdims=True)
    return yc * lax.rsqrt(var + LN_EPS) * g + b


def _dot(a, b):
    return jnp.dot(a, b, preferred_element_type=F32)


def _dot_nt(a, b):
    return lax.dot_general(a, b, (((1,), (1,)), ((), ())), preferred_element_type=F32)


def _dot_tn(a, b):
    return lax.dot_general(a, b, (((0,), (0,)), ((), ())), preferred_element_type=F32)


def _qkv_proj_kernel(x_ref, w_ref, *refs, tm):
    outs, x_planes, x_perm = refs[:A_GROUPS], refs[A_GROUPS], refs[A_GROUPS + 1]
    x = x_ref[0]
    n_planes = x.shape[1] // LANES
    for h in range(n_planes):
        x_planes[h] = x[:, h * LANES:(h + 1) * LANES]
    for g, dil in enumerate(A_DILATIONS):
        n = tm // dil
        if dil == 1:
            xb = x.astype(BF16)
        else:
            xp = x_perm.at[g - 1]
            for r in range(dil):
                for h in range(n_planes):
                    xp[r * n:(r + 1) * n, h * LANES:(h + 1) * LANES] = x_planes[h, pl.ds(r, n, stride=dil), :]
            xb = xp[...].astype(BF16)
        for c in range(3):
            n0 = (g * 3 + c) * A_WIDTH
            cols = slice(c * A_WIDTH, (c + 1) * A_WIDTH)
            res = _dot(xb, w_ref[:, n0:n0 + A_WIDTH]).astype(BF16)
            for r in range(dil):
                outs[g][0, r, :, cols] = res[r * n:(r + 1) * n]


def _qkv_proj(x3d, w_bf16, tm):
    b, s, d = x3d.shape
    width = 3 * A_WIDTH
    return pl.pallas_call(
        functools.partial(_qkv_proj_kernel, tm=tm),
        grid=(b, s // tm),
        in_specs=[pl.BlockSpec((1, tm, d), lambda bi, i: (bi, i, 0)), _resident(w_bf16.shape)],
        out_specs=[pl.BlockSpec((1, dil, tm // dil, width), lambda bi, i: (bi, 0, i, 0))
                   for dil in A_DILATIONS],
        out_shape=[jax.ShapeDtypeStruct((b, dil, s // dil, width), BF16) for dil in A_DILATIONS],
        scratch_shapes=[pltpu.VMEM((d // LANES, tm, LANES), F32),
                        pltpu.VMEM((sum(dil > 1 for dil in A_DILATIONS), tm, d), F32)],
        compiler_params=_cparams(("parallel", "parallel")),
        name="qkv_proj",
    )(x3d, w_bf16)


def _t5_bucket(rel):
    nb = NUM_BUCKETS // 2
    max_exact = nb // 2
    ret = jnp.where(rel > 0, nb, 0)
    n = jnp.abs(rel)
    large = max_exact + (jnp.log(jnp.maximum(n, 1).astype(F32) / max_exact)
                         / math.log(MAX_DISTANCE / max_exact) * (nb - max_exact)).astype(jnp.int32)
    large = jnp.minimum(large, nb - 1)
    return ret + jnp.where(n < max_exact, n, large)


def _band_bucket_index():
    qi = jnp.arange(2 * A_HALF)[:, None]
    kj = jnp.arange(4 * A_HALF)[None, :]
    rel = kj - A_HALF - qi
    tiles = [jnp.where(jnp.abs(rel) <= A_HALF, _t5_bucket(rel * dil), -1) for dil in A_DILATIONS]
    return jnp.stack(tiles, axis=0).astype(jnp.int32)


def _bias_tile_kernel(tab_ref, idx_ref, o_ref):
    col = pl.program_id(0) * A_HEADS + pl.program_id(1)
    idx = idx_ref[0]
    acc = jnp.full(idx.shape, NEG_INF, F32)
    for bucket in range(NUM_BUCKETS):
        acc = jnp.where(idx == bucket, tab_ref[bucket, col], acc)
    o_ref[0, 0] = acc


def _bias_tiles(rel_bias):
    idx = _band_bucket_index()
    tq, tk = idx.shape[1:]
    return pl.pallas_call(
        _bias_tile_kernel,
        grid=(A_GROUPS, A_HEADS),
        in_specs=[pl.BlockSpec(memory_space=pltpu.SMEM),
                  pl.BlockSpec((1, tq, tk), lambda g, h: (g, 0, 0))],
        out_specs=pl.BlockSpec((1, 1, tq, tk), lambda g, h: (g, h, 0, 0)),
        out_shape=jax.ShapeDtypeStruct((A_GROUPS, A_HEADS, tq, tk), F32),
        compiler_params=_cparams(("arbitrary", "arbitrary")),
        name="bias_tiles",
    )(rel_bias, idx)


def _dilated_attn_kernel(q_ref, kl_ref, km_ref, kr_ref, vl_ref, vm_ref, vr_ref, bias_ref,
                         o_ref, lse_ref, kbuf, vbuf, *, tq, sub_len):
    li = pl.program_id(2)
    kbuf[0:A_HALF] = kl_ref[0, 0]
    kbuf[A_HALF:A_HALF + tq] = km_ref[0, 0]
    kbuf[A_HALF + tq:] = kr_ref[0, 0]
    vbuf[0:A_HALF] = vl_ref[0, 0]
    vbuf[A_HALF:A_HALF + tq] = vm_ref[0, 0]
    vbuf[A_HALF + tq:] = vr_ref[0, 0]

    scale = A_HEAD_DIM ** -0.5
    qt, kt = 2 * A_HALF, 4 * A_HALF
    lane = lax.broadcasted_iota(jnp.int32, (qt, LANES), 1)
    for t in range(tq // qt):
        pos = li * tq + t * qt - A_HALF + lax.broadcasted_iota(jnp.int32, (1, kt), 1)
        edge = jnp.where((pos >= 0) & (pos < sub_len), 0.0, NEG_INF).astype(F32)
        lse_tile = jnp.zeros((qt, LANES), F32)
        for h in range(A_HEADS):
            cols = slice(h * A_HEAD_DIM, (h + 1) * A_HEAD_DIM)
            q = q_ref[0, 0, t * qt:(t + 1) * qt, cols]
            k = kbuf[t * qt:t * qt + kt, cols]
            v = vbuf[t * qt:t * qt + kt, cols]
            s = _dot_nt(q, k) * scale + bias_ref[0, h] + edge
            m = jnp.max(s, axis=-1, keepdims=True)
            p = jnp.exp(s - m)
            den = jnp.sum(p, axis=-1, keepdims=True)
            o = _dot(p.astype(BF16), v) / den
            o_ref[0, 0, t * qt:(t + 1) * qt, cols] = o.astype(BF16)
            lse_tile = jnp.where(lane == h, m + jnp.log(den), lse_tile)
        lse_ref[0, 0, t * qt:(t + 1) * qt, :] = lse_tile


def _dilated_attn(qkv_g, bias, g):
    batch, dil, sub_len, _ = qkv_g.shape
    tq = min(DILATED_Q_TILE, sub_len)
    n_t = sub_len // tq
    hb = tq // A_HALF
    n_hb = sub_len // A_HALF

    main = lambda which: pl.BlockSpec((1, 1, tq, A_WIDTH), lambda b, r, i: (b, r, i, which))
    left = lambda which: pl.BlockSpec(
        (1, 1, A_HALF, A_WIDTH), lambda b, r, i: (b, r, jnp.maximum(i * hb - 1, 0), which))
    right = lambda which: pl.BlockSpec(
        (1, 1, A_HALF, A_WIDTH), lambda b, r, i: (b, r, jnp.minimum((i + 1) * hb, n_hb - 1), which))

    return pl.pallas_call(
        functools.partial(_dilated_attn_kernel, tq=tq, sub_len=sub_len),
        grid=(batch, dil, n_t),
        in_specs=[main(0), left(1), main(1), right(1), left(2), main(2), right(2),
                  pl.BlockSpec((1, A_HEADS, 2 * A_HALF, 4 * A_HALF), lambda b, r, i: (g, 0, 0, 0))],
        out_specs=[pl.BlockSpec((1, 1, tq, A_WIDTH), lambda b, r, i: (b, r, i, 0)),
                   pl.BlockSpec((1, 1, tq, LANES), lambda b, r, i: (b, r, i, 0))],
        out_shape=[jax.ShapeDtypeStruct((batch, dil, sub_len, A_WIDTH), BF16),
                   jax.ShapeDtypeStruct((batch, dil, sub_len, LANES), F32)],
        scratch_shapes=[pltpu.VMEM((tq + 2 * A_HALF, A_WIDTH), BF16),
                        pltpu.VMEM((tq + 2 * A_HALF, A_WIDTH), BF16)],
        compiler_params=_cparams(("parallel", "parallel", "parallel")),
        name=f"dilated_attn_g{g}",
    )(qkv_g, qkv_g, qkv_g, qkv_g, qkv_g, qkv_g, qkv_g, bias)


def _merge_proj_ln_kernel(o0_ref, o1_ref, o2_ref, l0_ref, l1_ref, l2_ref, x_ref, w_ref, g_ref, b_ref,
                          out_ref, merged, o_scr, l_scr, *, tm):
    for g, (o_ref, l_ref, dil) in enumerate(zip((o0_ref, o1_ref, o2_ref), (l0_ref, l1_ref, l2_ref),
                                                A_DILATIONS)):
        for r in range(dil):
            rows = pl.ds(r, tm // dil, stride=dil) if dil > 1 else slice(None)
            l_scr[g, rows, :] = l_ref[0, r]
            for h in range(A_HEADS):
                o_scr[g, h, rows, :] = o_ref[0, r, :, h * A_HEAD_DIM:(h + 1) * A_HEAD_DIM].astype(F32)
    l0, l1, l2 = l_scr[0], l_scr[1], l_scr[2]
    m = jnp.maximum(jnp.maximum(l0, l1), l2)
    e0, e1, e2 = jnp.exp(l0 - m), jnp.exp(l1 - m), jnp.exp(l2 - m)
    den = e0 + e1 + e2
    w0, w1, w2 = e0 / den, e1 / den, e2 / den
    for h in range(A_HEADS):
        cols = slice(h * A_HEAD_DIM, (h + 1) * A_HEAD_DIM)
        mix = (w0[:, h:h + 1] * o_scr[0, h] + w1[:, h:h + 1] * o_scr[1, h]
               + w2[:, h:h + 1] * o_scr[2, h])
        merged[:, cols] = mix.astype(BF16)
    y = ALPHA * x_ref[0] + _dot(merged[...], w_ref[...])
    out_ref[0] = _layer_norm(y, g_ref[...], b_ref[...])


def _merge_proj_ln(os, lses, x3d, w_bf16, ln_g, ln_b, tm):
    b, s, d = x3d.shape
    plane = lambda w: [pl.BlockSpec((1, dil, tm // dil, w), lambda bi, i: (bi, 0, i, 0))
                       for dil in A_DILATIONS]
    row = pl.BlockSpec((1, tm, d), lambda bi, i: (bi, i, 0))
    return pl.pallas_call(
        functools.partial(_merge_proj_ln_kernel, tm=tm),
        grid=(b, s // tm),
        in_specs=plane(A_WIDTH) + plane(LANES) + [row, _resident((A_WIDTH, d)),
                                                  _resident((1, d)), _resident((1, d))],
        out_specs=row,
        out_shape=jax.ShapeDtypeStruct((b, s, d), F32),
        scratch_shapes=[pltpu.VMEM((tm, A_WIDTH), BF16),
                        pltpu.VMEM((A_GROUPS, A_HEADS, tm, A_HEAD_DIM), F32),
                        pltpu.VMEM((A_GROUPS, tm, LANES), F32)],
        compiler_params=_cparams(("parallel", "parallel")),
        name="merge_proj_ln",
    )(*os, *lses, x3d, w_bf16, ln_g, ln_b)


def _ffn_ln_kernel(x_ref, xp_ref, xn_ref, win_ref, cw_ref, cb_ref, wout_ref, g_ref, b_ref,
                   out_ref, act, *, tm):
    si = pl.program_id(1)
    n_s = pl.num_programs(1)
    x = x_ref[0]
    xb = x.astype(BF16)
    prev_ok = (si > 0).astype(F32)
    next_ok = (si < n_s - 1).astype(F32)
    xe = jnp.concatenate([xp_ref[0] * prev_ok, xn_ref[0] * next_ok], axis=0).astype(BF16)
    xall = jnp.concatenate([xb, xe], axis=0)
    row = lax.broadcasted_iota(jnp.int32, (tm, 2 * FF_CHUNK), 0)
    for c in range(D_FF // FF_CHUNK):
        cols = slice(c * 2 * FF_CHUNK, (c + 1) * 2 * FF_CHUNK)
        hall = _dot(xall, win_ref[:, cols])
        h = hall[:tm]
        he = hall[tm:]
        up = jnp.where(row == 0, he[SUBLANES - 1:SUBLANES], pltpu.roll(h, 1, 0))
        dn = jnp.where(row == tm - 1, he[SUBLANES:SUBLANES + 1], pltpu.roll(h, tm - 1, 0))
        cw = cw_ref[:, cols]
        hc = up * cw[0:1] + h * cw[1:2] + dn * cw[2:3] + cb_ref[:, cols]
        a = hc[:, :FF_CHUNK]
        gate = hc[:, FF_CHUNK:]
        gelu = 0.5 * gate * (1.0 + lax.erf(gate * np.float32(math.sqrt(0.5))))
        act[:, c * FF_CHUNK:(c + 1) * FF_CHUNK] = (a * gelu).astype(BF16)
    y = ALPHA * x + _dot(act[...], wout_ref[...])
    out_ref[0] = _layer_norm(y, g_ref[...], b_ref[...])


def _chunk_interleave(w):
    lead = w.shape[:-1]
    w = w.reshape(lead + (2, D_FF // FF_CHUNK, FF_CHUNK))
    w = jnp.swapaxes(w, -3, -2)
    return w.reshape(lead + (2 * D_FF,))


def _ffn_ln(x3d, w_in, conv_w, conv_b, w_out, ln_g, ln_b, tm):
    b, s, d = x3d.shape
    nb8 = tm // SUBLANES
    n8 = s // SUBLANES
    win = _chunk_interleave(w_in).astype(BF16)
    cw = _chunk_interleave(conv_w)
    cb = _chunk_interleave(conv_b)[None]
    return pl.pallas_call(
        functools.partial(_ffn_ln_kernel, tm=tm),
        grid=(b, s // tm),
        in_specs=[pl.BlockSpec((1, tm, d), lambda bi, si: (bi, si, 0)),
                  pl.BlockSpec((1, SUBLANES, d), lambda bi, si: (bi, jnp.maximum(si * nb8 - 1, 0), 0)),
                  pl.BlockSpec((1, SUBLANES, d), lambda bi, si: (bi, jnp.minimum((si + 1) * nb8, n8 - 1), 0)),
                  _resident((d, 2 * D_FF)), _resident((3, 2 * D_FF)), _resident((1, 2 * D_FF)),
                  _resident((D_FF, d)), _resident((1, d)), _resident((1, d))],
        out_specs=pl.BlockSpec((1, tm, d), lambda bi, si: (bi, si, 0)),
        out_shape=jax.ShapeDtypeStruct((b, s, d), F32),
        scratch_shapes=[pltpu.VMEM((tm, D_FF), BF16)],
        compiler_params=_cparams(("parallel", "parallel")),
        name="ffn_ln",
    )(x3d, x3d, x3d, win, cw, cb, w_out.astype(BF16), ln_g, ln_b)


def _rope_tables(seq):
    inv = 1.0 / (ROPE_THETA ** (jnp.arange(0, QK_ROPE, 2, dtype=F32) / QK_ROPE))
    ang = jnp.arange(seq, dtype=F32)[:, None] * inv[None, :]
    cos, sin = jnp.cos(ang), jnp.sin(ang)
    half = QK_ROPE // 2
    ones = jnp.ones((seq, QK_NOPE), F32)
    z_nope = jnp.zeros((seq, QK_NOPE), F32)
    z_half = jnp.zeros((seq, half), F32)
    z_pad = jnp.zeros((seq, HEAD_PAD - QK_NOPE - QK_ROPE), F32)
    cmul = jnp.concatenate([ones, cos, cos, z_pad], axis=1)
    s_first = jnp.concatenate([z_nope, -sin, z_half, z_pad], axis=1)
    s_second = jnp.concatenate([z_nope, z_half, sin, z_pad], axis=1)
    return cmul, s_first, s_second


def _rope(t, cmul, s_first, s_second):
    half = QK_ROPE // 2
    return (t * cmul + pltpu.roll(t, HEAD_PAD - half, 1) * s_first
            + pltpu.roll(t, half, 1) * s_second)


def _rms_norm(x, g):
    ms = jnp.mean(x * x, axis=-1, keepdims=True)
    return x * lax.rsqrt(ms + RMS_EPS) * g


def _mla_proj_kernel(x_ref, wd_ref, gq_ref, gkv_ref, wuqt_ref, wuk_ref, wuvt_ref,
                     cm_ref, s1_ref, s2_ref, cmt_ref, s1t_ref, s2t_ref, qt_ref, k_ref, vt_ref):
    xb = x_ref[0].astype(BF16)
    c = _dot(xb, wd_ref[...])
    cqn = _rms_norm(c[:, :Q_LORA], gq_ref[...]).astype(BF16)
    ckvn = _rms_norm(c[:, Q_LORA:Q_LORA + KV_LORA], gkv_ref[...]).astype(BF16)
    cm, s1, s2 = cm_ref[...], s1_ref[...], s2_ref[...]
    kr = _rope(c[:, Q_LORA + KV_LORA:], cm, s1, s2)
    k = _dot(ckvn, wuk_ref[...])
    for h in range(B_HEADS):
        cols = slice(h * HEAD_PAD, (h + 1) * HEAD_PAD)
        k_ref[0, :, cols] = (k[:, cols] + kr).astype(BF16)
    qt = _dot_nt(wuqt_ref[...], cqn)
    cmt, s1t, s2t = cmt_ref[...], s1t_ref[...], s2t_ref[...]
    half = QK_ROPE // 2
    for h in range(B_HEADS):
        rows = slice(h * HEAD_PAD, (h + 1) * HEAD_PAD)
        t = qt[rows]
        rot = t * cmt + pltpu.roll(t, HEAD_PAD - half, 0) * s1t + pltpu.roll(t, half, 0) * s2t
        qt_ref[0, rows, :] = (rot * QK_SCALE_LOG2E).astype(BF16)
    vt = _dot_nt(wuvt_ref[...], ckvn).astype(BF16)
    tm = vt.shape[1]
    row = lax.broadcasted_iota(jnp.int32, (V_ROWS - V_HEAD, tm), 0)
    ones_row = jnp.where(row == 0, 1.0, 0.0).astype(BF16)
    for h in range(B_HEADS):
        vt_ref[0, h * V_ROWS:h * V_ROWS + V_HEAD, :] = vt[h * V_HEAD:(h + 1) * V_HEAD]
        vt_ref[0, h * V_ROWS + V_HEAD:(h + 1) * V_ROWS, :] = ones_row


def _mla_weights(w_dkv, w_uq, w_ukv):
    d = w_dkv.shape[0]
    pad_l = jnp.zeros((d, QK_NOPE), F32)
    pad_r = jnp.zeros((d, HEAD_PAD - QK_NOPE - QK_ROPE), F32)
    wd = jnp.concatenate([w_dkv[:, :Q_LORA + KV_LORA], pad_l, w_dkv[:, Q_LORA + KV_LORA:], pad_r], axis=1)
    wq = w_uq.reshape(Q_LORA, B_HEADS, QK_NOPE + QK_ROPE)
    wq = jnp.pad(wq, ((0, 0), (0, 0), (0, HEAD_PAD - QK_NOPE - QK_ROPE))).reshape(Q_LORA, B_HEADS * HEAD_PAD)
    wkv = w_ukv.reshape(KV_LORA, B_HEADS, QK_NOPE + V_HEAD)
    wk = jnp.pad(wkv[:, :, :QK_NOPE], ((0, 0), (0, 0), (0, HEAD_PAD - QK_NOPE))).reshape(KV_LORA, B_HEADS * HEAD_PAD)
    wvt = wkv[:, :, QK_NOPE:].reshape(KV_LORA, B_HEADS * V_HEAD).T
    return wd.astype(BF16), wq.T.astype(BF16), wk.astype(BF16), wvt.astype(BF16)


def _mla_proj(x3d, w_dkv, g_q, g_kv, w_uq, w_ukv, tm):
    b, s, d = x3d.shape
    wd, wqt, wk, wvt = _mla_weights(w_dkv, w_uq, w_ukv)
    tabs = _rope_tables(s)
    tabs_t = [t.T for t in tabs]
    hq = B_HEADS * HEAD_PAD
    hv = B_HEADS * V_ROWS
    tab = pl.BlockSpec((tm, HEAD_PAD), lambda bi, si: (si, 0))
    tab_t = pl.BlockSpec((HEAD_PAD, tm), lambda bi, si: (0, si))
    return pl.pallas_call(
        _mla_proj_kernel,
        grid=(b, s // tm),
        in_specs=[pl.BlockSpec((1, tm, d), lambda bi, si: (bi, si, 0)),
                  _resident(wd.shape), _resident((1, Q_LORA)), _resident((1, KV_LORA)),
                  _resident(wqt.shape), _resident(wk.shape), _resident(wvt.shape),
                  tab, tab, tab, tab_t, tab_t, tab_t],
        out_specs=[pl.BlockSpec((1, hq, tm), lambda bi, si: (bi, 0, si)),
                   pl.BlockSpec((1, tm, hq), lambda bi, si: (bi, si, 0)),
                   pl.BlockSpec((1, hv, tm), lambda bi, si: (bi, 0, si))],
        out_shape=[jax.ShapeDtypeStruct((b, hq, s), BF16),
                   jax.ShapeDtypeStruct((b, s, hq), BF16),
                   jax.ShapeDtypeStruct((b, hv, s), BF16)],
        compiler_params=_cparams(("parallel", "parallel")),
        name="mla_proj",
    )(x3d, wd, g_q[None], g_kv[None], wqt, wk, wvt, *tabs, *tabs_t)


def _mla_attn_kernel(qt_ref, k_ref, vt_ref, o_ref, s_scr, cmax_scr, m_all, acc_all, *, tq, tk, n_q, unroll):
    seq = k_ref.shape[1]
    nk = seq // tk
    blk_q = n_q * tq
    n_blk = seq // blk_q
    total = n_blk * nk
    trips_per_blk = nk // unroll

    def finalize(blk):
        par = blk % 2
        for j in range(n_q):
            acc = acc_all[par, j]
            q0 = pl.multiple_of(blk * blk_q + j * tq, tq)
            o_ref[0, :, pl.ds(q0, tq)] = (acc[:V_HEAD] / acc[V_HEAD:V_HEAD + 1]).astype(BF16)

    def run_step(t, t_static, do_qk=True):
        s_qk, s_sm = t_static % N_SLOTS, (t_static - 1) % N_SLOTS
        n_sm = t - 1 + nk
        par_sm, first_sm = (n_sm // nk + 1) % 2, (n_sm % nk) == 0
        v0 = pl.multiple_of((n_sm % nk) * tk, tk)
        vt = vt_ref[0, :, pl.ds(v0, tk)]
        if do_qk:
            k0 = pl.multiple_of((t % nk) * tk, tk)
            k = k_ref[0, pl.ds(k0, tk), :]
            for j in range(n_q):
                q0 = pl.multiple_of((t // nk) * blk_q + j * tq, tq)
                sc = _dot(k, qt_ref[0, :, pl.ds(q0, tq)])
                s_scr[s_qk, j] = sc
                cmax_scr[s_qk, j] = jnp.max(sc, axis=0, keepdims=True)
        for j in range(n_q):
            m = jnp.where(first_sm, NEG_INF, m_all[par_sm, j])
            m_new = jnp.maximum(m, cmax_scr[s_sm, j])
            alpha = jnp.exp2(m - m_new)
            m_all[par_sm, j] = m_new
            p = jnp.exp2(s_scr[s_sm, j] - m_new).astype(BF16)
            acc_all[par_sm, j] = alpha * acc_all[par_sm, j] + _dot(vt, p)

    s_scr[...] = jnp.zeros(s_scr.shape, F32)
    cmax_scr[...] = jnp.zeros(cmax_scr.shape, F32)
    m_all[...] = jnp.full(m_all.shape, NEG_INF, F32)
    acc_all[...] = jnp.zeros(acc_all.shape, F32)

    def trip(u, carry):
        @pl.when((u % trips_per_blk == 1) & (u > trips_per_blk))
        def _():
            finalize(u // trips_per_blk - 1)

        for i in range(unroll):
            run_step(unroll * u + i, i)
        return carry

    lax.fori_loop(0, total // unroll, trip, 0)
    run_step(total, total, do_qk=False)
    finalize(n_blk - 1)


def _mla_attn(qt, k, vt, n_q, tq, tk):
    b, s, _ = k.shape
    nk = s // tk
    unroll = max(u for u in range(N_SLOTS, MAX_UNROLL + 1, N_SLOTS) if nk % u == 0 and nk // u >= 2)
    assert s % (n_q * tq) == 0 and N_SLOTS == 2
    return pl.pallas_call(
        functools.partial(_mla_attn_kernel, tq=tq, tk=tk, n_q=n_q, unroll=unroll),
        grid=(b, B_HEADS),
        in_specs=[pl.BlockSpec((1, HEAD_PAD, s), lambda bi, h: (bi, h, 0)),
                  pl.BlockSpec((1, s, HEAD_PAD), lambda bi, h: (bi, 0, h)),
                  pl.BlockSpec((1, V_ROWS, s), lambda bi, h: (bi, h, 0))],
        out_specs=pl.BlockSpec((1, V_HEAD, s), lambda bi, h: (bi, h, 0)),
        out_shape=jax.ShapeDtypeStruct((b, B_HEADS * V_HEAD, s), BF16),
        scratch_shapes=[pltpu.VMEM((N_SLOTS, n_q, tk, tq), F32), pltpu.VMEM((N_SLOTS, n_q, 1, tq), F32),
                        pltpu.VMEM((2, n_q, 1, tq), F32), pltpu.VMEM((2, n_q, V_ROWS, tq), F32)],
        compiler_params=_cparams(("parallel", "parallel")),
        name="mla_attn",
    )(qt, k, vt)


def _proj_t_ln_kernel(ot_ref, x_ref, w_ref, g_ref, b_ref, out_ref):
    y = ALPHA * x_ref[0] + _dot_tn(ot_ref[0], w_ref[...])
    out_ref[0] = _layer_norm(y, g_ref[...], b_ref[...])


def _proj_t_ln(ot, x3d, w_bf16, ln_g, ln_b, tm):
    b, s, d = x3d.shape
    k = ot.shape[1]
    return pl.pallas_call(
        _proj_t_ln_kernel,
        grid=(b, s // tm),
        in_specs=[pl.BlockSpec((1, k, tm), lambda bi, si: (bi, 0, si)),
                  pl.BlockSpec((1, tm, d), lambda bi, si: (bi, si, 0)),
                  _resident((k, d)), _resident((1, d)), _resident((1, d))],
        out_specs=pl.BlockSpec((1, tm, d), lambda bi, si: (bi, si, 0)),
        out_shape=jax.ShapeDtypeStruct((b, s, d), F32),
        compiler_params=_cparams(("parallel", "parallel")),
        name="proj_t_ln",
    )(ot, x3d, w_bf16, ln_g, ln_b)


def _encoder(x, bias, w_qkv_a, w_o_a, w_dkv_b, g_q_b, g_kv_b, w_uq_b, w_ukv_b, w_o_b,
             ffn_w_in, ffn_conv_w, ffn_conv_b, ffn_w_out, ln_g, ln_b):
    b, s, d = x.shape
    tm = ROW_TILE
    for i in range(DEPTH):
        j = i // N_MIXERS
        g1, b1 = ln_g[i, 0][None], ln_b[i, 0][None]
        g2, b2 = ln_g[i, 1][None], ln_b[i, 1][None]
        if i % N_MIXERS == 0:
            qkv = _qkv_proj(x, w_qkv_a[j].astype(BF16), QKV_ROW_TILE)
            outs = [_dilated_attn(qkv[g], bias, g) for g in range(A_GROUPS)]
            x = _merge_proj_ln([o for o, _ in outs], [l for _, l in outs], x,
                               w_o_a[j].astype(BF16), g1, b1, tm)
        else:
            q, k, vt = _mla_proj(x, w_dkv_b[j], g_q_b[j], g_kv_b[j], w_uq_b[j], w_ukv_b[j], tm)
            ot = _mla_attn(q, k, vt, MLA_Q_TILES, MLA_Q_TILE, MLA_K_TILE)
            x = _proj_t_ln(ot, x, w_o_b[j].astype(BF16), g1, b1, tm)
        x = _ffn_ln(x, ffn_w_in[i], ffn_conv_w[i], ffn_conv_b[i], ffn_w_out[i], g2, b2, tm)
    return x


def kernel(x_prompt, x_sample, rel_bias, w_qkv_a, w_o_a, w_dkv_b, g_q_b, g_kv_b, w_uq_b, w_ukv_b, w_o_b,
           ffn_w_in, ffn_conv_w, ffn_conv_b, ffn_w_out, ln_g, ln_b):
    bias = _bias_tiles(rel_bias)
    args = (bias, w_qkv_a, w_o_a, w_dkv_b, g_q_b, g_kv_b, w_uq_b, w_ukv_b, w_o_b,
            ffn_w_in, ffn_conv_w, ffn_conv_b, ffn_w_out, ln_g, ln_b)
    return (_encoder(x_prompt, *args), _encoder(x_sample, *args))
```

```python
import functools
import math

import jax
import jax.numpy as jnp
import numpy as np
from jax import lax
from jax.experimental import pallas as pl
from jax.experimental.pallas import tpu as pltpu

F32 = jnp.float32
BF16 = jnp.bfloat16

D_MODEL = 1024
DEPTH = 2
N_MIXERS = 2

A_WINDOWS = (128, 512, 2048)
A_DILATIONS = (1, 4, 16)
A_GROUPS = 3
A_HEADS = 8
A_HEAD_DIM = 128
A_WIDTH = A_HEADS * A_HEAD_DIM
A_HALF = 64
NUM_BUCKETS = 32
MAX_DISTANCE = max(A_WINDOWS) // 2

B_HEADS = 16
Q_LORA = 384
KV_LORA = 256
QK_NOPE = 64
QK_ROPE = 32
V_HEAD = 64
ROPE_THETA = 10000.0
HEAD_PAD = 128
V_ROWS = V_HEAD + 16
N_SLOTS = 2
MAX_UNROLL = 8

D_FF = 2816
FF_CHUNK = 256

ALPHA = (2.0 * DEPTH) ** 0.25
LN_EPS = 1e-5
RMS_EPS = 1e-6
NEG_INF = -1e30

LANES = 128
SUBLANES = 8
VMEM_LIMIT = 56 * 1024 * 1024

ROW_TILE = 512
QKV_ROW_TILE = 256
DILATED_Q_TILE = 1024
MLA_Q_TILES = 2
MLA_Q_TILE = 512
MLA_K_TILE = 256

QK_SCALE_LOG2E = np.float32((QK_NOPE + QK_ROPE) ** -0.5 * math.log2(math.e))


def _cparams(sem):
    return pltpu.CompilerParams(dimension_semantics=sem, vmem_limit_bytes=VMEM_LIMIT)


def _resident(shape):
    nd = len(shape)
    return pl.BlockSpec(shape, lambda *_: (0,) * nd, pipeline_mode=pl.Buffered(1))


def _layer_norm(y, g, b):
    mu = jnp.mean(y, axis=-1, keepdims=True)
    yc = y - mu
    var = jnp.mean(yc * yc, axis=-1, keepdims=True)
    return yc * lax.rsqrt(var + LN_EPS) * g + b


def _dot(a, b):
    return jnp.dot(a, b, preferred_element_type=F32)


def _dot_nt(a, b):
    return lax.dot_general(a, b, (((1,), (1,)), ((), ())), preferred_element_type=F32)


def _dot_tn(a, b):
    return lax.dot_general(a, b, (((0,), (0,)), ((), ())), preferred_element_type=F32)


def _qkv_proj_kernel(x_ref, w_ref, *refs, tm):
    outs, x_planes, x_perm = refs[:A_GROUPS], refs[A_GROUPS], refs[A_GROUPS + 1]
    x = x_ref[0]
    n_planes = x.shape[1] // LANES
    for h in range(n_planes):
        x_planes[h] = x[:, h * LANES:(h + 1) * LANES]
    for g, dil in enumerate(A_DILATIONS):
        n = tm // dil
        if dil == 1:
            xb = x.astype(BF16)
        else:
            xp = x_perm.at[g - 1]
            for r in range(dil):
                for h in range(n_planes):
                    xp[r * n:(r + 1) * n, h * LANES:(h + 1) * LANES] = x_planes[h, pl.ds(r, n, stride=dil), :]
            xb = xp[...].astype(BF16)
        for c in range(3):
            n0 = (g * 3 + c) * A_WIDTH
            cols = slice(c * A_WIDTH, (c + 1) * A_WIDTH)
            res = _dot(xb, w_ref[:, n0:n0 + A_WIDTH])
            if c == 0:
                res = res * np.float32(A_HEAD_DIM ** -0.5)
            res = res.astype(BF16)
            for r in range(dil):
                outs[g][0, r, :, cols] = res[r * n:(r + 1) * n]


def _qkv_proj(x3d, w_bf16, tm):
    b, s, d = x3d.shape
    width = 3 * A_WIDTH
    return pl.pallas_call(
        functools.partial(_qkv_proj_kernel, tm=tm),
        grid=(b, s // tm),
        in_specs=[pl.BlockSpec((1, tm, d), lambda bi, i: (bi, i, 0)), _resident(w_bf16.shape)],
        out_specs=[pl.BlockSpec((1, dil, tm // dil, width), lambda bi, i: (bi, 0, i, 0))
                   for dil in A_DILATIONS],
        out_shape=[jax.ShapeDtypeStruct((b, dil, s // dil, width), BF16) for dil in A_DILATIONS],
        scratch_shapes=[pltpu.VMEM((d // LANES, tm, LANES), F32),
                        pltpu.VMEM((sum(dil > 1 for dil in A_DILATIONS), tm, d), F32)],
        compiler_params=_cparams(("parallel", "parallel")),
        name="qkv_proj",
    )(x3d, w_bf16)


def _t5_bucket(rel):
    nb = NUM_BUCKETS // 2
    max_exact = nb // 2
    ret = jnp.where(rel > 0, nb, 0)
    n = jnp.abs(rel)
    large = max_exact + (jnp.log(jnp.maximum(n, 1).astype(F32) / max_exact)
                         / math.log(MAX_DISTANCE / max_exact) * (nb - max_exact)).astype(jnp.int32)
    large = jnp.minimum(large, nb - 1)
    return ret + jnp.where(n < max_exact, n, large)


def _band_bucket_index():
    qi = jnp.arange(2 * A_HALF)[:, None]
    kj = jnp.arange(4 * A_HALF)[None, :]
    rel = kj - A_HALF - qi
    tiles = [jnp.where(jnp.abs(rel) <= A_HALF, _t5_bucket(rel * dil), -1) for dil in A_DILATIONS]
    return jnp.stack(tiles, axis=0).astype(jnp.int32)


def _bias_tile_kernel(tab_ref, idx_ref, o_ref):
    col = pl.program_id(0) * A_HEADS + pl.program_id(1)
    idx = idx_ref[0]
    acc = jnp.full(idx.shape, NEG_INF, F32)
    for bucket in range(NUM_BUCKETS):
        acc = jnp.where(idx == bucket, tab_ref[bucket, col], acc)
    o_ref[0, 0] = acc


def _bias_tiles(rel_bias):
    idx = _band_bucket_index()
    tq, tk = idx.shape[1:]
    return pl.pallas_call(
        _bias_tile_kernel,
        grid=(A_GROUPS, A_HEADS),
        in_specs=[pl.BlockSpec(memory_space=pltpu.SMEM),
                  pl.BlockSpec((1, tq, tk), lambda g, h: (g, 0, 0))],
        out_specs=pl.BlockSpec((1, 1, tq, tk), lambda g, h: (g, h, 0, 0)),
        out_shape=jax.ShapeDtypeStruct((A_GROUPS, A_HEADS, tq, tk), F32),
        compiler_params=_cparams(("arbitrary", "arbitrary")),
        name="bias_tiles",
    )(rel_bias, idx)


def _dilated_attn_kernel(q_ref, kl_ref, km_ref, kr_ref, vl_ref, vm_ref, vr_ref, bias_ref,
                         o_ref, lse_ref, kbuf, vbuf, *, tq, sub_len):
    li = pl.program_id(2)
    kbuf[0:A_HALF] = kl_ref[0, 0]
    kbuf[A_HALF:A_HALF + tq] = km_ref[0, 0]
    kbuf[A_HALF + tq:] = kr_ref[0, 0]
    vbuf[0:A_HALF] = vl_ref[0, 0]
    vbuf[A_HALF:A_HALF + tq] = vm_ref[0, 0]
    vbuf[A_HALF + tq:] = vr_ref[0, 0]

    qt, kt = 2 * A_HALF, 4 * A_HALF
    lane = lax.broadcasted_iota(jnp.int32, (qt, LANES), 1)
    for t in range(tq // qt):
        pos = li * tq + t * qt - A_HALF + lax.broadcasted_iota(jnp.int32, (1, kt), 1)
        edge = jnp.where((pos >= 0) & (pos < sub_len), 0.0, NEG_INF).astype(F32)
        lse_tile = jnp.zeros((qt, LANES), F32)
        for h in range(A_HEADS):
            cols = slice(h * A_HEAD_DIM, (h + 1) * A_HEAD_DIM)
            q = q_ref[0, 0, t * qt:(t + 1) * qt, cols]
            k = kbuf[t * qt:t * qt + kt, cols]
            v = vbuf[t * qt:t * qt + kt, cols]
            s = _dot_nt(q, k) + bias_ref[0, h] + edge
            m = jnp.max(s, axis=-1, keepdims=True)
            p = jnp.exp(s - m)
            den = jnp.sum(p, axis=-1, keepdims=True)
            o = _dot(p.astype(BF16), v) / den
            o_ref[0, 0, t * qt:(t + 1) * qt, cols] = o.astype(BF16)
            lse_tile = jnp.where(lane == h, m + jnp.log(den), lse_tile)
        lse_ref[0, 0, t * qt:(t + 1) * qt, :] = lse_tile


def _dilated_attn(qkv_g, bias, g):
    batch, dil, sub_len, _ = qkv_g.shape
    tq = min(DILATED_Q_TILE, sub_len)
    n_t = sub_len // tq
    hb = tq // A_HALF
    n_hb = sub_len // A_HALF

    main = lambda which: pl.BlockSpec((1, 1, tq, A_WIDTH), lambda b, r, i: (b, r, i, which))
    left = lambda which: pl.BlockSpec(
        (1, 1, A_HALF, A_WIDTH), lambda b, r, i: (b, r, jnp.maximum(i * hb - 1, 0), which))
    right = lambda which: pl.BlockSpec(
        (1, 1, A_HALF, A_WIDTH), lambda b, r, i: (b, r, jnp.minimum((i + 1) * hb, n_hb - 1), which))

    return pl.pallas_call(
        functools.partial(_dilated_attn_kernel, tq=tq, sub_len=sub_len),
        grid=(batch, dil, n_t),
        in_specs=[main(0), left(1), main(1), right(1), left(2), main(2), right(2),
                  pl.BlockSpec((1, A_HEADS, 2 * A_HALF, 4 * A_HALF), lambda b, r, i: (g, 0, 0, 0))],
        out_specs=[pl.BlockSpec((1, 1, tq, A_WIDTH), lambda b, r, i: (b, r, i, 0)),
                   pl.BlockSpec((1, 1, tq, LANES), lambda b, r, i: (b, r, i, 0))],
        out_shape=[jax.ShapeDtypeStruct((batch, dil, sub_len, A_WIDTH), BF16),
                   jax.ShapeDtypeStruct((batch, dil, sub_len, LANES), F32)],
        scratch_shapes=[pltpu.VMEM((tq + 2 * A_HALF, A_WIDTH), BF16),
                        pltpu.VMEM((tq + 2 * A_HALF, A_WIDTH), BF16)],
        compiler_params=_cparams(("parallel", "parallel", "parallel")),
        name=f"dilated_attn_g{g}",
    )(qkv_g, qkv_g, qkv_g, qkv_g, qkv_g, qkv_g, qkv_g, bias)


def _merge_proj_ln_kernel(o0_ref, o1_ref, o2_ref, l0_ref, l1_ref, l2_ref, x_ref, w_ref, g_ref, b_ref,
                          out_ref, merged, o_scr, l_scr, *, tm):
    for g, (o_ref, l_ref, dil) in enumerate(zip((o0_ref, o1_ref, o2_ref), (l0_ref, l1_ref, l2_ref),
                                                A_DILATIONS)):
        for r in range(dil):
            rows = pl.ds(r, tm // dil, stride=dil) if dil > 1 else slice(None)
            l_scr[g, rows, :] = l_ref[0, r]
            for h in range(A_HEADS):
                o_scr[g, h, rows, :] = o_ref[0, r, :, h * A_HEAD_DIM:(h + 1) * A_HEAD_DIM].astype(F32)
    l0, l1, l2 = l_scr[0], l_scr[1], l_scr[2]
    m = jnp.maximum(jnp.maximum(l0, l1), l2)
    e0, e1, e2 = jnp.exp(l0 - m), jnp.exp(l1 - m), jnp.exp(l2 - m)
    den = e0 + e1 + e2
    w0, w1, w2 = e0 / den, e1 / den, e2 / den
    for h in range(A_HEADS):
        cols = slice(h * A_HEAD_DIM, (h + 1) * A_HEAD_DIM)
        mix = (w0[:, h:h + 1] * o_scr[0, h] + w1[:, h:h + 1] * o_scr[1, h]
               + w2[:, h:h + 1] * o_scr[2, h])
        merged[:, cols] = mix.astype(BF16)
    y = ALPHA * x_ref[0] + _dot(merged[...], w_ref[...])
    out_ref[0] = _layer_norm(y, g_ref[...], b_ref[...])


def _merge_proj_ln(os, lses, x3d, w_bf16, ln_g, ln_b, tm):
    b, s, d = x3d.shape
    plane = lambda w: [pl.BlockSpec((1, dil, tm // dil, w), lambda bi, i: (bi, 0, i, 0))
                       for dil in A_DILATIONS]
    row = pl.BlockSpec((1, tm, d), lambda bi, i: (bi, i, 0))
    return pl.pallas_call(
        functools.partial(_merge_proj_ln_kernel, tm=tm),
        grid=(b, s // tm),
        in_specs=plane(A_WIDTH) + plane(LANES) + [row, _resident((A_WIDTH, d)),
                                                  _resident((1, d)), _resident((1, d))],
        out_specs=row,
        out_shape=jax.ShapeDtypeStruct((b, s, d), F32),
        scratch_shapes=[pltpu.VMEM((tm, A_WIDTH), BF16),
                        pltpu.VMEM((A_GROUPS, A_HEADS, tm, A_HEAD_DIM), F32),
                        pltpu.VMEM((A_GROUPS, tm, LANES), F32)],
        compiler_params=_cparams(("parallel", "parallel")),
        name="merge_proj_ln",
    )(*os, *lses, x3d, w_bf16, ln_g, ln_b)


def _ffn_ln_kernel(x_ref, xp_ref, xn_ref, win_ref, cw_ref, cb_ref, wout_ref, g_ref, b_ref,
                   out_ref, act, *, tm):
    si = pl.program_id(1)
    n_s = pl.num_programs(1)
    x = x_ref[0]
    xb = x.astype(BF16)
    prev_ok = (si > 0).astype(F32)
    next_ok = (si < n_s - 1).astype(F32)
    xe = jnp.concatenate([xp_ref[0] * prev_ok, xn_ref[0] * next_ok], axis=0).astype(BF16)
    xall = jnp.concatenate([xb, xe], axis=0)
    row = lax.broadcasted_iota(jnp.int32, (tm, 2 * FF_CHUNK), 0)
    for c in range(D_FF // FF_CHUNK):
        cols = slice(c * 2 * FF_CHUNK, (c + 1) * 2 * FF_CHUNK)
        hall = _dot(xall, win_ref[:, cols])
        h = hall[:tm]
        he = hall[tm:]
        up = jnp.where(row == 0, he[SUBLANES - 1:SUBLANES], pltpu.roll(h, 1, 0))
        dn = jnp.where(row == tm - 1, he[SUBLANES:SUBLANES + 1], pltpu.roll(h, tm - 1, 0))
        cw = cw_ref[:, cols]
        hc = up * cw[0:1] + h * cw[1:2] + dn * cw[2:3] + cb_ref[:, cols]
        a = hc[:, :FF_CHUNK]
        gate = hc[:, FF_CHUNK:]
        gelu = 0.5 * gate * (1.0 + lax.erf(gate * np.float32(math.sqrt(0.5))))
        act[:, c * FF_CHUNK:(c + 1) * FF_CHUNK] = (a * gelu).astype(BF16)
    y = ALPHA * x + _dot(act[...], wout_ref[...])
    out_ref[0] = _layer_norm(y, g_ref[...], b_ref[...])


def _chunk_interleave(w):
    lead = w.shape[:-1]
    w = w.reshape(lead + (2, D_FF // FF_CHUNK, FF_CHUNK))
    w = jnp.swapaxes(w, -3, -2)
    return w.reshape(lead + (2 * D_FF,))


def _ffn_ln(x3d, w_in, conv_w, conv_b, w_out, ln_g, ln_b, tm):
    b, s, d = x3d.shape
    nb8 = tm // SUBLANES
    n8 = s // SUBLANES
    win = _chunk_interleave(w_in).astype(BF16)
    cw = _chunk_interleave(conv_w)
    cb = _chunk_interleave(conv_b)[None]
    return pl.pallas_call(
        functools.partial(_ffn_ln_kernel, tm=tm),
        grid=(b, s // tm),
        in_specs=[pl.BlockSpec((1, tm, d), lambda bi, si: (bi, si, 0)),
                  pl.BlockSpec((1, SUBLANES, d), lambda bi, si: (bi, jnp.maximum(si * nb8 - 1, 0), 0)),
                  pl.BlockSpec((1, SUBLANES, d), lambda bi, si: (bi, jnp.minimum((si + 1) * nb8, n8 - 1), 0)),
                  _resident((d, 2 * D_FF)), _resident((3, 2 * D_FF)), _resident((1, 2 * D_FF)),
                  _resident((D_FF, d)), _resident((1, d)), _resident((1, d))],
        out_specs=pl.BlockSpec((1, tm, d), lambda bi, si: (bi, si, 0)),
        out_shape=jax.ShapeDtypeStruct((b, s, d), F32),
        scratch_shapes=[pltpu.VMEM((tm, D_FF), BF16)],
        compiler_params=_cparams(("parallel", "parallel")),
        name="ffn_ln",
    )(x3d, x3d, x3d, win, cw, cb, w_out.astype(BF16), ln_g, ln_b)


def _rope_tables(seq):
    inv = 1.0 / (ROPE_THETA ** (jnp.arange(0, QK_ROPE, 2, dtype=F32) / QK_ROPE))
    ang = jnp.arange(seq, dtype=F32)[:, None] * inv[None, :]
    cos, sin = jnp.cos(ang), jnp.sin(ang)
    half = QK_ROPE // 2
    ones = jnp.ones((seq, QK_NOPE), F32)
    z_nope = jnp.zeros((seq, QK_NOPE), F32)
    z_half = jnp.zeros((seq, half), F32)
    z_pad = jnp.zeros((seq, HEAD_PAD - QK_NOPE - QK_ROPE), F32)
    cmul = jnp.concatenate([ones, cos, cos, z_pad], axis=1)
    s_first = jnp.concatenate([z_nope, -sin, z_half, z_pad], axis=1)
    s_second = jnp.concatenate([z_nope, z_half, sin, z_pad], axis=1)
    return cmul, s_first, s_second


def _rope(t, cmul, s_first, s_second):
    half = QK_ROPE // 2
    return (t * cmul + pltpu.roll(t, HEAD_PAD - half, 1) * s_first
            + pltpu.roll(t, half, 1) * s_second)


def _rms_norm(x, g):
    ms = jnp.mean(x * x, axis=-1, keepdims=True)
    return x * lax.rsqrt(ms + RMS_EPS) * g


def _mla_proj_kernel(x_ref, wd_ref, gq_ref, gkv_ref, wuqt_ref, wuk_ref, wuvt_ref,
                     cm_ref, s1_ref, s2_ref, cmt_ref, s1t_ref, s2t_ref, qt_ref, k_ref, vt_ref):
    xb = x_ref[0].astype(BF16)
    c = _dot(xb, wd_ref[...])
    cqn = _rms_norm(c[:, :Q_LORA], gq_ref[...]).astype(BF16)
    ckvn = _rms_norm(c[:, Q_LORA:Q_LORA + KV_LORA], gkv_ref[...]).astype(BF16)
    cm, s1, s2 = cm_ref[...], s1_ref[...], s2_ref[...]
    kr = _rope(c[:, Q_LORA + KV_LORA:], cm, s1, s2)
    k = _dot(ckvn, wuk_ref[...])
    for h in range(B_HEADS):
        cols = slice(h * HEAD_PAD, (h + 1) * HEAD_PAD)
        k_ref[0, :, cols] = (k[:, cols] + kr).astype(BF16)
    qt = _dot_nt(wuqt_ref[...], cqn)
    cmt, s1t, s2t = cmt_ref[...], s1t_ref[...], s2t_ref[...]
    half = QK_ROPE // 2
    for h in range(B_HEADS):
        rows = slice(h * HEAD_PAD, (h + 1) * HEAD_PAD)
        t = qt[rows]
        rot = t * cmt + pltpu.roll(t, HEAD_PAD - half, 0) * s1t + pltpu.roll(t, half, 0) * s2t
        qt_ref[0, rows, :] = (rot * QK_SCALE_LOG2E).astype(BF16)
    vt = _dot_nt(wuvt_ref[...], ckvn).astype(BF16)
    tm = vt.shape[1]
    row = lax.broadcasted_iota(jnp.int32, (V_ROWS - V_HEAD, tm), 0)
    ones_row = jnp.where(row == 0, 1.0, 0.0).astype(BF16)
    for h in range(B_HEADS):
        vt_ref[0, h * V_ROWS:h * V_ROWS + V_HEAD, :] = vt[h * V_HEAD:(h + 1) * V_HEAD]
        vt_ref[0, h * V_ROWS + V_HEAD:(h + 1) * V_ROWS, :] = ones_row


def _mla_weights(w_dkv, w_uq, w_ukv):
    d = w_dkv.shape[0]
    pad_l = jnp.zeros((d, QK_NOPE), F32)
    pad_r = jnp.zeros((d, HEAD_PAD - QK_NOPE - QK_ROPE), F32)
    wd = jnp.concatenate([w_dkv[:, :Q_LORA + KV_LORA], pad_l, w_dkv[:, Q_LORA + KV_LORA:], pad_r], axis=1)
    wq = w_uq.reshape(Q_LORA, B_HEADS, QK_NOPE + QK_ROPE)
    wq = jnp.pad(wq, ((0, 0), (0, 0), (0, HEAD_PAD - QK_NOPE - QK_ROPE))).reshape(Q_LORA, B_HEADS * HEAD_PAD)
    wkv = w_ukv.reshape(KV_LORA, B_HEADS, QK_NOPE + V_HEAD)
    wk = jnp.pad(wkv[:, :, :QK_NOPE], ((0, 0), (0, 0), (0, HEAD_PAD - QK_NOPE))).reshape(KV_LORA, B_HEADS * HEAD_PAD)
    wvt = wkv[:, :, QK_NOPE:].reshape(KV_LORA, B_HEADS * V_HEAD).T
    return wd.astype(BF16), wq.T.astype(BF16), wk.astype(BF16), wvt.astype(BF16)


def _mla_proj(x3d, w_dkv, g_q, g_kv, w_uq, w_ukv, tm):
    b, s, d = x3d.shape
    wd, wqt, wk, wvt = _mla_weights(w_dkv, w_uq, w_ukv)
    tabs = _rope_tables(s)
    tabs_t = [t.T for t in tabs]
    hq = B_HEADS * HEAD_PAD
    hv = B_HEADS * V_ROWS
    tab = pl.BlockSpec((tm, HEAD_PAD), lambda bi, si: (si, 0))
    tab_t = pl.BlockSpec((HEAD_PAD, tm), lambda bi, si: (0, si))
    return pl.pallas_call(
        _mla_proj_kernel,
        grid=(b, s // tm),
        in_specs=[pl.BlockSpec((1, tm, d), lambda bi, si: (bi, si, 0)),
                  _resident(wd.shape), _resident((1, Q_LORA)), _resident((1, KV_LORA)),
                  _resident(wqt.shape), _resident(wk.shape), _resident(wvt.shape),
                  tab, tab, tab, tab_t, tab_t, tab_t],
        out_specs=[pl.BlockSpec((1, hq, tm), lambda bi, si: (bi, 0, si)),
                   pl.BlockSpec((1, tm, hq), lambda bi, si: (bi, si, 0)),
                   pl.BlockSpec((1, hv, tm), lambda bi, si: (bi, 0, si))],
        out_shape=[jax.ShapeDtypeStruct((b, hq, s), BF16),
                   jax.ShapeDtypeStruct((b, s, hq), BF16),
                   jax.ShapeDtypeStruct((b, hv, s), BF16)],
        compiler_params=_cparams(("parallel", "parallel")),
        name="mla_proj",
    )(x3d, wd, g_q[None], g_kv[None], wqt, wk, wvt, *tabs, *tabs_t)


def _mla_attn_kernel(qt_ref, k_ref, vt_ref, o_ref, s_scr, cmax_scr, m_all, acc_all, *, tq, tk, n_q, unroll):
    seq = k_ref.shape[1]
    nk = seq // tk
    blk_q = n_q * tq
    n_blk = seq // blk_q
    total = n_blk * nk
    trips_per_blk = nk // unroll

    def finalize(blk):
        par = blk % 2
        for j in range(n_q):
            acc = acc_all[par, j]
            q0 = pl.multiple_of(blk * blk_q + j * tq, tq)
            o_ref[0, :, pl.ds(q0, tq)] = (acc[:V_HEAD] / acc[V_HEAD:V_HEAD + 1]).astype(BF16)

    def run_step(t, t_static, do_qk=True):
        s_qk, s_sm = t_static % N_SLOTS, (t_static - 1) % N_SLOTS
        n_sm = t - 1 + nk
        par_sm, first_sm = (n_sm // nk + 1) % 2, (n_sm % nk) == 0
        v0 = pl.multiple_of((n_sm % nk) * tk, tk)
        vt = vt_ref[0, :, pl.ds(v0, tk)]
        if do_qk:
            k0 = pl.multiple_of((t % nk) * tk, tk)
            k = k_ref[0, pl.ds(k0, tk), :]
            for j in range(n_q):
                q0 = pl.multiple_of((t // nk) * blk_q + j * tq, tq)
                sc = _dot(k, qt_ref[0, :, pl.ds(q0, tq)])
                s_scr[s_qk, j] = sc
                cmax_scr[s_qk, j] = jnp.max(sc, axis=0, keepdims=True)
        for j in range(n_q):
            m = jnp.where(first_sm, NEG_INF, m_all[par_sm, j])
            m_new = jnp.maximum(m, cmax_scr[s_sm, j])
            alpha = jnp.exp2(m - m_new)
            m_all[par_sm, j] = m_new
            p = jnp.exp2(s_scr[s_sm, j] - m_new).astype(BF16)
            acc_all[par_sm, j] = alpha * acc_all[par_sm, j] + _dot(vt, p)

    s_scr[...] = jnp.zeros(s_scr.shape, F32)
    cmax_scr[...] = jnp.zeros(cmax_scr.shape, F32)
    m_all[...] = jnp.full(m_all.shape, NEG_INF, F32)
    acc_all[...] = jnp.zeros(acc_all.shape, F32)

    def trip(u, carry):
        @pl.when((u % trips_per_blk == 1) & (u > trips_per_blk))
        def _():
            finalize(u // trips_per_blk - 1)

        for i in range(unroll):
            run_step(unroll * u + i, i)
        return carry

    lax.fori_loop(0, total // unroll, trip, 0)
    run_step(total, total, do_qk=False)
    finalize(n_blk - 1)


def _mla_attn(qt, k, vt, n_q, tq, tk):
    b, s, _ = k.shape
    nk = s // tk
    unroll = max(u for u in range(N_SLOTS, MAX_UNROLL + 1, N_SLOTS) if nk % u == 0 and nk // u >= 2)
    assert s % (n_q * tq) == 0 and N_SLOTS == 2
    return pl.pallas_call(
        functools.partial(_mla_attn_kernel, tq=tq, tk=tk, n_q=n_q, unroll=unroll),
        grid=(b, B_HEADS),
        in_specs=[pl.BlockSpec((1, HEAD_PAD, s), lambda bi, h: (bi, h, 0)),
                  pl.BlockSpec((1, s, HEAD_PAD), lambda bi, h: (bi, 0, h)),
                  pl.BlockSpec((1, V_ROWS, s), lambda bi, h: (bi, h, 0))],
        out_specs=pl.BlockSpec((1, V_HEAD, s), lambda bi, h: (bi, h, 0)),
        out_shape=jax.ShapeDtypeStruct((b, B_HEADS * V_HEAD, s), BF16),
        scratch_shapes=[pltpu.VMEM((N_SLOTS, n_q, tk, tq), F32), pltpu.VMEM((N_SLOTS, n_q, 1, tq), F32),
                        pltpu.VMEM((2, n_q, 1, tq), F32), pltpu.VMEM((2, n_q, V_ROWS, tq), F32)],
        compiler_params=_cparams(("parallel", "parallel")),
        name="mla_attn",
    )(qt, k, vt)


def _proj_t_ln_kernel(ot_ref, x_ref, w_ref, g_ref, b_ref, out_ref):
    y = ALPHA * x_ref[0] + _dot_tn(ot_ref[0], w_ref[...])
    out_ref[0] = _layer_norm(y, g_ref[...], b_ref[...])


def _proj_t_ln(ot, x3d, w_bf16, ln_g, ln_b, tm):
    b, s, d = x3d.shape
    k = ot.shape[1]
    return pl.pallas_call(
        _proj_t_ln_kernel,
        grid=(b, s // tm),
        in_specs=[pl.BlockSpec((1, k, tm), lambda bi, si: (bi, 0, si)),
                  pl.BlockSpec((1, tm, d), lambda bi, si: (bi, si, 0)),
                  _resident((k, d)), _resident((1, d)), _resident((1, d))],
        out_specs=pl.BlockSpec((1, tm, d), lambda bi, si: (bi, si, 0)),
        out_shape=jax.ShapeDtypeStruct((b, s, d), F32),
        compiler_params=_cparams(("parallel", "parallel")),
        name="proj_t_ln",
    )(ot, x3d, w_bf16, ln_g, ln_b)


def _encoder(x, bias, w_qkv_a, w_o_a, w_dkv_b, g_q_b, g_kv_b, w_uq_b, w_ukv_b, w_o_b,
             ffn_w_in, ffn_conv_w, ffn_conv_b, ffn_w_out, ln_g, ln_b):
    b, s, d = x.shape
    tm = ROW_TILE
    for i in range(DEPTH):
        j = i // N_MIXERS
        g1, b1 = ln_g[i, 0][None], ln_b[i, 0][None]
        g2, b2 = ln_g[i, 1][None], ln_b[i, 1][None]
        if i % N_MIXERS == 0:
            qkv = _qkv_proj(x, w_qkv_a[j].astype(BF16), QKV_ROW_TILE)
            outs = [_dilated_attn(qkv[g], bias, g) for g in range(A_GROUPS)]
            x = _merge_proj_ln([o for o, _ in outs], [l for _, l in outs], x,
                               w_o_a[j].astype(BF16), g1, b1, tm)
        else:
            q, k, vt = _mla_proj(x, w_dkv_b[j], g_q_b[j], g_kv_b[j], w_uq_b[j], w_ukv_b[j], tm)
            ot = _mla_attn(q, k, vt, MLA_Q_TILES, MLA_Q_TILE, MLA_K_TILE)
            x = _proj_t_ln(ot, x, w_o_b[j].astype(BF16), g1, b1, tm)
        x = _ffn_ln(x, ffn_w_in[i], ffn_conv_w[i], ffn_conv_b[i], ffn_w_out[i], g2, b2, tm)
    return x


def kernel(x_prompt, x_sample, rel_bias, w_qkv_a, w_o_a, w_dkv_b, g_q_b, g_kv_b, w_uq_b, w_ukv_b, w_o_b,
           ffn_w_in, ffn_conv_w, ffn_conv_b, ffn_w_out, ln_g, ln_b):
    bias = _bias_tiles(rel_bias)
    args = (bias, w_qkv_a, w_o_a, w_dkv_b, g_q_b, g_kv_b, w_uq_b, w_ukv_b, w_o_b,
            ffn_w_in, ffn_conv_w, ffn_conv_b, ffn_w_out, ln_g, ln_b)
    return (_encoder(x_prompt, *args), _encoder(x_sample, *args))
```
